```python
import math
import jax, jax.numpy as jnp
from jax import lax
import numpy as np

D_MODEL = 1024
BATCH = 16
SEQ = 256
DEPTH = 1
DEC_BATCH = 8
DEC_SEQ = 4096
PAST_LEN = 256

GRID_W = 64
MLSTM_WIDTH = 1024
MLSTM_HEADS = 4
MLSTM_HEAD_DIM = MLSTM_WIDTH // MLSTM_HEADS
MLSTM_CHUNK = 64
S5_WIDTH = 512
S5_GROUP = 16
S5_GROUPS = S5_WIDTH // S5_GROUP
S5_STATE = 64
N_EXPERTS = 16
EXPERT_FF = 1536
CAPACITY_FACTOR = 2
N_DIRS = 2
N_MOD = 6
EPS = 1e-6
IN_SPLITS = [MLSTM_WIDTH, 2 * MLSTM_WIDTH, 3 * MLSTM_WIDTH, 4 * MLSTM_WIDTH,
             4 * MLSTM_WIDTH + 4 * MLSTM_HEADS,
             4 * MLSTM_WIDTH + 4 * MLSTM_HEADS + S5_WIDTH,
             4 * MLSTM_WIDTH + 4 * MLSTM_HEADS + S5_WIDTH + D_MODEL]
IN_COLS = 4 * MLSTM_WIDTH + 4 * MLSTM_HEADS + S5_WIDTH + 2 * D_MODEL

kernel_name = "hybrid_mlstm_s5_ec_diffusion_step"


def rmsnorm(x, g):
    x32 = x.astype(jnp.float32)
    y = x32 * lax.rsqrt(jnp.mean(x32 * x32, axis=-1, keepdims=True) + EPS)
    return (y * g.astype(jnp.float32)).astype(x.dtype)


def adaln(cvec, w, b):
    m = jax.nn.silu(cvec) @ w + b
    return m.reshape(cvec.shape[0], N_MOD, D_MODEL)


def modulate(x, shift, scale):
    return x * (1 + scale[:, None]) + shift[:, None]


def pos2d(T, dtype):
    rows = T // GRID_W
    r, cl = jnp.meshgrid(jnp.arange(rows, dtype=jnp.float32), jnp.arange(GRID_W, dtype=jnp.float32), indexing='ij')
    r, cl = r.reshape(-1), cl.reshape(-1)
    quarter = D_MODEL // 4
    freqs = 1.0 / (10000.0 ** (jnp.arange(quarter, dtype=jnp.float32) / quarter))
    er, ec = r[:, None] * freqs, cl[:, None] * freqs
    return jnp.concatenate([jnp.sin(er), jnp.cos(er), jnp.sin(ec), jnp.cos(ec)], axis=-1).astype(dtype)


def mlstm_chunkwise(q, k, v, ig, lf, C0, n0, m0):
    B, H, T, dh = q.shape
    nc = T // MLSTM_CHUNK

    def chunks(a):
        return jnp.moveaxis(a.reshape(a.shape[:2] + (nc, MLSTM_CHUNK) + a.shape[3:]), 2, 0)

    causal = jnp.tril(jnp.ones((MLSTM_CHUNK, MLSTM_CHUNK), dtype=bool))

    def step(carry, inp):
        C, n, m = carry
        qc, kc, vc, ic, fc = inp
        b = jnp.cumsum(fc, axis=-1)
        dmat = jnp.where(causal, b[..., :, None] - b[..., None, :] + ic[..., None, :], -jnp.inf)
        m_inter = b + m[..., None]
        m_t = jnp.maximum(m_inter, jnp.max(dmat, axis=-1))
        w_inter = jnp.exp(m_inter - m_t)
        s = jnp.einsum('bhtd,bhsd->bhts', qc, kc) * jnp.exp(dmat - m_t[..., None])
        num = w_inter[..., None] * jnp.einsum('bhtd,bhde->bhte', qc, C) + jnp.einsum('bhts,bhse->bhte', s, vc)
        den = w_inter * jnp.einsum('bhtd,bhd->bht', qc, n) + jnp.sum(s, axis=-1)
        h = num / jnp.maximum(jnp.abs(den), jnp.exp(-m_t))[..., None]
        b_last = b[..., -1]
        g = b_last[..., None] - b + ic
        m_new = jnp.maximum(b_last + m, jnp.max(g, axis=-1))
        decay = jnp.exp(b_last + m - m_new)
        wk = jnp.exp(g - m_new[..., None])
        C_new = decay[..., None, None] * C + jnp.einsum('bhs,bhsd,bhse->bhde', wk, kc, vc)
        n_new = decay[..., None] * n + jnp.einsum('bhs,bhsd->bhd', wk, kc)
        return (C_new, n_new, m_new), h

    (C, n, m), hs = lax.scan(step, (C0, n0, m0), (chunks(q), chunks(k), chunks(v), chunks(ig), chunks(lf)))
    h = jnp.moveaxis(hs, 0, 2).reshape(B, H, T, dh)
    return h, C, n, m


def s5_scan(u, a_re, a_im, log_dt, b_c, c_c, x0):
    A = lax.complex(a_re.astype(jnp.float32), a_im.astype(jnp.float32))
    dt = jnp.exp(log_dt.astype(jnp.float32))[:, None]
    a_bar = jnp.exp(A * dt)
    b_bar = ((a_bar - 1.0) / A)[..., None] * b_c
    bu = jnp.einsum('gpc,btgc->btgp', b_bar, u.astype(jnp.complex64))
    bu = bu.at[:, 0].add(a_bar * x0)
    a = jnp.broadcast_to(a_bar, bu.shape)

    def combine(l, r):
        return (r[0] * l[0], r[0] * l[1] + r[1])

    _, xs = lax.associative_scan(combine, (a, bu), axis=1)
    y = jnp.real(jnp.einsum('gcp,btgp->btgc', c_c, xs))
    return y, xs[:, -1]


def mixer(h, C0, n0, m0, s0, p):
    B, T, _ = h.shape
    f32 = jnp.float32
    proj = h @ p['w_in']
    q, k, v, o, gates, u, ga, gb = jnp.split(proj, IN_SPLITS, axis=-1)

    def heads(a):
        return a.reshape(B, T, MLSTM_HEADS, MLSTM_HEAD_DIM).transpose(0, 2, 1, 3).astype(f32)

    def flip_t(a):
        return jnp.flip(a, axis=2)

    qh = heads(q) * (MLSTM_HEAD_DIM ** -0.5)
    kh, vh = heads(k), heads(v)
    gt = (gates.reshape(B, T, 4, MLSTM_HEADS).astype(f32) + p['b_gate'].astype(f32)).transpose(0, 2, 3, 1)
    hf, Cf, nf, mf = mlstm_chunkwise(qh, kh, vh, gt[:, 0], jax.nn.log_sigmoid(gt[:, 1]),
                                     C0[:, 0], n0[:, 0], m0[:, 0])
    hb, Cb, nb, mb = mlstm_chunkwise(flip_t(qh), flip_t(kh), flip_t(vh), flip_t(gt[:, 2]),
                                     flip_t(jax.nn.log_sigmoid(gt[:, 3])), C0[:, 1], n0[:, 1], m0[:, 1])
    hm = (hf + flip_t(hb)) * jax.nn.sigmoid(heads(o))
    hm = hm * lax.rsqrt(jnp.mean(hm * hm, axis=-1, keepdims=True) + EPS) * p['head_norm'].astype(f32)[None, :, None, :]
    h_a = hm.transpose(0, 2, 1, 3).reshape(B, T, MLSTM_WIDTH).astype(h.dtype)

    u32 = u.astype(f32).reshape(B, T, S5_GROUPS, S5_GROUP)
    b_c = lax.complex(p['s5_b_re'].astype(f32), p['s5_b_im'].astype(f32))
    c_c = lax.complex(p['s5_c_re'].astype(f32), p['s5_c_im'].astype(f32))
    yf, xf = s5_scan(u32, p['s5_a_re'][0], p['s5_a_im'][0], p['s5_log_dt'][0], b_c, c_c[0], s0[:, 0])
    yb, xb = s5_scan(jnp.flip(u32, axis=1), p['s5_a_re'][1], p['s5_a_im'][1], p['s5_log_dt'][1], b_c, c_c[1], s0[:, 1])
    z = (yf + jnp.flip(yb, axis=1)).reshape(B, T, S5_WIDTH) + p['s5_d'].astype(f32) * u32.reshape(B, T, S5_WIDTH)
    h_b = (jax.nn.gelu(z) * jax.nn.sigmoid(z @ p['w_glu'].astype(f32) + p['b_glu'].astype(f32))).astype(h.dtype)

    merged = jax.nn.sigmoid(ga) * (h_a @ p['w_pA']) + jax.nn.sigmoid(gb) * (h_b @ p['w_pB'])
    out = merged @ p['w_out']
    C = jnp.stack([Cf, Cb], axis=1)
    n = jnp.stack([nf, nb], axis=1)
    m = jnp.stack([mf, mb], axis=1)
    s = jnp.stack([xf, xb], axis=1)
    return out, C, n, m, s


def ec_moe(h, w_router, b_router, w1, w3, w2):
    B, T, D = h.shape
    cap = CAPACITY_FACTOR * T // N_EXPERTS
    aff = jax.nn.softmax((h @ w_router).astype(jnp.float32) + b_router.astype(jnp.float32), axis=-1)
    topv, topi = lax.top_k(jnp.swapaxes(aff, 1, 2), cap)
    xe = jax.vmap(lambda hb, ib: hb[ib])(h, topi)
    hid = jax.nn.silu(jnp.einsum('becd,edf->becf', xe, w1)) * jnp.einsum('becd,edf->becf', xe, w3)
    ye = jnp.einsum('becf,efd->becd', hid, w2) * topv[..., None].astype(h.dtype)
    return jax.vmap(lambda ib, yb: jnp.zeros((T, D), h.dtype).at[ib.reshape(-1)].add(yb.reshape(-1, D)))(topi, ye)


def layer_stream(x, mod, C0, n0, m0, s0, p):
    shift1, scale1, gate1, shift2, scale2, gate2 = (mod[:, i] for i in range(N_MOD))
    h = modulate(rmsnorm(x, p['norm_mix']), shift1, scale1)
    out, C, n, m, s = mixer(h, C0, n0, m0, s0, p)
    x = x + gate1[:, None] * out
    h = modulate(rmsnorm(x, p['norm_ffn']), shift2, scale2)
    x = x + gate2[:, None] * ec_moe(h, p['w_router'], p['b_router'], p['w_e1'], p['w_e3'], p['w_e2'])
    return x, C, n, m, s


def setup_inputs(seed: int = 0) -> dict:
    key = jax.random.key(seed)
    ks = jax.random.split(key, 40)
    f32 = jnp.float32

    def nrm(k, shape, s):
        return jax.random.normal(k, shape, f32) * s

    H, DH, G, P = MLSTM_HEADS, MLSTM_HEAD_DIM, S5_GROUPS, S5_STATE
    fbias = jnp.linspace(3.0, 6.0, H, dtype=f32)
    base_gate = jnp.stack([jnp.zeros((H,), f32), fbias, jnp.zeros((H,), f32), fbias])
    return {
        "x_prompt": nrm(ks[0], (BATCH, SEQ, D_MODEL), 1.0),
        "x_sample": nrm(ks[1], (DEC_BATCH, DEC_SEQ, D_MODEL), 1.0),
        "state_mlstm_C": nrm(ks[2], (DEC_BATCH, DEPTH, N_DIRS, H, DH, DH), 0.1),
        "state_mlstm_n": nrm(ks[3], (DEC_BATCH, DEPTH, N_DIRS, H, DH), 0.5),
        "state_mlstm_m": nrm(ks[4], (DEC_BATCH, DEPTH, N_DIRS, H), 1.0),
        "state_s5_re": nrm(ks[5], (DEC_BATCH, DEPTH, N_DIRS, G, P), 0.1),
        "state_s5_im": nrm(ks[6], (DEC_BATCH, DEPTH, N_DIRS, G, P), 0.1),
        "c": nrm(ks[7], (DEC_BATCH, D_MODEL), 1.0),
        "c_ctx": nrm(ks[8], (D_MODEL,), 1.0),
        "w_ada": nrm(ks[9], (DEPTH, D_MODEL, N_MOD * D_MODEL), D_MODEL ** -0.5),
        "b_ada": nrm(ks[10], (DEPTH, N_MOD * D_MODEL), 0.01),
        "norm_mix": 1.0 + nrm(ks[11], (DEPTH, D_MODEL), 0.02),
        "norm_ffn": 1.0 + nrm(ks[12], (DEPTH, D_MODEL), 0.02),
        "w_in": nrm(ks[13], (DEPTH, D_MODEL, IN_COLS), D_MODEL ** -0.5),
        "b_gate": base_gate[None] + nrm(ks[14], (DEPTH, 4, H), 0.1),
        "head_norm": 1.0 + nrm(ks[15], (DEPTH, H, DH), 0.02),
        "s5_a_re": -0.5 + nrm(ks[16], (DEPTH, N_DIRS, G, P), 0.01),
        "s5_a_im": jnp.pi * jnp.arange(P, dtype=f32) + nrm(ks[17], (DEPTH, N_DIRS, G, P), 0.01),
        "s5_log_dt": jax.random.uniform(ks[18], (DEPTH, N_DIRS, G), f32, math.log(1e-3), math.log(1e-1)),
        "s5_b_re": nrm(ks[19], (DEPTH, G, P, S5_GROUP), (2 * S5_GROUP) ** -0.5),
        "s5_b_im": nrm(ks[20], (DEPTH, G, P, S5_GROUP), (2 * S5_GROUP) ** -0.5),
        "s5_c_re": nrm(ks[21], (DEPTH, N_DIRS, G, S5_GROUP, P), (2 * P) ** -0.5),
        "s5_c_im": nrm(ks[22], (DEPTH, N_DIRS, G, S5_GROUP, P), (2 * P) ** -0.5),
        "s5_d": nrm(ks[23], (DEPTH, S5_WIDTH), 1.0),
        "w_glu": nrm(ks[24], (DEPTH, S5_WIDTH, S5_WIDTH), S5_WIDTH ** -0.5),
        "b_glu": nrm(ks[25], (DEPTH, S5_WIDTH), 0.01),
        "w_pA": nrm(ks[26], (DEPTH, MLSTM_WIDTH, D_MODEL), MLSTM_WIDTH ** -0.5),
        "w_pB": nrm(ks[27], (DEPTH, S5_WIDTH, D_MODEL), S5_WIDTH ** -0.5),
        "w_out": nrm(ks[28], (DEPTH, D_MODEL, D_MODEL), D_MODEL ** -0.5),
        "w_router": nrm(ks[29], (DEPTH, D_MODEL, N_EXPERTS), D_MODEL ** -0.5),
        "b_router": nrm(ks[30], (DEPTH, N_EXPERTS), 0.01),
        "w_e1": nrm(ks[31], (DEPTH, N_EXPERTS, D_MODEL, EXPERT_FF), D_MODEL ** -0.5),
        "w_e3": nrm(ks[32], (DEPTH, N_EXPERTS, D_MODEL, EXPERT_FF), D_MODEL ** -0.5),
        "w_e2": nrm(ks[33], (DEPTH, N_EXPERTS, EXPERT_FF, D_MODEL), EXPERT_FF ** -0.5),
        "norm_final": 1.0 + nrm(ks[34], (D_MODEL,), 0.02),
    }


def reference(x_prompt, x_sample, state_mlstm_C, state_mlstm_n, state_mlstm_m, state_s5_re, state_s5_im,
              c, c_ctx, w_ada, b_ada, norm_mix, norm_ffn, w_in, b_gate, head_norm,
              s5_a_re, s5_a_im, s5_log_dt, s5_b_re, s5_b_im, s5_c_re, s5_c_im, s5_d, w_glu, b_glu,
              w_pA, w_pB, w_out, w_router, b_router, w_e1, w_e3, w_e2, norm_final):
    f32 = jnp.float32
    Bp = x_prompt.shape[0]
    xp = x_prompt
    xs = x_sample + pos2d(x_sample.shape[1], x_sample.dtype)[None]
    new_C, new_n, new_m, new_re, new_im = [], [], [], [], []
    for l in range(DEPTH):
        p = {"norm_mix": norm_mix[l], "norm_ffn": norm_ffn[l], "w_in": w_in[l], "b_gate": b_gate[l],
             "head_norm": head_norm[l], "s5_a_re": s5_a_re[l], "s5_a_im": s5_a_im[l],
             "s5_log_dt": s5_log_dt[l], "s5_b_re": s5_b_re[l], "s5_b_im": s5_b_im[l],
             "s5_c_re": s5_c_re[l], "s5_c_im": s5_c_im[l], "s5_d": s5_d[l], "w_glu": w_glu[l],
             "b_glu": b_glu[l], "w_pA": w_pA[l], "w_pB": w_pB[l], "w_out": w_out[l],
             "w_router": w_router[l], "b_router": b_router[l], "w_e1": w_e1[l], "w_e3": w_e3[l],
             "w_e2": w_e2[l]}
        mod_ctx = adaln(c_ctx[None], w_ada[l], b_ada[l])
        mod_lat = adaln(c, w_ada[l], b_ada[l])
        z_C = jnp.zeros((Bp, N_DIRS, MLSTM_HEADS, MLSTM_HEAD_DIM, MLSTM_HEAD_DIM), f32)
        z_n = jnp.zeros((Bp, N_DIRS, MLSTM_HEADS, MLSTM_HEAD_DIM), f32)
        z_m = jnp.zeros((Bp, N_DIRS, MLSTM_HEADS), f32)
        z_s = jnp.zeros((Bp, N_DIRS, S5_GROUPS, S5_STATE), jnp.complex64)
        xp, C_l, n_l, m_l, s_l = layer_stream(xp, mod_ctx, z_C, z_n, z_m, z_s, p)
        new_C.append(C_l)
        new_n.append(n_l)
        new_m.append(m_l)
        new_re.append(jnp.real(s_l))
        new_im.append(jnp.imag(s_l))
        s0 = lax.complex(state_s5_re[:, l].astype(f32), state_s5_im[:, l].astype(f32))
        xs, _, _, _, _ = layer_stream(xs, mod_lat, state_mlstm_C[:, l].astype(f32),
                                      state_mlstm_n[:, l].astype(f32), state_mlstm_m[:, l].astype(f32), s0, p)
    y_prompt = rmsnorm(xp, norm_final)
    y_sample = rmsnorm(xs, norm_final)
    dt = x_prompt.dtype
    out_C = jnp.stack(new_C, axis=1).astype(dt)
    out_n = jnp.stack(new_n, axis=1).astype(dt)
    out_m = jnp.stack(new_m, axis=1).astype(dt)
    out_re = jnp.stack(new_re, axis=1).astype(dt)
    out_im = jnp.stack(new_im, axis=1).astype(dt)
    return (y_prompt, y_sample, out_C, out_n, out_m, out_re, out_im)
```

```python
import functools
import math

import jax
import jax.numpy as jnp
from jax import lax
from jax.experimental import pallas as pl
from jax.experimental.pallas import tpu as pltpu

F32 = jnp.float32
BF16 = jnp.bfloat16
EPS = 1e-6

MLSTM_HEADS = 4
MLSTM_HEAD_DIM = 256
N_EXPERTS = 16
CAPACITY_FACTOR = 2
S5_GROUP = 16
S5_STATE = 64
GRID_W = 64
N_MOD = 6

LANES = 128
SUBLANES = 8
VMEM_LIMIT_BYTES = 60000 * 1024

MLSTM_CHUNK = 256
S5_CHUNK = 64
S5_LANE_GROUPS = 4
EXPERT_FF_SPLIT = 2
MOE_TABLE_ROWS = 4096


def _cparams(sem):
    return pltpu.CompilerParams(dimension_semantics=sem, vmem_limit_bytes=VMEM_LIMIT_BYTES)


def _sigmoid(x):
    return jax.nn.sigmoid(x)


def _log_sigmoid(x):
    return jnp.minimum(x, 0.0) - jnp.log1p(jnp.exp(-jnp.abs(x)))


def _adaln_kernel(c_ref, w_ref, b_ref, o_ref):
    c = c_ref[...]
    s = c * _sigmoid(c)
    o_ref[...] = jnp.dot(s, w_ref[...], precision=lax.Precision.HIGHEST,
                         preferred_element_type=F32) + b_ref[...]


def _adaln(cpad, w, b):
    rows, d = cpad.shape
    n = w.shape[1]
    tn = 1536
    return pl.pallas_call(
        _adaln_kernel,
        grid=(n // tn,),
        in_specs=[pl.BlockSpec((rows, d), lambda j: (0, 0)),
                  pl.BlockSpec((d, tn), lambda j: (0, j)),
                  pl.BlockSpec((1, tn), lambda j: (0, j))],
        out_specs=pl.BlockSpec((rows, tn), lambda j: (0, j)),
        out_shape=jax.ShapeDtypeStruct((rows, n), F32),
        compiler_params=_cparams(("arbitrary",)),
        name="adaln",
    )(cpad, w, b.reshape(1, n))


def _rms_mod(x, nw, shift, scale):
    ms = jnp.mean(x * x, axis=-1, keepdims=True)
    y = x * lax.rsqrt(ms + EPS) * nw
    return y * (1.0 + scale) + shift


def _inproj_kernel(has_pos, nchunk, *refs):
    if has_pos:
        x_ref, pos_ref = refs[0], refs[1]
        refs = refs[2:]
    else:
        x_ref, pos_ref = refs[0], None
        refs = refs[1:]
    (mod_ref, nw_ref, wqkv_ref, wo_ref, wug_ref, wab_ref, bg_ref,
     q_ref, k_ref, v_ref, o_ref, g_ref, gt_ref, u_ref, sa_ref, sb_ref) = refs
    x = x_ref[0]
    if has_pos:
        x = x + pos_ref[...]
    h = _rms_mod(x, nw_ref[...], mod_ref[0, 0:1, :], mod_ref[0, 1:2, :])
    hb = h.astype(BF16)
    w = MLSTM_HEADS * MLSTM_HEAD_DIM
    qkv = jnp.dot(hb, wqkv_ref[...], preferred_element_type=F32)
    q_ref[0] = (qkv[:, 0:w] * (MLSTM_HEAD_DIM ** -0.5)).astype(BF16)
    k_ref[0] = qkv[:, w:2 * w].astype(BF16)
    v_ref[0] = qkv[:, 2 * w:3 * w].astype(BF16)
    o_ref[0] = jnp.dot(hb, wo_ref[...], preferred_element_type=F32)
    ug = jnp.dot(hb, wug_ref[...], preferred_element_type=F32)
    su = u_ref.shape[-1]
    u_ref[0] = ug[:, 0:su]
    g = ug[:, su:su + LANES] + bg_ref[...]
    col = lax.broadcasted_iota(jnp.int32, g.shape, 1)
    g = jnp.where((col % 2) == 1, _log_sigmoid(g), g)
    gt = g.T
    L = gt_ref.shape[-1]
    for hh in range(MLSTM_HEADS):
        g_ref[0, hh] = g[:, 4 * hh:4 * hh + 4]
        for cc in range(nchunk):
            gt_ref[0, hh, cc] = gt[4 * hh:4 * hh + 4, cc * L:(cc + 1) * L]
    ab = jnp.dot(hb, wab_ref[...], preferred_element_type=F32)
    d = sa_ref.shape[-1]
    sa_ref[0] = _sigmoid(ab[:, 0:d])
    sb_ref[0] = _sigmoid(ab[:, d:2 * d])


def _inproj(x, pos, mod, mod_row_fn, nw, wts, tm, L):
    B, T, D = x.shape
    has_pos = pos is not None
    wqkv, wo, wug, wab, bg = wts
    su = wug.shape[1] - LANES
    nchunk = tm // L
    G8 = B // SUBLANES
    const = lambda shape: pl.BlockSpec(shape, lambda b, i: (0,) * len(shape),
                                       pipeline_mode=pl.Buffered(1))
    in_specs = [pl.BlockSpec((1, tm, D), lambda b, i: (b, i, 0))]
    args = [x]
    if has_pos:
        in_specs.append(pl.BlockSpec((tm, D), lambda b, i: (i, 0)))
        args.append(pos)
    in_specs += [pl.BlockSpec((1, N_MOD, D), lambda b, i: (mod_row_fn(b), 0, 0)),
                 const((1, D)), const(wqkv.shape), const(wo.shape), const(wug.shape),
                 const(wab.shape), const((1, LANES))]
    args += [mod, nw, wqkv, wo, wug, wab, bg]
    W = MLSTM_HEADS * MLSTM_HEAD_DIM
    tok = lambda width, dt: (jax.ShapeDtypeStruct((B, T, width), dt),
                             pl.BlockSpec((1, tm, width), lambda b, i: (b, i, 0)))
    outs = [tok(W, BF16), tok(W, BF16), tok(W, BF16), tok(W, F32),
            (jax.ShapeDtypeStruct((B, MLSTM_HEADS, T, 4), F32),
             pl.BlockSpec((1, MLSTM_HEADS, tm, 4), lambda b, i: (b, 0, i, 0))),
            (jax.ShapeDtypeStruct((B, MLSTM_HEADS, T // L, 4, L), F32),
             pl.BlockSpec((1, MLSTM_HEADS, nchunk, 4, L), lambda b, i: (b, 0, i, 0, 0))),
            (jax.ShapeDtypeStruct((G8, T, SUBLANES * su), F32),
             pl.BlockSpec((1, tm, su), lambda b, i: (b // SUBLANES, i, b % SUBLANES))),
            tok(D, F32), tok(D, F32)]
    return pl.pallas_call(
        functools.partial(_inproj_kernel, has_pos, nchunk),
        grid=(B, T // tm),
        in_specs=in_specs,
        out_specs=[o[1] for o in outs],
        out_shape=[o[0] for o in outs],
        compiler_params=_cparams(("arbitrary", "arbitrary")),
        name="inproj",
    )(*args)


def _mlstm_chunk(q, k, v, gc, gr, C_ref, n_ref, m_ref, d, reverse, low, up):
    ig_c, lf_c = gc[:, 2 * d:2 * d + 1], gc[:, 2 * d + 1:2 * d + 2]
    ig_r, lf_r = gr[2 * d:2 * d + 1, :], gr[2 * d + 1:2 * d + 2, :]
    valid, other = (up, low) if reverse else (low, up)
    C = C_ref[d]
    n = n_ref[d]
    m = m_ref[d]
    b_col = jnp.sum(jnp.where(valid, lf_r, 0.0), axis=1, keepdims=True)
    b_row = jnp.sum(jnp.where(other, lf_c, 0.0), axis=0, keepdims=True)
    total = jnp.sum(lf_r, axis=1, keepdims=True)
    dmat = jnp.where(valid, b_col - b_row + ig_r, -jnp.inf)
    m_inter = b_col + m
    m_t = jnp.maximum(m_inter, jnp.max(dmat, axis=1, keepdims=True))
    w_inter = jnp.exp(m_inter - m_t)
    p = jnp.exp(dmat - m_t)
    s = lax.dot_general(q, k, (((1,), (1,)), ((), ())), preferred_element_type=F32) * p
    num = w_inter * jnp.dot(q, C.astype(BF16), preferred_element_type=F32) \
        + jnp.dot(s.astype(BF16), v, preferred_element_type=F32)
    qn = jnp.sum(q.astype(F32) * n, axis=1, keepdims=True)
    den = w_inter * qn + jnp.sum(s, axis=1, keepdims=True)
    h = num / jnp.maximum(jnp.abs(den), jnp.exp(-m_t))
    g_col = total - b_col + ig_c
    m_new = jnp.maximum(total + m, jnp.max(g_col, axis=0, keepdims=True))
    decay = jnp.exp(total + m - m_new)
    kw = k.astype(F32) * jnp.exp(g_col - m_new)
    C_ref[d] = decay * C + lax.dot_general(kw.astype(BF16), v, (((0,), (0,)), ((), ())),
                                           preferred_element_type=F32)
    n_ref[d] = decay * n + jnp.sum(kw, axis=0, keepdims=True)
    m_ref[d] = m_new
    return h


def _mlstm_kernel(zero_init, L, *refs):
    if zero_init:
        (q_ref, k_ref, v_ref, o_ref, g_ref, gt_ref, hn_ref,
         out_ref, Co_ref, no_ref, mo_ref, C_s, n_s, m_s, hf_s, hb_s) = refs
        C_s[...] = jnp.zeros_like(C_s)
        n_s[...] = jnp.zeros_like(n_s)
        m_s[...] = jnp.zeros_like(m_s)
    else:
        (q_ref, k_ref, v_ref, o_ref, g_ref, gt_ref, hn_ref, C0_ref, n0_ref, m0_ref,
         out_ref, C_s, n_s, m_s, hf_s, hb_s) = refs
        C_s[...] = C0_ref[0, :, 0]
        n_s[...] = n0_ref[0, :, 0]
        m_s[...] = m0_ref[0, :, 0]
    T = q_ref.shape[1]
    nc = T // L
    row = lax.broadcasted_iota(jnp.int32, (L, L), 0)
    col = lax.broadcasted_iota(jnp.int32, (L, L), 1)
    low = col <= row
    up = col >= row

    def body(c, carry):
        for d in range(2):
            cc = c if d == 0 else nc - 1 - c
            r0 = pl.multiple_of(cc * L, L)
            q = q_ref[0, pl.ds(r0, L), :]
            k = k_ref[0, pl.ds(r0, L), :]
            v = v_ref[0, pl.ds(r0, L), :]
            gc = g_ref[0, 0, pl.ds(r0, L), :]
            gr = gt_ref[0, 0, cc]
            h = _mlstm_chunk(q, k, v, gc, gr, C_s, n_s, m_s, d, d == 1, low, up)
            (hf_s if d == 0 else hb_s)[pl.ds(r0, L), :] = h
        return carry

    lax.fori_loop(0, nc, body, 0)

    def fin(c, carry):
        r0 = pl.multiple_of(c * L, L)
        hm = (hf_s[pl.ds(r0, L), :] + hb_s[pl.ds(r0, L), :]) * _sigmoid(o_ref[0, pl.ds(r0, L), :])
        hm = hm * lax.rsqrt(jnp.mean(hm * hm, axis=-1, keepdims=True) + EPS) * hn_ref[0]
        out_ref[0, pl.ds(r0, L), :] = hm.astype(out_ref.dtype)
        return carry

    lax.fori_loop(0, nc, fin, 0)
    if zero_init:
        Co_ref[0, :, 0] = C_s[...]
        no_ref[0, :, 0] = n_s[...]
        mo_ref[0, :, 0] = m_s[...]


def _mlstm(q, k, v, o, g, gt, head_norm, state, L):
    B, T, W = q.shape
    H, dh = MLSTM_HEADS, MLSTM_HEAD_DIM
    zero_init = state is None
    bh = lambda width: pl.BlockSpec((1, T, width), lambda b, h: (b, 0, h))
    in_specs = [bh(dh), bh(dh), bh(dh), bh(dh),
                pl.BlockSpec((1, 1, T, 4), lambda b, h: (b, h, 0, 0)),
                pl.BlockSpec((1, 1, T // L, 4, L), lambda b, h: (b, h, 0, 0, 0)),
                pl.BlockSpec((1, 1, dh), lambda b, h: (h, 0, 0))]
    args = [q, k, v, o, g, gt, head_norm.reshape(H, 1, dh)]
    st_specs = [pl.BlockSpec((1, 2, 1, dh, dh), lambda b, h: (b, 0, h, 0, 0)),
                pl.BlockSpec((1, 2, 1, 1, dh), lambda b, h: (b, 0, h, 0, 0)),
                pl.BlockSpec((1, 2, 1, 1, 1), lambda b, h: (b, 0, h, 0, 0))]
    st_shapes = [jax.ShapeDtypeStruct((B, 2, H, dh, dh), F32),
                 jax.ShapeDtypeStruct((B, 2, H, 1, dh), F32),
                 jax.ShapeDtypeStruct((B, 2, H, 1, 1), F32)]
    out_specs = [bh(dh)]
    out_shape = [jax.ShapeDtypeStruct((B, T, W), BF16)]
    if zero_init:
        out_specs += st_specs
        out_shape += st_shapes
    else:
        C0, n0, m0 = state
        in_specs += st_specs
        args += [C0, n0.reshape(B, 2, H, 1, dh), m0.reshape(B, 2, H, 1, 1)]
    return pl.pallas_call(
        functools.partial(_mlstm_kernel, zero_init, L),
        grid=(B, H),
        in_specs=in_specs,
        out_specs=out_specs,
        out_shape=out_shape,
        scratch_shapes=[pltpu.VMEM((2, dh, dh), F32), pltpu.VMEM((2, 1, dh), F32),
                        pltpu.VMEM((2, 1, 1), F32), pltpu.VMEM((T, dh), F32), pltpu.VMEM((T, dh), F32)],
        compiler_params=_cparams(("arbitrary", "arbitrary")),
        name="mlstm",
    )(*args)


def _s5_kernel(backward, Lc, *refs):
    if backward:
        (u_ref, yf_ref, x0_ref, ar_ref, as_ref, bw_ref, cw_ref, d_ref, wg_ref, bgl_ref,
         out_ref, xl_ref, X_s, st_s) = refs
    else:
        (u_ref, x0_ref, ar_ref, as_ref, bw_ref, cw_ref, out_ref, xl_ref, X_s, st_s) = refs
    ci = pl.program_id(1)
    nj = S5_LANE_GROUPS
    half = X_s.shape[1] // (2 * nj)
    uw = u_ref.shape[-1] // nj

    @pl.when(ci == 0)
    def _():
        st_s[...] = x0_ref[0]

    u = u_ref[0]
    ub = u.astype(BF16)
    for j in range(nj):
        X_s[:, 2 * half * j:2 * half * (j + 1)] = jnp.dot(
            ub[:, uw * j:uw * (j + 1)], bw_ref[j], preferred_element_type=F32)

    def swap(x):
        parts = []
        for j in range(nj):
            parts.append(x[:, 2 * half * j + half:2 * half * (j + 1)])
            parts.append(x[:, 2 * half * j:2 * half * j + half])
        return jnp.concatenate(parts, axis=1)

    def step(i, x):
        t = (Lc - 1 - i) if backward else i
        r0 = pl.multiple_of(t * SUBLANES, SUBLANES)
        xn = ar_ref[...] * x + as_ref[...] * swap(x) + X_s[pl.ds(r0, SUBLANES), :]
        X_s[pl.ds(r0, SUBLANES), :] = xn
        return xn

    xl = lax.fori_loop(0, Lc, step, st_s[...], unroll=2)
    st_s[...] = xl
    xl_ref[0] = xl
    ow = out_ref.shape[-1] // nj
    ys = []
    for j in range(nj):
        ys.append(jnp.dot(X_s[:, 2 * half * j:2 * half * (j + 1)].astype(BF16), cw_ref[j],
                          preferred_element_type=F32))
    y = jnp.concatenate(ys, axis=1)
    if backward:
        z = yf_ref[0] + y + d_ref[...] * u
        gate = _sigmoid(jnp.dot(z.astype(BF16), wg_ref[...], preferred_element_type=F32) + bgl_ref[...])
        out_ref[0] = (jax.nn.gelu(z) * gate).astype(out_ref.dtype)
    else:
        out_ref[0] = y


def _s5_pass(backward, u2, yf, x0, ar, asg, bw, cw, glu, Lc):
    G8, R, su = u2.shape
    rows = Lc * SUBLANES
    nchunk = R // rows
    SL = x0.shape[-1]
    cidx = (lambda c: nchunk - 1 - c) if backward else (lambda c: c)
    rowspec = lambda width: pl.BlockSpec((1, rows, width), lambda g, c: (g, cidx(c), 0))
    const = lambda shape: pl.BlockSpec(shape, lambda g, c: (0,) * len(shape))
    in_specs = [rowspec(su)]
    args = [u2]
    if backward:
        in_specs.append(rowspec(su))
        args.append(yf)
    in_specs += [pl.BlockSpec((1, SUBLANES, SL), lambda g, c: (g, 0, 0)),
                 const((SUBLANES, SL)), const((SUBLANES, SL)), const(bw.shape), const(cw.shape)]
    args += [x0, ar, asg, bw, cw]
    if backward:
        d, wg, bgl = glu
        in_specs += [const((1, su)), const(wg.shape), const((1, su))]
        args += [d, wg, bgl]
    out_dt = BF16 if backward else F32
    return pl.pallas_call(
        functools.partial(_s5_kernel, backward, Lc),
        grid=(G8, nchunk),
        in_specs=in_specs,
        out_specs=[rowspec(su), pl.BlockSpec((1, SUBLANES, SL), lambda g, c: (g, 0, 0))],
        out_shape=[jax.ShapeDtypeStruct((G8, R, su), out_dt),
                   jax.ShapeDtypeStruct((G8, SUBLANES, SL), F32)],
        scratch_shapes=[pltpu.VMEM((rows, SL), F32), pltpu.VMEM((SUBLANES, SL), F32)],
        compiler_params=_cparams(("arbitrary", "arbitrary")),
        name="s5_bwd" if backward else "s5_fwd",
    )(*args)


def _s5_discretise(a_re, a_im, log_dt, b_re, b_im, c_re, c_im):
    G, P = a_re.shape
    nj = S5_LANE_GROUPS
    gl = G // nj
    dt = jnp.exp(log_dt)[:, None]
    e = jnp.exp(a_re * dt)
    ar = e * jnp.cos(a_im * dt)
    ai = e * jnp.sin(a_im * dt)
    den = a_re * a_re + a_im * a_im
    fr = ((ar - 1.0) * a_re + ai * a_im) / den
    fi = (ai * a_re - (ar - 1.0) * a_im) / den
    bbr = fr[..., None] * b_re - fi[..., None] * b_im
    bbi = fr[..., None] * b_im + fi[..., None] * b_re

    def lanes(re, im):
        return jnp.stack([re.reshape(nj, gl * P), im.reshape(nj, gl * P)], axis=1).reshape(1, -1)

    a_row = lanes(ar, ar)
    as_row = lanes(-ai, ai)
    eye = jnp.eye(gl, dtype=F32)
    C = b_re.shape[-1]

    def bmat(bb):
        x = bb.reshape(nj, gl, P, C)
        return jnp.einsum('jgpc,gh->jgchp', x, eye).reshape(nj, gl * C, gl * P)

    bw = jnp.concatenate([bmat(bbr), bmat(bbi)], axis=2).astype(BF16)

    def cmat(cc):
        x = cc.reshape(nj, gl, C, P)
        return jnp.einsum('jgcp,gh->jgphc', x, eye).reshape(nj, gl * P, gl * C)

    cw = jnp.concatenate([cmat(c_re), cmat(-c_im)], axis=1).astype(BF16)
    return a_row, as_row, bw, cw


def _s5_state_to_lanes(s_re, s_im):
    B, G, P = s_re.shape
    nj = S5_LANE_GROUPS
    x = jnp.stack([s_re.reshape(B, nj, (G // nj) * P), s_im.reshape(B, nj, (G // nj) * P)], axis=2)
    return x.reshape(B // SUBLANES, SUBLANES, 2 * G * P)


def _s5_lanes_to_state(x, G, P):
    G8 = x.shape[0]
    nj = S5_LANE_GROUPS
    x = x.reshape(G8 * SUBLANES, nj, 2, G // nj, P)
    return x[:, :, 0].reshape(-1, G, P), x[:, :, 1].reshape(-1, G, P)


def _merge_kernel(has_pos, *refs):
    if has_pos:
        x_ref, pos_ref = refs[0], refs[1]
        refs = refs[2:]
    else:
        x_ref, pos_ref = refs[0], None
        refs = refs[1:]
    (ha_ref, hb_ref, sa_ref, sb_ref, mod_ref, nw_ref, wpa_ref, wpb_ref, wout_ref, wr_ref, br_ref,
     x1_ref, hp_ref, aff_ref) = refs
    x = x_ref[0]
    if has_pos:
        x = x + pos_ref[...]
    merged = sa_ref[0] * jnp.dot(ha_ref[0], wpa_ref[...], preferred_element_type=F32) \
        + sb_ref[0] * jnp.dot(hb_ref[0], wpb_ref[...], preferred_element_type=F32)
    out = jnp.dot(merged.astype(BF16), wout_ref[...], preferred_element_type=F32)
    x1 = x + mod_ref[0, 2:3, :] * out
    x1_ref[0] = x1
    h2 = _rms_mod(x1, nw_ref[...], mod_ref[0, 3:4, :], mod_ref[0, 4:5, :])
    hp_ref[0] = h2
    logits =jnp.dot(h2, wr_ref[...], precision=lax.Precision.HIGHEST, preferred_element_type=F32)
    lt = logits.T[0:N_EXPERTS, :] + br_ref[...]
    mx = jnp.max(lt, axis=0, keepdims=True)
    ex = jnp.exp(lt - mx)
    aff_ref[0] = ex / jnp.sum(ex, axis=0, keepdims=True)


def _merge(x, pos, ha, hb2, sa, sb, mod, mod_row_fn, nw, wts, tm):
    B, T, D = x.shape
    has_pos = pos is not None
    wpa, wpb, wout, wr, br = wts
    su = wpb.shape[0]
    const = lambda shape: pl.BlockSpec(shape, lambda b, i: (0,) * len(shape),
                                       pipeline_mode=pl.Buffered(1))
    tok = lambda width: pl.BlockSpec((1, tm, width), lambda b, i: (b, i, 0))
    in_specs = [tok(D)]
    args = [x]
    if has_pos:
        in_specs.append(pl.BlockSpec((tm, D), lambda b, i: (i, 0)))
        args.append(pos)
    in_specs += [tok(ha.shape[-1]),
                 pl.BlockSpec((1, tm, su), lambda b, i: (b // SUBLANES, i, b % SUBLANES)),
                 tok(D), tok(D),
                 pl.BlockSpec((1, N_MOD, D), lambda b, i: (mod_row_fn(b), 0, 0)),
                 const((1, D)), const(wpa.shape), const(wpb.shape), const(wout.shape),
                 const(wr.shape), const(br.shape)]
    args += [ha, hb2, sa, sb, mod, nw, wpa, wpb, wout, wr, br]
    return pl.pallas_call(
        functools.partial(_merge_kernel, has_pos),
        grid=(B, T // tm),
        in_specs=in_specs,
        out_specs=[tok(D), tok(D),
                   pl.BlockSpec((1, N_EXPERTS, tm), lambda b, i: (b, 0, i))],
        out_shape=[jax.ShapeDtypeStruct((B, T, D), F32),
                   jax.ShapeDtypeStruct((B, T, D), F32),
                   jax.ShapeDtypeStruct((B, N_EXPERTS, T), F32)],
        compiler_params=_cparams(("arbitrary", "arbitrary")),
        name="merge",
    )(*args)


def _route_kernel(cap, aff_ref, idx_ref, val_ref, posm_s):
    aff = aff_ref[0]
    E, T = aff.shape
    thr_bits = jnp.zeros((E, 1), jnp.int32)
    for bit in range(30, -1, -1):
        cand = thr_bits | (1 << bit)
        cnt = jnp.sum((aff >= lax.bitcast_convert_type(cand, F32)).astype(F32), axis=1, keepdims=True)
        thr_bits = jnp.where(cnt >= cap, cand, thr_bits)
    thr = lax.bitcast_convert_type(thr_bits, F32)
    nxt = lax.bitcast_convert_type(thr_bits + 1, F32)
    gt = aff >= nxt
    eq = (aff >= thr) & jnp.logical_not(gt)
    need = cap - jnp.sum(gt.astype(F32), axis=1, keepdims=True)

    r = lax.broadcasted_iota(jnp.int32, (LANES, LANES), 0)
    c = lax.broadcasted_iota(jnp.int32, (LANES, LANES), 1)
    tri = (r < c).astype(BF16)

    def excl_cumsum(mk):
        outs = []
        carry = jnp.zeros((E, 1), F32)
        for kb in range(T // LANES):
            blk = mk[:, kb * LANES:(kb + 1) * LANES]
            outs.append(jnp.dot(blk.astype(BF16), tri, preferred_element_type=F32) + carry)
            carry = carry + jnp.sum(blk.astype(F32), axis=1, keepdims=True)
        return jnp.concatenate(outs, axis=1)

    tie_rank = excl_cumsum(eq)
    mask = gt | (eq & (tie_rank < need))
    pos = excl_cumsum(mask)
    posm_s[...] = jnp.where(mask, pos, -1.0)
    tio = lax.broadcasted_iota(jnp.int32, (1, T), 1).astype(F32)
    JB = min(cap, LANES)
    lane = lax.broadcasted_iota(jnp.int32, (JB, E), 1)
    for jb in range(cap // JB):
        jj = (lax.broadcasted_iota(jnp.int32, (JB, 1), 0) + jb * JB).astype(F32)

        def per_expert(e, carry):
            idx_t, val_t = carry
            hit = posm_s[pl.ds(e, 1), :] == jj
            ie = jnp.sum(jnp.where(hit, tio, 0.0), axis=1, keepdims=True)
            ve = jnp.sum(jnp.where(hit, aff_ref[0, pl.ds(e, 1), :], 0.0), axis=1, keepdims=True)
            return jnp.where(lane == e, ie, idx_t), jnp.where(lane == e, ve, val_t)

        idx_t, val_t = lax.fori_loop(0, E, per_expert,
                                     (jnp.zeros((JB, E), F32), jnp.zeros((JB, E), F32)))
        idx_ref[0, jb * JB:(jb + 1) * JB, :] = idx_t.astype(jnp.int32)
        val_ref[0, jb * JB:(jb + 1) * JB, :] = val_t


def _route(aff, cap):
    B, E, T = aff.shape
    return pl.pallas_call(
        functools.partial(_route_kernel, cap),
        grid=(B,),
        in_specs=[pl.BlockSpec((1, E, T), lambda b: (b, 0, 0))],
        out_specs=[pl.BlockSpec((1, cap, E), lambda b: (b, 0, 0)),
                   pl.BlockSpec((1, cap, E), lambda b: (b, 0, 0))],
        out_shape=[jax.ShapeDtypeStruct((B, cap, E), jnp.int32),
                   jax.ShapeDtypeStruct((B, cap, E), F32)],
        scratch_shapes=[pltpu.VMEM((E, T), F32)],
        compiler_params=_cparams(("arbitrary",)),
        name="route",
    )(aff)


def _moe_kernel(idx_ref, tok_ref, val_ref, w1_ref, w3_ref, w2_ref, out_ref, xe_s, ye_s, acc_s, sem):
    b, e, f = pl.program_id(0), pl.program_id(1), pl.program_id(2)
    ne, nf = pl.num_programs(1), pl.num_programs(2)
    cap = xe_s.shape[0]

    @pl.when((e == 0) & (f == 0))
    def _():
        acc_s[...] = jnp.zeros_like(acc_s)

    @pl.when(f == 0)
    def _():
        def gather(j, carry):
            r = idx_ref[0, 0, j]
            xe_s[pl.ds(j, 1), :] = tok_ref[0, pl.ds(r, 1), :]
            return carry
        lax.fori_loop(0, cap, gather, 0, unroll=8)

    xe = xe_s[...].astype(BF16)
    h1 = jnp.dot(xe, w1_ref[0], preferred_element_type=F32)
    h3 = jnp.dot(xe, w3_ref[0], preferred_element_type=F32)
    hid = (h1 * _sigmoid(h1) * h3).astype(BF16)
    ye = jnp.dot(hid, w2_ref[0], preferred_element_type=F32)

    @pl.when(f == 0)
    def _():
        ye_s[...] = ye

    @pl.when(f != 0)
    def _():
        ye_s[...] += ye

    @pl.when(f == nf - 1)
    def _():
        ye_s[...] = ye_s[...] * val_ref[0, 0]

        def scatter(j, carry):
            r = idx_ref[0, 0, j]
            acc_s[pl.ds(r, 1), :] += ye_s[pl.ds(j, 1), :]
            return carry
        lax.fori_loop(0, cap, scatter, 0, unroll=8)

    @pl.when((e == ne - 1) & (f == nf - 1))
    def _():
        cp = pltpu.make_async_copy(acc_s, out_ref.at[b], sem)
        cp.start()
        cp.wait()


def _moe(idx, tok, val, w1, w3, w2):
    B, NT, DP = tok.shape
    E, D, FF = w1.shape
    cap = idx.shape[-1]
    nf = EXPERT_FF_SPLIT
    fh = FF // nf
    return pl.pallas_call(
        _moe_kernel,
        grid=(B, E, nf),
        in_specs=[pl.BlockSpec((1, 1, cap), lambda b, e, f: (b * E + e, 0, 0), memory_space=pltpu.SMEM),
                  pl.BlockSpec((1, NT, DP), lambda b, e, f: (b, 0, 0), pipeline_mode=pl.Buffered(1)),
                  pl.BlockSpec((1, 1, cap, 1), lambda b, e, f: (b, e, 0, 0)),
                  pl.BlockSpec((1, D, fh), lambda b, e, f: (e, 0, f)),
                  pl.BlockSpec((1, D, fh), lambda b, e, f: (e, 0, f)),
                  pl.BlockSpec((1, fh, D), lambda b, e, f: (e, f, 0))],
        out_specs=pl.BlockSpec(memory_space=pl.ANY),
        out_shape=jax.ShapeDtypeStruct((B, NT, D), F32),
        scratch_shapes=[pltpu.VMEM((cap, DP), F32), pltpu.VMEM((cap, D), F32),
                        pltpu.VMEM((NT, D), F32), pltpu.SemaphoreType.DMA(())],
        compiler_params=_cparams(("arbitrary", "arbitrary", "arbitrary")),
        name="moe",
    )(idx, tok, val, w1, w3, w2)


def _final_kernel(x1_ref, moe_ref, mod_ref, nw_ref, y_ref):
    x = x1_ref[0] + mod_ref[0, 5:6, :] * moe_ref[0]
    ms = jnp.mean(x * x, axis=-1, keepdims=True)
    y_ref[0] = x * lax.rsqrt(ms + EPS) * nw_ref[...]


def _final(x1, moe, mod, mod_row_fn, nw, tm):
    B, T, D = x1.shape
    tok = pl.BlockSpec((1, tm, D), lambda b, i: (b, i, 0))
    return pl.pallas_call(
        _final_kernel,
        grid=(B, T // tm),
        in_specs=[tok, tok, pl.BlockSpec((1, N_MOD, D), lambda b, i: (mod_row_fn(b), 0, 0)),
                  pl.BlockSpec((1, D), lambda b, i: (0, 0))],
        out_specs=tok,
        out_shape=jax.ShapeDtypeStruct((B, T, D), F32),
        compiler_params=_cparams(("arbitrary", "arbitrary")),
        name="final",
    )(x1, moe, mod, nw)


def _pos2d(T, D):
    rows = T // GRID_W
    r, cl = jnp.meshgrid(jnp.arange(rows, dtype=F32), jnp.arange(GRID_W, dtype=F32), indexing='ij')
    r, cl = r.reshape(-1), cl.reshape(-1)
    quarter = D // 4
    freqs = 1.0 / (10000.0 ** (jnp.arange(quarter, dtype=F32) / quarter))
    er, ec = r[:, None] * freqs, cl[:, None] * freqs
    return jnp.concatenate([jnp.sin(er), jnp.cos(er), jnp.sin(ec), jnp.cos(ec)], axis=-1)


def _stream(x, pos, mod, mod_row_fn, P, mstate, s5_x0, tm):
    B, T, D = x.shape
    L = min(MLSTM_CHUNK, T)
    q, k, v, o, g, gt, u2, sa, sb = _inproj(x, pos, mod, mod_row_fn, P['norm_mix'], P['w_in_parts'], tm, L)
    res = _mlstm(q, k, v, o, g, gt, P['head_norm'], mstate, L)
    ha, mout = res[0], res[1:]
    G8 = B // SUBLANES
    su = u2.shape[-1] // SUBLANES
    u2r = u2.reshape(G8, T * SUBLANES, su)
    Lc = min(S5_CHUNK, T)
    yf, xf = _s5_pass(False, u2r, None, s5_x0[0], *P['s5_dir'][0], None, Lc)
    hb, xb = _s5_pass(True, u2r, yf, s5_x0[1], *P['s5_dir'][1], P['glu'], Lc)
    hb2 = hb.reshape(G8, T, SUBLANES * su)
    x1, hp, aff = _merge(x, pos, ha, hb2, sa, sb, mod, mod_row_fn, P['norm_ffn'], P['merge_w'], tm)
    cap = CAPACITY_FACTOR * T // N_EXPERTS
    idx, val = _route(aff, cap)
    E = N_EXPERTS
    gb = max(1, min(B, MOE_TABLE_ROWS // T))
    nb = B // gb
    offs = (jnp.arange(B, dtype=jnp.int32) % gb) * T
    idx_f = (idx + offs[:, None, None]).reshape(nb, gb, cap, E).transpose(0, 3, 1, 2).reshape(nb * E, 1, gb * cap)
    val_f = val.reshape(nb, gb, cap, E).transpose(0, 3, 1, 2).reshape(nb, E, gb * cap, 1)
    moe = _moe(idx_f, hp.reshape(nb, gb * T, D), val_f, *P['experts']).reshape(B, T, D)
    y = _final(x1, moe, mod, mod_row_fn, P['norm_final'], tm)
    return y, mout, (xf, xb)


def kernel(x_prompt, x_sample, state_mlstm_C, state_mlstm_n, state_mlstm_m, state_s5_re, state_s5_im,
           c, c_ctx, w_ada, b_ada, norm_mix, norm_ffn, w_in, b_gate, head_norm,
           s5_a_re, s5_a_im, s5_log_dt, s5_b_re, s5_b_im, s5_c_re, s5_c_im, s5_d, w_glu, b_glu,
           w_pA, w_pB, w_out, w_router, b_router, w_e1, w_e3, w_e2, norm_final):
    Bp, Tp, D = x_prompt.shape
    Bs, Ts, _ = x_sample.shape
    depth = w_ada.shape[0]
    H, dh = MLSTM_HEADS, MLSTM_HEAD_DIM
    W = H * dh
    G, Pn = s5_a_re.shape[2], s5_a_re.shape[3]
    SW = G * S5_GROUP

    assert depth == 1, depth
    xp, xs = x_prompt, x_sample
    pos = _pos2d(Ts, D)
    outs = None
    for l in range(depth):
        nrow = Bs + 1
        rows = -(-nrow // SUBLANES) * SUBLANES
        cpad = jnp.zeros((rows, D), F32).at[0:Bs].set(c).at[Bs].set(c_ctx)
        mod = _adaln(cpad, w_ada[l], b_ada[l]).reshape(rows, N_MOD, D)

        wi = w_in[l]
        gcols = wi[:, 4 * W:4 * W + 4 * H].reshape(D, 4, H).transpose(0, 2, 1).reshape(D, 4 * H)
        wug = jnp.concatenate([wi[:, 4 * W + 4 * H:4 * W + 4 * H + SW], gcols,
                               jnp.zeros((D, LANES - 4 * H), F32)], axis=1).astype(BF16)
        bg = jnp.concatenate([b_gate[l].T.reshape(1, 4 * H), jnp.zeros((1, LANES - 4 * H), F32)], axis=1)
        w_in_parts = (wi[:, 0:3 * W].astype(BF16), wi[:, 3 * W:4 * W].astype(BF16), wug,
                      wi[:, 4 * W + 4 * H + SW:].astype(BF16), bg)
        s5_dir = [_s5_discretise(s5_a_re[l, d], s5_a_im[l, d], s5_log_dt[l, d], s5_b_re[l], s5_b_im[l],
                                 s5_c_re[l, d], s5_c_im[l, d]) for d in range(2)]
        s5_dir = [(jnp.broadcast_to(a, (SUBLANES, a.shape[1])), jnp.broadcast_to(s, (SUBLANES, s.shape[1])), bw, cw)
                  for (a, s, bw, cw) in s5_dir]
        wr = jnp.concatenate([w_router[l], jnp.zeros((D, LANES - N_EXPERTS), F32)], axis=1)
        P = {
            'norm_mix': norm_mix[l].reshape(1, D), 'norm_ffn': norm_ffn[l].reshape(1, D),
            'norm_final': norm_final.reshape(1, D),
            'w_in_parts': w_in_parts, 'head_norm': head_norm[l], 's5_dir': s5_dir,
            'glu': (s5_d[l].reshape(1, SW), w_glu[l].astype(BF16), b_glu[l].reshape(1, SW)),
            'merge_w': (w_pA[l].astype(BF16), w_pB[l].astype(BF16), w_out[l].astype(BF16), wr,
                        b_router[l].reshape(N_EXPERTS, 1)),
            'experts': (w_e1[l].astype(BF16), w_e3[l].astype(BF16), w_e2[l].astype(BF16)),
        }
        zero_x0 = jnp.zeros((Bp // SUBLANES, SUBLANES, 2 * G * Pn), F32)
        yp, mout, (xf, xb) = _stream(xp, None, mod, lambda b: Bs, P, None, (zero_x0, zero_x0), min(256, Tp))
        lat_x0 = [_s5_state_to_lanes(state_s5_re[:, l, d], state_s5_im[:, l, d]) for d in range(2)]
        ys, _, _ = _stream(xs, pos, mod, lambda b: b, P,
                           (state_mlstm_C[:, l], state_mlstm_n[:, l], state_mlstm_m[:, l]), lat_x0,
                           min(256, Ts))
        Cn, nn, mn = mout
        fre, fim = _s5_lanes_to_state(xf, G, Pn)
        bre, bim = _s5_lanes_to_state(xb, G, Pn)
        outs = (Cn, nn.reshape(Bp, 2, H, dh), mn.reshape(Bp, 2, H),
                jnp.stack([fre, bre], axis=1), jnp.stack([fim, bim], axis=1))
        xp, xs = yp, ys
    out_C, out_n, out_m, out_re, out_im = (o[:, None] for o in outs)
    return (xp, xs, out_C, out_n, out_m, out_re, out_im)
```

```python
import functools

import jax
import jax.numpy as jnp
from jax import lax
from jax.experimental import pallas as pl
from jax.experimental.pallas import tpu as pltpu

F32 = jnp.float32
BF16 = jnp.bfloat16
EPS = 1e-6

MLSTM_HEADS = 4
MLSTM_HEAD_DIM = 256
N_EXPERTS = 16
CAPACITY_FACTOR = 2
S5_GROUP = 16
S5_STATE = 64
GRID_W = 64
N_MOD = 6

LANES = 128
SUBLANES = 8
VMEM_LIMIT_BYTES = 60000 * 1024

MLSTM_CHUNK = 256
S5_CHUNK = 64
S5_LANE_GROUPS = 4
EXPERT_FF_SPLIT = 2
MOE_TABLE_ROWS = 4096


def _cparams(sem):
    return pltpu.CompilerParams(dimension_semantics=sem, vmem_limit_bytes=VMEM_LIMIT_BYTES)


def _sigmoid(x):
    return jax.nn.sigmoid(x)


def _log_sigmoid(x):
    return jnp.minimum(x, 0.0) - jnp.log1p(jnp.exp(-jnp.abs(x)))


def _adaln_kernel(c_ref, w_ref, b_ref, o_ref):
    c = c_ref[...]
    s = c * _sigmoid(c)
    o_ref[...] = jnp.dot(s, w_ref[...], precision=lax.Precision.HIGHEST,
                         preferred_element_type=F32) + b_ref[...]


def _adaln(cpad, w, b):
    rows, d = cpad.shape
    n = w.shape[1]
    tn = 1536
    return pl.pallas_call(
        _adaln_kernel,
        grid=(n // tn,),
        in_specs=[pl.BlockSpec((rows, d), lambda j: (0, 0)),
                  pl.BlockSpec((d, tn), lambda j: (0, j)),
                  pl.BlockSpec((1, tn), lambda j: (0, j))],
        out_specs=pl.BlockSpec((rows, tn), lambda j: (0, j)),
        out_shape=jax.ShapeDtypeStruct((rows, n), F32),
        compiler_params=_cparams(("arbitrary",)),
        name="adaln",
    )(cpad, w, b.reshape(1, n))


def _rms_mod(x, nw, shift, scale):
    ms = jnp.mean(x * x, axis=-1, keepdims=True)
    y = x * lax.rsqrt(ms + EPS) * nw
    return y * (1.0 + scale) + shift


def _split3(x):
    hi = x.astype(BF16)
    r = x - hi.astype(F32)
    mid = r.astype(BF16)
    lo = (r - mid.astype(F32)).astype(BF16)
    return hi, mid, lo


def _inproj_kernel(has_pos, *refs):
    if has_pos:
        x_ref, pos_ref = refs[0], refs[1]
        refs = refs[2:]
    else:
        x_ref, pos_ref = refs[0], None
        refs = refs[1:]
    (mod_ref, nw_ref, wqkv_ref, wo_ref, wug_ref, wab_ref, bg_ref,
     q_ref, k_ref, v_ref, o_ref, gc_ref, gr_ref, gtot_ref, u_ref, sa_ref, sb_ref) = refs
    x = x_ref[0]
    if has_pos:
        x = x + pos_ref[...]
    h = _rms_mod(x, nw_ref[...], mod_ref[0, 0:1, :], mod_ref[0, 1:2, :])
    hb = h.astype(BF16)
    w = MLSTM_HEADS * MLSTM_HEAD_DIM
    qkv = jnp.dot(hb, wqkv_ref[...], preferred_element_type=F32)
    q_ref[0] = (qkv[:, 0:w] * (MLSTM_HEAD_DIM ** -0.5)).astype(BF16)
    k_ref[0] = qkv[:, w:2 * w].astype(BF16)
    v_ref[0] = qkv[:, 2 * w:3 * w].astype(BF16)
    o_ref[0] = jnp.dot(hb, wo_ref[...], preferred_element_type=F32)
    ug = jnp.dot(hb, wug_ref[...], preferred_element_type=F32)
    su = u_ref.shape[-1]
    u_ref[0] = ug[:, 0:su]
    g = ug[:, su:su + LANES] + bg_ref[...]
    col = lax.broadcasted_iota(jnp.int32, g.shape, 1)
    g = jnp.where((col % 2) == 1, _log_sigmoid(g), g)
    tm = g.shape[0]
    tril = (lax.broadcasted_iota(jnp.int32, (tm, tm), 1)
            <= lax.broadcasted_iota(jnp.int32, (tm, tm), 0)).astype(BF16)
    cs = sum(jnp.dot(tril, part, preferred_element_type=F32) for part in _split3(g))
    tot = cs[tm - 1:tm, :]
    cum = jnp.where((col % 4) < 2, cs, tot - cs + g)
    cum_l = pltpu.roll(cum, LANES - 1, axis=1)
    tot_l = pltpu.roll(jnp.broadcast_to(tot, g.shape), LANES - 1, axis=1)
    rt = (g - cum_l).T
    ggt = (tot_l - cum_l + g).T
    for hh in range(MLSTM_HEADS):
        gc_ref[0, hh] = cum[:, 4 * hh:4 * hh + 4]
        gr_ref[0, hh, 0, 0:4, :] = rt[4 * hh:4 * hh + 4, :]
        gr_ref[0, hh, 0, 4:8, :] = ggt[4 * hh:4 * hh + 4, :]
        gtot_ref[0, hh, 0] = tot[:, 4 * hh:4 * hh + 4]
    ab = jnp.dot(hb, wab_ref[...], preferred_element_type=F32)
    d = sa_ref.shape[-1]
    sa_ref[0] = _sigmoid(ab[:, 0:d])
    sb_ref[0] = _sigmoid(ab[:, d:2 * d])


def _inproj(x, pos, mod, mod_row_fn, nw, wts, tm, L):
    B, T, D = x.shape
    has_pos = pos is not None
    wqkv, wo, wug, wab, bg = wts
    su = wug.shape[1] - LANES
    assert tm == L, (tm, L)
    const = lambda shape: pl.BlockSpec(shape, lambda b, i: (0,) * len(shape),
                                       pipeline_mode=pl.Buffered(1))
    in_specs = [pl.BlockSpec((1, tm, D), lambda b, i: (b, i, 0))]
    args = [x]
    if has_pos:
        in_specs.append(pl.BlockSpec((tm, D), lambda b, i: (i, 0)))
        args.append(pos)
    in_specs += [pl.BlockSpec((1, N_MOD, D), lambda b, i: (mod_row_fn(b), 0, 0)),
                 const((1, D)), const(wqkv.shape), const(wo.shape), const(wug.shape),
                 const(wab.shape), const((1, LANES))]
    args += [mod, nw, wqkv, wo, wug, wab, bg]
    W = MLSTM_HEADS * MLSTM_HEAD_DIM
    tok = lambda width, dt: (jax.ShapeDtypeStruct((B, T, width), dt),
                             pl.BlockSpec((1, tm, width), lambda b, i: (b, i, 0)))
    outs = [tok(W, BF16), tok(W, BF16), tok(W, BF16), tok(W, F32),
            (jax.ShapeDtypeStruct((B, MLSTM_HEADS, T, 4), F32),
             pl.BlockSpec((1, MLSTM_HEADS, tm, 4), lambda b, i: (b, 0, i, 0))),
            (jax.ShapeDtypeStruct((B, MLSTM_HEADS, T // L, 8, L), F32),
             pl.BlockSpec((1, MLSTM_HEADS, 1, 8, L), lambda b, i: (b, 0, i, 0, 0))),
            (jax.ShapeDtypeStruct((B, MLSTM_HEADS, T // L, 1, 4), F32),
             pl.BlockSpec((1, MLSTM_HEADS, 1, 1, 4), lambda b, i: (b, 0, i, 0, 0))),
            tok(su, F32), tok(D, F32), tok(D, F32)]
    return pl.pallas_call(
        functools.partial(_inproj_kernel, has_pos),
        grid=(B, T // tm),
        in_specs=in_specs,
        out_specs=[o[1] for o in outs],
        out_shape=[o[0] for o in outs],
        compiler_params=_cparams(("arbitrary", "arbitrary")),
        name="inproj",
    )(*args)


def _mlstm_chunk(q, k, v_aug, b_col, r_row, gg_row, total, C_ref, m_ref, d, valid):
    dh = q.shape[1]
    L = q.shape[0]
    Ca = C_ref[d]
    m = m_ref[d]
    rmask = jnp.where(valid, r_row, -jnp.inf)
    folded = rmask[:, 0:LANES]
    for j in range(1, L // LANES):
        folded = jnp.maximum(folded, rmask[:, j * LANES:(j + 1) * LANES])
    mx = jnp.maximum(m, jnp.max(folded, axis=1, keepdims=True))
    p = jnp.exp(rmask - mx)
    w_inter = jnp.exp(m - mx)
    s = lax.dot_general(q, k, (((1,), (1,)), ((), ())), preferred_element_type=F32) * p
    na = w_inter * jnp.dot(q, Ca.astype(BF16), preferred_element_type=F32) \
        + jnp.dot(s.astype(BF16), v_aug, preferred_element_type=F32)
    rden = 1.0 / jnp.maximum(jnp.abs(na[:, dh:dh + LANES]), jnp.exp(-(b_col + mx)))
    h = jnp.concatenate([na[:, j * LANES:(j + 1) * LANES] * rden for j in range(dh // LANES)], axis=1)
    m_new = jnp.maximum(total + m, jnp.max(gg_row, axis=1, keepdims=True))
    decay = jnp.exp(total + m - m_new)
    kwt = (k.T.astype(F32) * jnp.exp(gg_row - m_new)).astype(BF16)
    C_ref[d] = decay * Ca + jnp.dot(kwt, v_aug, preferred_element_type=F32)
    m_ref[d] = m_new
    return h


def _mlstm_kernel(zero_init, L, *refs):
    T, dh = refs[0].shape[1], refs[0].shape[2]
    if zero_init:
        (q_ref, k_ref, v_ref, o_ref, gc_ref, gr_ref, gtot_ref, hn_ref,
         out_ref, Co_ref, no_ref, mo_ref, C_s, m_s, hf_s, hb_s) = refs
        C_s[...] = jnp.zeros_like(C_s)
        m_s[...] = jnp.zeros_like(m_s)
    else:
        (q_ref, k_ref, v_ref, o_ref, gc_ref, gr_ref, gtot_ref, hn_ref, C0_ref, n0_ref, m0_ref,
         out_ref, C_s, m_s, hf_s, hb_s) = refs
        C_s[:, :, 0:dh] = C0_ref[0, :, 0]
        C_s[:, :, dh:dh + LANES] = jnp.broadcast_to(n0_ref[0, :, 0], (2, dh, LANES))
        m_s[...] = m0_ref[0, :, 0]
    nc = T // L
    row = lax.broadcasted_iota(jnp.int32, (L, L), 0)
    col = lax.broadcasted_iota(jnp.int32, (L, L), 1)
    masks = (col <= row, col >= row)
    ones = jnp.ones((L, LANES), BF16)

    def body(c, carry):
        for d in range(2):
            cc = c if d == 0 else nc - 1 - c
            r0 = pl.multiple_of(cc * L, L)
            q = q_ref[0, pl.ds(r0, L), :]
            k = k_ref[0, pl.ds(r0, L), :]
            v_aug = jnp.concatenate([v_ref[0, pl.ds(r0, L), :], ones], axis=1)
            gc = gc_ref[0, 0, pl.ds(r0, L), :]
            gr = gr_ref[0, 0, cc]
            tot = gtot_ref[0, 0, cc]
            h = _mlstm_chunk(q, k, v_aug, gc[:, 2 * d + 1:2 * d + 2], gr[2 * d:2 * d + 1, :],
                             gr[4 + 2 * d:5 + 2 * d, :], tot[:, 2 * d + 1:2 * d + 2],
                             C_s, m_s, d, masks[d])
            (hf_s if d == 0 else hb_s)[pl.ds(r0, L), :] = h
        return carry

    lax.fori_loop(0, nc, body, 0)

    def fin(c, carry):
        r0 = pl.multiple_of(c * L, L)
        hm = (hf_s[pl.ds(r0, L), :] + hb_s[pl.ds(r0, L), :]) * _sigmoid(o_ref[0, pl.ds(r0, L), :])
        hm = hm * lax.rsqrt(jnp.mean(hm * hm, axis=-1, keepdims=True) + EPS) * hn_ref[0]
        out_ref[0, pl.ds(r0, L), :] = hm.astype(out_ref.dtype)
        return carry

    lax.fori_loop(0, nc, fin, 0)
    if zero_init:
        Co_ref[0, :, 0] = C_s[:, :, 0:dh]
        no_ref[0, :, 0] = C_s[:, :, dh:dh + 1]
        mo_ref[0, :, 0] = m_s[...]


def _mlstm(q, k, v, o, gc, gr, gtot, head_norm, state, L):
    B, T, W = q.shape
    H, dh = MLSTM_HEADS, MLSTM_HEAD_DIM
    nc = T // L
    zero_init = state is None
    bh = lambda width: pl.BlockSpec((1, T, width), lambda b, h: (b, 0, h))
    in_specs = [bh(dh), bh(dh), bh(dh), bh(dh),
                pl.BlockSpec((1, 1, T, 4), lambda b, h: (b, h, 0, 0)),
                pl.BlockSpec((1, 1, nc, 8, L), lambda b, h: (b, h, 0, 0, 0)),
                pl.BlockSpec((1, 1, nc, 1, 4), lambda b, h: (b, h, 0, 0, 0)),
                pl.BlockSpec((1, 1, dh), lambda b, h: (h, 0, 0))]
    args = [q, k, v, o, gc, gr, gtot, head_norm.reshape(H, 1, dh)]
    st_specs = [pl.BlockSpec((1, 2, 1, dh, dh), lambda b, h: (b, 0, h, 0, 0)),
                pl.BlockSpec((1, 2, 1, dh, 1), lambda b, h: (b, 0, h, 0, 0)),
                pl.BlockSpec((1, 2, 1, 1, 1), lambda b, h: (b, 0, h, 0, 0))]
    st_shapes = [jax.ShapeDtypeStruct((B, 2, H, dh, dh), F32),
                 jax.ShapeDtypeStruct((B, 2, H, dh, 1), F32),
                 jax.ShapeDtypeStruct((B, 2, H, 1, 1), F32)]
    out_specs = [bh(dh)]
    out_shape = [jax.ShapeDtypeStruct((B, T, W), BF16)]
    if zero_init:
        out_specs += st_specs
        out_shape += st_shapes
    else:
        C0, n0, m0 = state
        in_specs += st_specs
        args += [C0, n0.reshape(B, 2, H, dh, 1), m0.reshape(B, 2, H, 1, 1)]
    return pl.pallas_call(
        functools.partial(_mlstm_kernel, zero_init, L),
        grid=(B, H),
        in_specs=in_specs,
        out_specs=out_specs,
        out_shape=out_shape,
        scratch_shapes=[pltpu.VMEM((2, dh, dh + LANES), F32), pltpu.VMEM((2, 1, 1), F32),
                        pltpu.VMEM((T, dh), F32), pltpu.VMEM((T, dh), F32)],
        compiler_params=_cparams(("arbitrary", "arbitrary")),
        name="mlstm",
    )(*args)


def _s5_kernel(backward, Lc, *refs):
    if backward:
        (u_ref, yf_ref, x0_ref, ar_ref, as_ref, bw_ref, cw_ref, d_ref, wg_ref, bgl_ref,
         out_ref, xl_ref, X_s, st_s, U_s, H_s) = refs
    else:
        (u_ref, x0_ref, ar_ref, as_ref, bw_ref, cw_ref, out_ref, xl_ref, X_s, st_s, U_s) = refs
    ci = pl.program_id(1)
    nj = S5_LANE_GROUPS
    half = X_s.shape[1] // (2 * nj)
    nreq = u_ref.shape[0]

    @pl.when(ci == 0)
    def _():
        st_s[...] = x0_ref[0]

    for r in range(nreq):
        for j in range(nj):
            U_s[j, pl.ds(r, Lc, stride=nreq), :] = u_ref[r, :, j * LANES:(j + 1) * LANES]
    for j in range(nj):
        X_s[:, 2 * half * j:2 * half * (j + 1)] = jnp.dot(
            U_s[j].astype(BF16), bw_ref[j], preferred_element_type=F32)

    def swap(x):
        parts = []
        for j in range(nj):
            parts.append(x[:, 2 * half * j + half:2 * half * (j + 1)])
            parts.append(x[:, 2 * half * j:2 * half * j + half])
        return jnp.concatenate(parts, axis=1)

    def step(i, x):
        t = (Lc - 1 - i) if backward else i
        r0 = pl.multiple_of(t * SUBLANES, SUBLANES)
        xn = ar_ref[...] * x + as_ref[...] * swap(x) + X_s[pl.ds(r0, SUBLANES), :]
        X_s[pl.ds(r0, SUBLANES), :] = xn
        return xn

    xl = lax.fori_loop(0, Lc, step, st_s[...], unroll=2)
    st_s[...] = xl
    xl_ref[0] = xl
    ys = []
    for j in range(nj):
        ys.append(jnp.dot(X_s[:, 2 * half * j:2 * half * (j + 1)].astype(BF16), cw_ref[j],
                          preferred_element_type=F32))
    y = jnp.concatenate(ys, axis=1)
    if backward:
        u = jnp.concatenate([U_s[j] for j in range(nj)], axis=1)
        z = yf_ref[0] + y + d_ref[...] * u
        gate = _sigmoid(jnp.dot(z.astype(BF16), wg_ref[...], preferred_element_type=F32) + bgl_ref[...])
        hb = jax.nn.gelu(z) * gate
        for j in range(nj):
            H_s[j] = hb[:, j * LANES:(j + 1) * LANES]
        for r in range(nreq):
            for j in range(nj):
                out_ref[r, :, j * LANES:(j + 1) * LANES] = \
                    H_s[j, pl.ds(r, Lc, stride=nreq), :].astype(out_ref.dtype)
    else:
        out_ref[0] = y


def _s5_pass(backward, u, yf, x0, ar, asg, bw, cw, glu, Lc):
    B, T, su = u.shape
    G8 = B // SUBLANES
    rows = Lc * SUBLANES
    nchunk = T // Lc
    SL = x0.shape[-1]
    nj = S5_LANE_GROUPS
    assert su == nj * LANES, su
    cidx = (lambda c: nchunk - 1 - c) if backward else (lambda c: c)
    reqspec = pl.BlockSpec((SUBLANES, Lc, su), lambda g, c: (g, cidx(c), 0))
    rowspec = pl.BlockSpec((1, rows, su), lambda g, c: (g, cidx(c), 0))
    const = lambda shape: pl.BlockSpec(shape, lambda g, c: (0,) * len(shape))
    in_specs = [reqspec]
    args = [u]
    if backward:
        in_specs.append(rowspec)
        args.append(yf)
    in_specs += [pl.BlockSpec((1, SUBLANES, SL), lambda g, c: (g, 0, 0)),
                 const((SUBLANES, SL)), const((SUBLANES, SL)), const(bw.shape), const(cw.shape)]
    args += [x0, ar, asg, bw, cw]
    scratch = [pltpu.VMEM((rows, SL), F32), pltpu.VMEM((SUBLANES, SL), F32),
               pltpu.VMEM((nj, rows, LANES), F32)]
    if backward:
        d, wg, bgl = glu
        in_specs += [const((1, su)), const(wg.shape), const((1, su))]
        args += [d, wg, bgl]
        scratch.append(pltpu.VMEM((nj, rows, LANES), F32))
        out0 = (jax.ShapeDtypeStruct((B, T, su), BF16), reqspec)
    else:
        out0 = (jax.ShapeDtypeStruct((G8, T * SUBLANES, su), F32), rowspec)
    return pl.pallas_call(
        functools.partial(_s5_kernel, backward, Lc),
        grid=(G8, nchunk),
        in_specs=in_specs,
        out_specs=[out0[1], pl.BlockSpec((1, SUBLANES, SL), lambda g, c: (g, 0, 0))],
        out_shape=[out0[0], jax.ShapeDtypeStruct((G8, SUBLANES, SL), F32)],
        scratch_shapes=scratch,
        compiler_params=_cparams(("arbitrary", "arbitrary")),
        name="s5_bwd" if backward else "s5_fwd",
    )(*args)


def _s5_discretise(a_re, a_im, log_dt, b_re, b_im, c_re, c_im):
    G, P = a_re.shape
    nj = S5_LANE_GROUPS
    gl = G // nj
    dt = jnp.exp(log_dt)[:, None]
    e = jnp.exp(a_re * dt)
    ar = e * jnp.cos(a_im * dt)
    ai = e * jnp.sin(a_im * dt)
    den = a_re * a_re + a_im * a_im
    fr = ((ar - 1.0) * a_re + ai * a_im) / den
    fi = (ai * a_re - (ar - 1.0) * a_im) / den
    bbr = fr[..., None] * b_re - fi[..., None] * b_im
    bbi = fr[..., None] * b_im + fi[..., None] * b_re

    def lanes(re, im):
        return jnp.stack([re.reshape(nj, gl * P), im.reshape(nj, gl * P)], axis=1).reshape(1, -1)

    a_row = lanes(ar, ar)
    as_row = lanes(-ai, ai)
    eye = jnp.eye(gl, dtype=F32)
    C = b_re.shape[-1]

    def bmat(bb):
        x = bb.reshape(nj, gl, P, C)
        return jnp.einsum('jgpc,gh->jgchp', x, eye).reshape(nj, gl * C, gl * P)

    bw = jnp.concatenate([bmat(bbr), bmat(bbi)], axis=2).astype(BF16)

    def cmat(cc):
        x = cc.reshape(nj, gl, C, P)
        return jnp.einsum('jgcp,gh->jgphc', x, eye).reshape(nj, gl * P, gl * C)

    cw = jnp.concatenate([cmat(c_re), cmat(-c_im)], axis=1).astype(BF16)
    return a_row, as_row, bw, cw


def _s5_state_to_lanes(s_re, s_im):
    B, G, P = s_re.shape
    nj = S5_LANE_GROUPS
    x = jnp.stack([s_re.reshape(B, nj, (G // nj) * P), s_im.reshape(B, nj, (G // nj) * P)], axis=2)
    return x.reshape(B // SUBLANES, SUBLANES, 2 * G * P)


def _s5_lanes_to_state(x, G, P):
    G8 = x.shape[0]
    nj = S5_LANE_GROUPS
    x = x.reshape(G8 * SUBLANES, nj, 2, G // nj, P)
    return x[:, :, 0].reshape(-1, G, P), x[:, :, 1].reshape(-1, G, P)


def _merge_kernel(has_pos, *refs):
    if has_pos:
        x_ref, pos_ref = refs[0], refs[1]
        refs = refs[2:]
    else:
        x_ref, pos_ref = refs[0], None
        refs = refs[1:]
    (ha_ref, hb_ref, sa_ref, sb_ref, mod_ref, nw_ref, wpa_ref, wpb_ref, wout_ref, wrh_ref, wrm_ref,
     br_ref, x1_ref, hp_ref, aff_ref) = refs
    x = x_ref[0]
    if has_pos:
        x = x + pos_ref[...]
    merged = sa_ref[0] * jnp.dot(ha_ref[0], wpa_ref[...], preferred_element_type=F32) \
        + sb_ref[0] * jnp.dot(hb_ref[0], wpb_ref[...], preferred_element_type=F32)
    out = jnp.dot(merged.astype(BF16), wout_ref[...], preferred_element_type=F32)
    x1 = x + mod_ref[0, 2:3, :] * out
    x1_ref[0] = x1
    h2 = _rms_mod(x1, nw_ref[...], mod_ref[0, 3:4, :], mod_ref[0, 4:5, :])
    tm = h2.shape[0]
    for sl in range(h2.shape[1] // LANES):
        hp_ref[0, pl.ds(sl, tm, stride=SUBLANES), :] = h2[:, sl * LANES:(sl + 1) * LANES]
    h2h, h2m, _ = _split3(h2)
    logits = jnp.dot(h2h, wrh_ref[...], preferred_element_type=F32) \
        + jnp.dot(h2m, wrh_ref[...], preferred_element_type=F32) \
        + jnp.dot(h2h, wrm_ref[...], preferred_element_type=F32)
    lt = logits.T[0:N_EXPERTS, :] + br_ref[...]
    mx = jnp.max(lt, axis=0, keepdims=True)
    ex = jnp.exp(lt - mx)
    aff_ref[0] = ex / jnp.sum(ex, axis=0, keepdims=True)


def _merge(x, pos, ha, hb, sa, sb, mod, mod_row_fn, nw, wts, tm):
    B, T, D = x.shape
    has_pos = pos is not None
    wpa, wpb, wout, wrh, wrm, br = wts
    assert D == SUBLANES * LANES, D
    const = lambda shape: pl.BlockSpec(shape, lambda b, i: (0,) * len(shape),
                                       pipeline_mode=pl.Buffered(1))
    tok = lambda width: pl.BlockSpec((1, tm, width), lambda b, i: (b, i, 0))
    in_specs = [tok(D)]
    args = [x]
    if has_pos:
        in_specs.append(pl.BlockSpec((tm, D), lambda b, i: (i, 0)))
        args.append(pos)
    in_specs += [tok(ha.shape[-1]), tok(hb.shape[-1]), tok(D), tok(D),
                 pl.BlockSpec((1, N_MOD, D), lambda b, i: (mod_row_fn(b), 0, 0)),
                 const((1, D)), const(wpa.shape), const(wpb.shape), const(wout.shape),
                 const(wrh.shape), const(wrm.shape), const(br.shape)]
    args += [ha, hb, sa, sb, mod, nw, wpa, wpb, wout, wrh, wrm, br]
    return pl.pallas_call(
        functools.partial(_merge_kernel, has_pos),
        grid=(B, T // tm),
        in_specs=in_specs,
        out_specs=[tok(D), pl.BlockSpec((1, tm * SUBLANES, LANES), lambda b, i: (b, i, 0)),
                   pl.BlockSpec((1, N_EXPERTS, tm), lambda b, i: (b, 0, i))],
        out_shape=[jax.ShapeDtypeStruct((B, T, D), F32),
                   jax.ShapeDtypeStruct((B, T * SUBLANES, LANES), F32),
                   jax.ShapeDtypeStruct((B, N_EXPERTS, T), F32)],
        compiler_params=_cparams(("arbitrary", "arbitrary")),
        name="merge",
    )(*args)


def _route_kernel(cap, aff_ref, idx_ref, val_ref, posm_s):
    aff = aff_ref[0]
    E, T = aff.shape
    thr_bits = jnp.zeros((E, 1), jnp.int32)
    for bit in range(30, -1, -1):
        cand = thr_bits | (1 << bit)
        cnt = jnp.sum((aff >= lax.bitcast_convert_type(cand, F32)).astype(F32), axis=1, keepdims=True)
        thr_bits = jnp.where(cnt >= cap, cand, thr_bits)
    thr = lax.bitcast_convert_type(thr_bits, F32)
    nxt = lax.bitcast_convert_type(thr_bits + 1, F32)
    gt = aff >= nxt
    eq = (aff >= thr) & jnp.logical_not(gt)
    need = cap - jnp.sum(gt.astype(F32), axis=1, keepdims=True)

    r = lax.broadcasted_iota(jnp.int32, (LANES, LANES), 0)
    c = lax.broadcasted_iota(jnp.int32, (LANES, LANES), 1)
    tri = (r < c).astype(BF16)

    def excl_cumsum(mk):
        outs = []
        carry = jnp.zeros((E, 1), F32)
        for kb in range(T // LANES):
            blk = mk[:, kb * LANES:(kb + 1) * LANES]
            outs.append(jnp.dot(blk.astype(BF16), tri, preferred_element_type=F32) + carry)
            carry = carry + jnp.sum(blk.astype(F32), axis=1, keepdims=True)
        return jnp.concatenate(outs, axis=1)

    tie_rank = excl_cumsum(eq)
    mask = gt | (eq & (tie_rank < need))
    pos = excl_cumsum(mask)
    posm_s[...] = jnp.where(mask, pos, -1.0)
    tio = lax.broadcasted_iota(jnp.int32, (1, T), 1).astype(F32)
    JB = min(cap, LANES)
    lane = lax.broadcasted_iota(jnp.int32, (JB, E), 1)
    for jb in range(cap // JB):
        jj = (lax.broadcasted_iota(jnp.int32, (JB, 1), 0) + jb * JB).astype(F32)

        def per_expert(e, carry):
            idx_t, val_t = carry
            hit = posm_s[pl.ds(e, 1), :] == jj
            ie = jnp.sum(jnp.where(hit, tio, 0.0), axis=1, keepdims=True)
            ve = jnp.sum(jnp.where(hit, aff_ref[0, pl.ds(e, 1), :], 0.0), axis=1, keepdims=True)
            return jnp.where(lane == e, ie, idx_t), jnp.where(lane == e, ve, val_t)

        idx_t, val_t = lax.fori_loop(0, E, per_expert,
                                     (jnp.zeros((JB, E), F32), jnp.zeros((JB, E), F32)))
        idx_ref[0, jb * JB:(jb + 1) * JB, :] = idx_t.astype(jnp.int32)
        val_ref[0, jb * JB:(jb + 1) * JB, :] = val_t


def _route(aff, cap):
    B, E, T = aff.shape
    return pl.pallas_call(
        functools.partial(_route_kernel, cap),
        grid=(B,),
        in_specs=[pl.BlockSpec((1, E, T), lambda b: (b, 0, 0))],
        out_specs=[pl.BlockSpec((1, cap, E), lambda b: (b, 0, 0)),
                   pl.BlockSpec((1, cap, E), lambda b: (b, 0, 0))],
        out_shape=[jax.ShapeDtypeStruct((B, cap, E), jnp.int32),
                   jax.ShapeDtypeStruct((B, cap, E), F32)],
        scratch_shapes=[pltpu.VMEM((E, T), F32)],
        compiler_params=_cparams(("arbitrary",)),
        name="route",
    )(aff)


def _moe_kernel(idx_ref, tok_ref, val_ref, w1_ref, w3_ref, w2_ref, out_ref, xe_s, ye_s, yt_s, acc_s, sem):
    b, e, f = pl.program_id(0), pl.program_id(1), pl.program_id(2)
    ne, nf = pl.num_programs(1), pl.num_programs(2)
    cap = ye_s.shape[0]
    S = SUBLANES
    GR = 8

    def tile(ref, r):
        return ref.at[pl.ds(pl.multiple_of(r * S, S), S), :]

    @pl.when((e == 0) & (f == 0))
    def _():
        acc_s[...] = jnp.zeros_like(acc_s)

    @pl.when(f == 0)
    def _():
        def gather(g, carry):
            rows = [idx_ref[0, 0, g * GR + i] for i in range(GR)]
            vals = [tile(tok_ref.at[0], rows[i])[...] for i in range(GR)]
            for i in range(GR):
                tile(xe_s, g * GR + i)[...] = vals[i]
            return carry
        lax.fori_loop(0, cap // GR, gather, 0)

    xe = jnp.concatenate([xe_s[pl.ds(sl, cap, stride=S), :] for sl in range(S)], axis=1).astype(BF16)
    h1 = jnp.dot(xe, w1_ref[0], preferred_element_type=F32)
    h3 = jnp.dot(xe, w3_ref[0], preferred_element_type=F32)
    hid = (h1 * _sigmoid(h1) * h3).astype(BF16)
    ye = jnp.dot(hid, w2_ref[0], preferred_element_type=F32)

    @pl.when(f == 0)
    def _():
        ye_s[...] = ye

    @pl.when(f != 0)
    def _():
        ye_s[...] += ye

    @pl.when(f == nf - 1)
    def _():
        yw = ye_s[...] * val_ref[0, 0]
        for sl in range(S):
            yt_s[pl.ds(sl, cap, stride=S), :] = yw[:, sl * LANES:(sl + 1) * LANES]

        def scatter(g, carry):
            rows = [idx_ref[0, 0, g * GR + i] for i in range(GR)]
            vals = [tile(acc_s, rows[i])[...] + tile(yt_s, g * GR + i)[...] for i in range(GR)]
            for i in range(GR):
                tile(acc_s, rows[i])[...] = vals[i]
            return carry
        lax.fori_loop(0, cap // GR, scatter, 0)

    @pl.when((e == ne - 1) & (f == nf - 1))
    def _():
        cp = pltpu.make_async_copy(acc_s, out_ref.at[b], sem)
        cp.start()
        cp.wait()


def _moe(idx, tok, val, w1, w3, w2):
    B, NR, _ = tok.shape
    E, D, FF = w1.shape
    cap = idx.shape[-1]
    nf = EXPERT_FF_SPLIT
    fh = FF // nf
    return pl.pallas_call(
        _moe_kernel,
        grid=(B, E, nf),
        in_specs=[pl.BlockSpec((1, 1, cap), lambda b, e, f: (b * E + e, 0, 0), memory_space=pltpu.SMEM),
                  pl.BlockSpec((1, NR, LANES), lambda b, e, f: (b, 0, 0), pipeline_mode=pl.Buffered(1)),
                  pl.BlockSpec((1, 1, cap, 1), lambda b, e, f: (b, e, 0, 0)),
                  pl.BlockSpec((1, D, fh), lambda b, e, f: (e, 0, f)),
                  pl.BlockSpec((1, D, fh), lambda b, e, f: (e, 0, f)),
                  pl.BlockSpec((1, fh, D), lambda b, e, f: (e, f, 0))],
        out_specs=pl.BlockSpec(memory_space=pl.ANY),
        out_shape=jax.ShapeDtypeStruct((B, NR, LANES), F32),
        scratch_shapes=[pltpu.VMEM((cap * SUBLANES, LANES), F32), pltpu.VMEM((cap, D), F32),
                        pltpu.VMEM((cap * SUBLANES, LANES), F32), pltpu.VMEM((NR, LANES), F32),
                        pltpu.SemaphoreType.DMA(())],
        compiler_params=_cparams(("arbitrary", "arbitrary", "arbitrary")),
        name="moe",
    )(idx, tok, val, w1, w3, w2)


def _final_kernel(x1_ref, moe_ref, mod_ref, nw_ref, y_ref):
    tm = x1_ref.shape[1]
    moe = jnp.concatenate([moe_ref[0, pl.ds(sl, tm, stride=SUBLANES), :] for sl in range(SUBLANES)], axis=1)
    x = x1_ref[0] + mod_ref[0, 5:6, :] * moe
    ms = jnp.mean(x * x, axis=-1, keepdims=True)
    y_ref[0] = x * lax.rsqrt(ms + EPS) * nw_ref[...]


def _final(x1, moe, mod, mod_row_fn, nw, tm):
    B, T, D = x1.shape
    tok = pl.BlockSpec((1, tm, D), lambda b, i: (b, i, 0))
    return pl.pallas_call(
        _final_kernel,
        grid=(B, T // tm),
        in_specs=[tok, pl.BlockSpec((1, tm * SUBLANES, LANES), lambda b, i: (b, i, 0)),
                  pl.BlockSpec((1, N_MOD, D), lambda b, i: (mod_row_fn(b), 0, 0)),
                  pl.BlockSpec((1, D), lambda b, i: (0, 0))],
        out_specs=tok,
        out_shape=jax.ShapeDtypeStruct((B, T, D), F32),
        compiler_params=_cparams(("arbitrary", "arbitrary")),
        name="final",
    )(x1, moe, mod, nw)


def _pos2d(T, D):
    rows = T // GRID_W
    r, cl = jnp.meshgrid(jnp.arange(rows, dtype=F32), jnp.arange(GRID_W, dtype=F32), indexing='ij')
    r, cl = r.reshape(-1), cl.reshape(-1)
    quarter = D // 4
    freqs = 1.0 / (10000.0 ** (jnp.arange(quarter, dtype=F32) / quarter))
    er, ec = r[:, None] * freqs, cl[:, None] * freqs
    return jnp.concatenate([jnp.sin(er), jnp.cos(er), jnp.sin(ec), jnp.cos(ec)], axis=-1)


def _stream(x, pos, mod, mod_row_fn, P, mstate, s5_x0, tm):
    B, T, D = x.shape
    L = min(MLSTM_CHUNK, T)
    q, k, v, o, gc, gr, gtot, u, sa, sb = _inproj(x, pos, mod, mod_row_fn, P['norm_mix'],
                                                   P['w_in_parts'], tm, L)
    res = _mlstm(q, k, v, o, gc, gr, gtot, P['head_norm'], mstate, L)
    ha, mout = res[0], res[1:]
    Lc = min(S5_CHUNK, T)
    yf, xf = _s5_pass(False, u, None, s5_x0[0], *P['s5_dir'][0], None, Lc)
    hb, xb = _s5_pass(True, u, yf, s5_x0[1], *P['s5_dir'][1], P['glu'], Lc)
    x1, hp, aff = _merge(x, pos, ha, hb, sa, sb, mod, mod_row_fn, P['norm_ffn'], P['merge_w'], tm)
    cap = CAPACITY_FACTOR * T // N_EXPERTS
    idx, val = _route(aff, cap)
    E = N_EXPERTS
    gb = max(1, min(B, MOE_TABLE_ROWS // T))
    nb = B // gb
    offs = (jnp.arange(B, dtype=jnp.int32) % gb) * T
    idx_f = (idx + offs[:, None, None]).reshape(nb, gb, cap, E).transpose(0, 3, 1, 2).reshape(nb * E, 1, gb * cap)
    val_f = val.reshape(nb, gb, cap, E).transpose(0, 3, 1, 2).reshape(nb, E, gb * cap, 1)
    moe = _moe(idx_f, hp.reshape(nb, gb * T * SUBLANES, LANES), val_f, *P['experts'])
    y = _final(x1, moe.reshape(B, T * SUBLANES, LANES), mod, mod_row_fn, P['norm_final'], tm)
    return y, mout, (xf, xb)


def kernel(x_prompt, x_sample, state_mlstm_C, state_mlstm_n, state_mlstm_m, state_s5_re, state_s5_im,
           c, c_ctx, w_ada, b_ada, norm_mix, norm_ffn, w_in, b_gate, head_norm,
           s5_a_re, s5_a_im, s5_log_dt, s5_b_re, s5_b_im, s5_c_re, s5_c_im, s5_d, w_glu, b_glu,
           w_pA, w_pB, w_out, w_router, b_router, w_e1, w_e3, w_e2, norm_final):
    Bp, Tp, D = x_prompt.shape
    Bs, Ts, _ = x_sample.shape
    depth = w_ada.shape[0]
    H, dh = MLSTM_HEADS, MLSTM_HEAD_DIM
    W = H * dh
    G, Pn = s5_a_re.shape[2], s5_a_re.shape[3]
    SW = G * S5_GROUP

    assert depth == 1, depth
    l = 0
    pos = _pos2d(Ts, D)
    nrow = Bs + 1
    rows = -(-nrow // SUBLANES) * SUBLANES
    cpad = jnp.zeros((rows, D), F32).at[0:Bs].set(c).at[Bs].set(c_ctx)
    mod = _adaln(cpad, w_ada[l], b_ada[l]).reshape(rows, N_MOD, D)

    wi = w_in[l]
    gcols = wi[:, 4 * W:4 * W + 4 * H].reshape(D, 4, H).transpose(0, 2, 1).reshape(D, 4 * H)
    wug = jnp.concatenate([wi[:, 4 * W + 4 * H:4 * W + 4 * H + SW], gcols,
                           jnp.zeros((D, LANES - 4 * H), F32)], axis=1).astype(BF16)
    bg = jnp.concatenate([b_gate[l].T.reshape(1, 4 * H), jnp.zeros((1, LANES - 4 * H), F32)], axis=1)
    w_in_parts = (wi[:, 0:3 * W].astype(BF16), wi[:, 3 * W:4 * W].astype(BF16), wug,
                  wi[:, 4 * W + 4 * H + SW:].astype(BF16), bg)
    s5_dir = [_s5_discretise(s5_a_re[l, d], s5_a_im[l, d], s5_log_dt[l, d], s5_b_re[l], s5_b_im[l],
                             s5_c_re[l, d], s5_c_im[l, d]) for d in range(2)]
    s5_dir = [(jnp.broadcast_to(a, (SUBLANES, a.shape[1])), jnp.broadcast_to(s, (SUBLANES, s.shape[1])), bw, cw)
              for (a, s, bw, cw) in s5_dir]
    wr = jnp.concatenate([w_router[l], jnp.zeros((D, LANES - N_EXPERTS), F32)], axis=1)
    wr_hi = wr.astype(BF16)
    wr_mid = (wr - wr_hi.astype(F32)).astype(BF16)
    P = {
        'norm_mix': norm_mix[l].reshape(1, D), 'norm_ffn': norm_ffn[l].reshape(1, D),
        'norm_final': norm_final.reshape(1, D),
        'w_in_parts': w_in_parts, 'head_norm': head_norm[l], 's5_dir': s5_dir,
        'glu': (s5_d[l].reshape(1, SW), w_glu[l].astype(BF16), b_glu[l].reshape(1, SW)),
        'merge_w': (w_pA[l].astype(BF16), w_pB[l].astype(BF16), w_out[l].astype(BF16), wr_hi, wr_mid,
                    b_router[l].reshape(N_EXPERTS, 1)),
        'experts': (w_e1[l].astype(BF16), w_e3[l].astype(BF16), w_e2[l].astype(BF16)),
    }
    zero_x0 = jnp.zeros((Bp // SUBLANES, SUBLANES, 2 * G * Pn), F32)
    yp, mout, (xf, xb) = _stream(x_prompt, None, mod, lambda b: Bs, P, None, (zero_x0, zero_x0),
                                 min(MLSTM_CHUNK, Tp))
    lat_x0 = [_s5_state_to_lanes(state_s5_re[:, l, d], state_s5_im[:, l, d]) for d in range(2)]
    ys, _, _ = _stream(x_sample, pos, mod, lambda b: b, P,
                       (state_mlstm_C[:, l], state_mlstm_n[:, l], state_mlstm_m[:, l]), lat_x0,
                       min(MLSTM_CHUNK, Ts))
    Cn, nn, mn = mout
    fre, fim = _s5_lanes_to_state(xf, G, Pn)
    bre, bim = _s5_lanes_to_state(xb, G, Pn)
    outs = (Cn, nn.reshape(Bp, 2, H, dh), mn.reshape(Bp, 2, H),
            jnp.stack([fre, bre], axis=1), jnp.stack([fim, bim], axis=1))
    out_C, out_n, out_m, out_re, out_im = (o[:, None] for o in outs)
    return (yp, ys, out_C, out_n, out_m, out_re, out_im)
```

```python
import functools

import jax
import jax.numpy as jnp
from jax import lax
from jax.experimental import pallas as pl
from jax.experimental.pallas import tpu as pltpu

F32 = jnp.float32
BF16 = jnp.bfloat16
EPS = 1e-6

MLSTM_HEADS = 4
MLSTM_HEAD_DIM = 256
N_EXPERTS = 16
CAPACITY_FACTOR = 2
S5_GROUP = 16
S5_STATE = 64
GRID_W = 64
N_MOD = 6

LANES = 128
SUBLANES = 8
VMEM_LIMIT_BYTES = 60000 * 1024

MLSTM_CHUNK = 256
S5_CHUNK = 64
S5_LANE_GROUPS = 4
S5_SEGMENTS = 2
EXPERT_FF_SPLIT = 2
MOE_TABLE_ROWS = 4096


def _cparams(sem):
    return pltpu.CompilerParams(dimension_semantics=sem, vmem_limit_bytes=VMEM_LIMIT_BYTES)


def _sigmoid(x):
    return jax.nn.sigmoid(x)


def _log_sigmoid(x):
    return jnp.minimum(x, 0.0) - jnp.log1p(jnp.exp(-jnp.abs(x)))


def _adaln_kernel(c_ref, w_ref, b_ref, o_ref):
    c = c_ref[...]
    s = c * _sigmoid(c)
    o_ref[...] = jnp.dot(s, w_ref[...], precision=lax.Precision.HIGHEST,
                         preferred_element_type=F32) + b_ref[...]


def _adaln(cpad, w, b):
    rows, d = cpad.shape
    n = w.shape[1]
    tn = 1536
    return pl.pallas_call(
        _adaln_kernel,
        grid=(n // tn,),
        in_specs=[pl.BlockSpec((rows, d), lambda j: (0, 0)),
                  pl.BlockSpec((d, tn), lambda j: (0, j)),
                  pl.BlockSpec((1, tn), lambda j: (0, j))],
        out_specs=pl.BlockSpec((rows, tn), lambda j: (0, j)),
        out_shape=jax.ShapeDtypeStruct((rows, n), F32),
        compiler_params=_cparams(("arbitrary",)),
        name="adaln",
    )(cpad, w, b.reshape(1, n))


def _rms_mod(x, nw, shift, scale):
    ms = jnp.mean(x * x, axis=-1, keepdims=True)
    y = x * lax.rsqrt(ms + EPS) * nw
    return y * (1.0 + scale) + shift


def _split3(x):
    hi = x.astype(BF16)
    r = x - hi.astype(F32)
    mid = r.astype(BF16)
    lo = (r - mid.astype(F32)).astype(BF16)
    return hi, mid, lo


def _inproj_kernel(has_pos, *refs):
    if has_pos:
        x_ref, pos_ref = refs[0], refs[1]
        refs = refs[2:]
    else:
        x_ref, pos_ref = refs[0], None
        refs = refs[1:]
    (mod_ref, nw_ref, wqkv_ref, wo_ref, wug_ref, wab_ref, bg_ref,
     q_ref, k_ref, v_ref, o_ref, gc_ref, gr_ref, gtot_ref, u_ref, sa_ref, sb_ref) = refs
    x = x_ref[0]
    if has_pos:
        x = x + pos_ref[...]
    h = _rms_mod(x, nw_ref[...], mod_ref[0, 0:1, :], mod_ref[0, 1:2, :])
    hb = h.astype(BF16)
    w = MLSTM_HEADS * MLSTM_HEAD_DIM
    qkv = jnp.dot(hb, wqkv_ref[...], preferred_element_type=F32)
    q_ref[0] = (qkv[:, 0:w] * (MLSTM_HEAD_DIM ** -0.5)).astype(BF16)
    k_ref[0] = qkv[:, w:2 * w].astype(BF16)
    vt = qkv[:, 2 * w:3 * w].T.astype(BF16)
    for hh in range(MLSTM_HEADS):
        v_ref[0, hh, 0] = vt[hh * MLSTM_HEAD_DIM:(hh + 1) * MLSTM_HEAD_DIM, :]
    o_ref[0] = jnp.dot(hb, wo_ref[...], preferred_element_type=F32)
    ug = jnp.dot(hb, wug_ref[...], preferred_element_type=F32)
    su = u_ref.shape[-1]
    u_ref[0] = ug[:, 0:su]
    g = ug[:, su:su + LANES] + bg_ref[...]
    col = lax.broadcasted_iota(jnp.int32, g.shape, 1)
    g = jnp.where((col % 2) == 1, _log_sigmoid(g), g)
    tm = g.shape[0]
    tril = (lax.broadcasted_iota(jnp.int32, (tm, tm), 1)
            <= lax.broadcasted_iota(jnp.int32, (tm, tm), 0)).astype(BF16)
    cs = sum(jnp.dot(tril, part, preferred_element_type=F32) for part in _split3(g))
    tot = cs[tm - 1:tm, :]
    cum = jnp.where((col % 4) < 2, cs, tot - cs + g)
    cum_l = pltpu.roll(cum, LANES - 1, axis=1)
    tot_l = pltpu.roll(jnp.broadcast_to(tot, g.shape), LANES - 1, axis=1)
    r = g - cum_l
    cumt = cum.T
    ggt = (tot_l - cum_l + g).T
    for hh in range(MLSTM_HEADS):
        gc_ref[0, hh] = r[:, 4 * hh:4 * hh + 4]
        gr_ref[0, hh, 0, 0:4, :] = cumt[4 * hh:4 * hh + 4, :]
        gr_ref[0, hh, 0, 4:8, :] = ggt[4 * hh:4 * hh + 4, :]
        gtot_ref[0, hh, 0] = tot[:, 4 * hh:4 * hh + 4]
    ab = jnp.dot(hb, wab_ref[...], preferred_element_type=F32)
    d = sa_ref.shape[-1]
    sa_ref[0] = _sigmoid(ab[:, 0:d])
    sb_ref[0] = _sigmoid(ab[:, d:2 * d])


def _inproj(x, pos, mod, mod_row_fn, nw, wts, tm, L):
    B, T, D = x.shape
    has_pos = pos is not None
    wqkv, wo, wug, wab, bg = wts
    su = wug.shape[1] - LANES
    assert tm == L, (tm, L)
    const = lambda shape: pl.BlockSpec(shape, lambda b, i: (0,) * len(shape),
                                       pipeline_mode=pl.Buffered(1))
    in_specs = [pl.BlockSpec((1, tm, D), lambda b, i: (b, i, 0))]
    args = [x]
    if has_pos:
        in_specs.append(pl.BlockSpec((tm, D), lambda b, i: (i, 0)))
        args.append(pos)
    in_specs += [pl.BlockSpec((1, N_MOD, D), lambda b, i: (mod_row_fn(b), 0, 0)),
                 const((1, D)), const(wqkv.shape), const(wo.shape), const(wug.shape),
                 const(wab.shape), const((1, LANES))]
    args += [mod, nw, wqkv, wo, wug, wab, bg]
    W = MLSTM_HEADS * MLSTM_HEAD_DIM
    tok = lambda width, dt: (jax.ShapeDtypeStruct((B, T, width), dt),
                             pl.BlockSpec((1, tm, width), lambda b, i: (b, i, 0)))
    outs = [tok(W, BF16), tok(W, BF16),
            (jax.ShapeDtypeStruct((B, MLSTM_HEADS, T // L, MLSTM_HEAD_DIM, L), BF16),
             pl.BlockSpec((1, MLSTM_HEADS, 1, MLSTM_HEAD_DIM, L), lambda b, i: (b, 0, i, 0, 0))),
            tok(W, F32),
            (jax.ShapeDtypeStruct((B, MLSTM_HEADS, T, 4), F32),
             pl.BlockSpec((1, MLSTM_HEADS, tm, 4), lambda b, i: (b, 0, i, 0))),
            (jax.ShapeDtypeStruct((B, MLSTM_HEADS, T // L, 8, L), F32),
             pl.BlockSpec((1, MLSTM_HEADS, 1, 8, L), lambda b, i: (b, 0, i, 0, 0))),
            (jax.ShapeDtypeStruct((B, MLSTM_HEADS, T // L, 1, 4), F32),
             pl.BlockSpec((1, MLSTM_HEADS, 1, 1, 4), lambda b, i: (b, 0, i, 0, 0))),
            tok(su, F32), tok(D, F32), tok(D, F32)]
    return pl.pallas_call(
        functools.partial(_inproj_kernel, has_pos),
        grid=(B, T // tm),
        in_specs=in_specs,
        out_specs=[o[1] for o in outs],
        out_shape=[o[0] for o in outs],
        compiler_params=_cparams(("arbitrary", "arbitrary")),
        name="inproj",
    )(*args)


MLSTM_AUG_ROWS = 16


def _mlstm_chunk(q, k, vt_aug, r_col, b_row, gg_row, total, C_ref, m_ref, d, valid_t):
    dh = q.shape[1]
    Ca = C_ref[d]
    m = m_ref[d]
    nt = (((1,), (1,)), ((), ()))
    rmask = jnp.where(valid_t, r_col, -jnp.inf)
    mx = jnp.maximum(m, jnp.max(rmask, axis=0, keepdims=True))
    p = jnp.exp(rmask - mx)
    w_inter = jnp.exp(m - mx)
    st = lax.dot_general(k, q, nt, preferred_element_type=F32) * p
    na = w_inter * lax.dot_general(Ca.astype(BF16), q, nt, preferred_element_type=F32) \
        + jnp.dot(vt_aug, st.astype(BF16), preferred_element_type=F32)
    rden = 1.0 / jnp.maximum(jnp.abs(na[dh:dh + 1, :]), jnp.exp(-(b_row + mx)))
    ht = na[0:dh, :] * rden
    m_new = jnp.maximum(total + m, jnp.max(gg_row, axis=1, keepdims=True))
    decay = jnp.exp(total + m - m_new)
    vw = (vt_aug.astype(F32) * jnp.exp(gg_row - m_new)).astype(BF16)
    C_ref[d] = decay * Ca + jnp.dot(vw, k, preferred_element_type=F32)
    m_ref[d] = m_new
    return ht


def _mlstm_kernel(zero_init, L, *refs):
    T, dh = refs[0].shape[1], refs[0].shape[2]
    if zero_init:
        (q_ref, k_ref, v_ref, o_ref, gc_ref, gr_ref, gtot_ref, hn_ref,
         out_ref, Co_ref, no_ref, mo_ref, C_s, m_s, hf_s, hb_s) = refs
        C_s[...] = jnp.zeros_like(C_s)
        m_s[...] = jnp.zeros_like(m_s)
    else:
        (q_ref, k_ref, v_ref, o_ref, gc_ref, gr_ref, gtot_ref, hn_ref, C0_ref, n0_ref, m0_ref,
         out_ref, C_s, m_s, hf_s, hb_s) = refs
        for d in range(2):
            C_s[d, 0:dh, :] = C0_ref[0, d, 0].T
            C_s[d, dh:dh + MLSTM_AUG_ROWS, :] = jnp.broadcast_to(n0_ref[0, d, 0], (MLSTM_AUG_ROWS, dh))
        m_s[...] = m0_ref[0, :, 0]
    nc = T // L
    row = lax.broadcasted_iota(jnp.int32, (L, L), 0)
    col = lax.broadcasted_iota(jnp.int32, (L, L), 1)
    masks_t = (row <= col, row >= col)
    ones = jnp.ones((MLSTM_AUG_ROWS, L), BF16)

    def body(c, carry):
        for d in range(2):
            cc = c if d == 0 else nc - 1 - c
            r0 = pl.multiple_of(cc * L, L)
            q = q_ref[0, pl.ds(r0, L), :]
            k = k_ref[0, pl.ds(r0, L), :]
            vt_aug = jnp.concatenate([v_ref[0, 0, cc], ones], axis=0)
            gc = gc_ref[0, 0, pl.ds(r0, L), :]
            gr = gr_ref[0, 0, cc]
            tot = gtot_ref[0, 0, cc]
            ht = _mlstm_chunk(q, k, vt_aug, gc[:, 2 * d:2 * d + 1], gr[2 * d + 1:2 * d + 2, :],
                              gr[4 + 2 * d:5 + 2 * d, :], tot[:, 2 * d + 1:2 * d + 2],
                              C_s, m_s, d, masks_t[d])
            (hf_s if d == 0 else hb_s)[cc] = ht
        return carry

    lax.fori_loop(0, nc, body, 0, unroll=2 if nc % 2 == 0 else 1)

    def fin(c, carry):
        r0 = pl.multiple_of(c * L, L)
        hm = (hf_s[c] + hb_s[c]).T * _sigmoid(o_ref[0, pl.ds(r0, L), :])
        hm = hm * lax.rsqrt(jnp.mean(hm * hm, axis=-1, keepdims=True) + EPS) * hn_ref[0]
        out_ref[0, pl.ds(r0, L), :] = hm.astype(out_ref.dtype)
        return carry

    lax.fori_loop(0, nc, fin, 0)
    if zero_init:
        for d in range(2):
            Co_ref[0, d, 0] = C_s[d, 0:dh, :].T
        no_ref[0, :, 0] = C_s[:, dh:dh + 1, :]
        mo_ref[0, :, 0] = m_s[...]


def _mlstm(q, k, v, o, gc, gr, gtot, head_norm, state, L):
    B, T, W = q.shape
    H, dh = MLSTM_HEADS, MLSTM_HEAD_DIM
    nc = T // L
    zero_init = state is None
    bh = lambda width: pl.BlockSpec((1, T, width), lambda b, h: (b, 0, h))
    in_specs = [bh(dh), bh(dh), pl.BlockSpec((1, 1, nc, dh, L), lambda b, h: (b, h, 0, 0, 0)), bh(dh),
                pl.BlockSpec((1, 1, T, 4), lambda b, h: (b, h, 0, 0)),
                pl.BlockSpec((1, 1, nc, 8, L), lambda b, h: (b, h, 0, 0, 0)),
                pl.BlockSpec((1, 1, nc, 1, 4), lambda b, h: (b, h, 0, 0, 0)),
                pl.BlockSpec((1, 1, dh), lambda b, h: (h, 0, 0))]
    args = [q, k, v, o, gc, gr, gtot, head_norm.reshape(H, 1, dh)]
    st_specs = [pl.BlockSpec((1, 2, 1, dh, dh), lambda b, h: (b, 0, h, 0, 0)),
                pl.BlockSpec((1, 2, 1, 1, dh), lambda b, h: (b, 0, h, 0, 0)),
                pl.BlockSpec((1, 2, 1, 1, 1), lambda b, h: (b, 0, h, 0, 0))]
    st_shapes = [jax.ShapeDtypeStruct((B, 2, H, dh, dh), F32),
                 jax.ShapeDtypeStruct((B, 2, H, 1, dh), F32),
                 jax.ShapeDtypeStruct((B, 2, H, 1, 1), F32)]
    out_specs = [bh(dh)]
    out_shape = [jax.ShapeDtypeStruct((B, T, W), BF16)]
    if zero_init:
        out_specs += st_specs
        out_shape += st_shapes
    else:
        C0, n0, m0 = state
        in_specs += st_specs
        args += [C0, n0.reshape(B, 2, H, 1, dh), m0.reshape(B, 2, H, 1, 1)]
    return pl.pallas_call(
        functools.partial(_mlstm_kernel, zero_init, L),
        grid=(B, H),
        in_specs=in_specs,
        out_specs=out_specs,
        out_shape=out_shape,
        scratch_shapes=[pltpu.VMEM((2, dh + MLSTM_AUG_ROWS, dh), F32), pltpu.VMEM((2, 1, 1), F32),
                        pltpu.VMEM((nc, dh, L), F32), pltpu.VMEM((nc, dh, L), F32)],
        compiler_params=_cparams(("arbitrary", "arbitrary")),
        name="mlstm",
    )(*args)


def _s5_kernel(backward, Lc, *refs):
    if backward:
        (u_ref, yf_ref, x0_ref, ar_ref, as_ref, bw_ref, cw_ref, d_ref, wg_ref, bgl_ref,
         out_ref, xl_ref, X_s, st_s, U_s, H_s) = refs
    else:
        (u_ref, x0_ref, ar_ref, as_ref, bw_ref, cw_ref, out_ref, xl_ref, X_s, st_s, U_s) = refs
    ci = pl.program_id(1)
    nj = S5_LANE_GROUPS
    half = X_s.shape[1] // (2 * nj)
    nreq = u_ref.shape[0]

    @pl.when(ci == 0)
    def _():
        st_s[...] = x0_ref[0]

    for r in range(nreq):
        for j in range(nj):
            U_s[j, pl.ds(r, Lc, stride=nreq), :] = u_ref[r, :, j * LANES:(j + 1) * LANES]
    nseg = S5_SEGMENTS
    seg_t = Lc // nseg
    seg_order = range(nseg - 1, -1, -1) if backward else range(nseg)

    def rows_of(sg):
        return slice(sg * seg_t * SUBLANES, (sg + 1) * seg_t * SUBLANES)

    for sg in seg_order:
        for j in range(nj):
            X_s[rows_of(sg), 2 * half * j:2 * half * (j + 1)] = jnp.dot(
                U_s[j, rows_of(sg), :].astype(BF16), bw_ref[j], preferred_element_type=F32)

    def swap(x):
        parts = []
        for j in range(nj):
            parts.append(x[:, 2 * half * j + half:2 * half * (j + 1)])
            parts.append(x[:, 2 * half * j:2 * half * j + half])
        return jnp.concatenate(parts, axis=1)

    x = st_s[...]
    for sg in seg_order:
        ts = range(sg * seg_t, (sg + 1) * seg_t)
        for t in (reversed(ts) if backward else ts):
            rs = slice(t * SUBLANES, (t + 1) * SUBLANES)
            x = ar_ref[...] * x + as_ref[...] * swap(x) + X_s[rs, :]
            X_s[rs, :] = x
        y = jnp.concatenate(
            [jnp.dot(X_s[rows_of(sg), 2 * half * j:2 * half * (j + 1)].astype(BF16), cw_ref[j],
                     preferred_element_type=F32) for j in range(nj)], axis=1)
        if backward:
            u = jnp.concatenate([U_s[j, rows_of(sg), :] for j in range(nj)], axis=1)
            z = yf_ref[0, rows_of(sg), :] + y + d_ref[...] * u
            gate = _sigmoid(jnp.dot(z.astype(BF16), wg_ref[...], preferred_element_type=F32) + bgl_ref[...])
            hb = jax.nn.gelu(z) * gate
            for j in range(nj):
                H_s[j, rows_of(sg), :] = hb[:, j * LANES:(j + 1) * LANES]
        else:
            out_ref[0, rows_of(sg), :] = y
    st_s[...] = x
    xl_ref[0] = x
    if backward:
        for r in range(nreq):
            for j in range(nj):
                out_ref[r, :, j * LANES:(j + 1) * LANES] = \
                    H_s[j, pl.ds(r, Lc, stride=nreq), :].astype(out_ref.dtype)


def _s5_pass(backward, u, yf, x0, ar, asg, bw, cw, glu, Lc):
    B, T, su = u.shape
    G8 = B // SUBLANES
    rows = Lc * SUBLANES
    nchunk = T // Lc
    SL = x0.shape[-1]
    nj = S5_LANE_GROUPS
    assert su == nj * LANES, su
    cidx = (lambda c: nchunk - 1 - c) if backward else (lambda c: c)
    reqspec = pl.BlockSpec((SUBLANES, Lc, su), lambda g, c: (g, cidx(c), 0))
    rowspec = pl.BlockSpec((1, rows, su), lambda g, c: (g, cidx(c), 0))
    const = lambda shape: pl.BlockSpec(shape, lambda g, c: (0,) * len(shape))
    in_specs = [reqspec]
    args = [u]
    if backward:
        in_specs.append(rowspec)
        args.append(yf)
    in_specs += [pl.BlockSpec((1, SUBLANES, SL), lambda g, c: (g, 0, 0)),
                 const((SUBLANES, SL)), const((SUBLANES, SL)), const(bw.shape), const(cw.shape)]
    args += [x0, ar, asg, bw, cw]
    scratch = [pltpu.VMEM((rows, SL), F32), pltpu.VMEM((SUBLANES, SL), F32),
               pltpu.VMEM((nj, rows, LANES), F32)]
    if backward:
        d, wg, bgl = glu
        in_specs += [const((1, su)), const(wg.shape), const((1, su))]
        args += [d, wg, bgl]
        scratch.append(pltpu.VMEM((nj, rows, LANES), F32))
        out0 = (jax.ShapeDtypeStruct((B, T, su), BF16), reqspec)
    else:
        out0 = (jax.ShapeDtypeStruct((G8, T * SUBLANES, su), F32), rowspec)
    return pl.pallas_call(
        functools.partial(_s5_kernel, backward, Lc),
        grid=(G8, nchunk),
        in_specs=in_specs,
        out_specs=[out0[1], pl.BlockSpec((1, SUBLANES, SL), lambda g, c: (g, 0, 0))],
        out_shape=[out0[0], jax.ShapeDtypeStruct((G8, SUBLANES, SL), F32)],
        scratch_shapes=scratch,
        compiler_params=_cparams(("arbitrary", "arbitrary")),
        name="s5_bwd" if backward else "s5_fwd",
    )(*args)


def _s5_discretise(a_re, a_im, log_dt, b_re, b_im, c_re, c_im):
    G, P = a_re.shape
    nj = S5_LANE_GROUPS
    gl = G // nj
    dt = jnp.exp(log_dt)[:, None]
    e = jnp.exp(a_re * dt)
    ar = e * jnp.cos(a_im * dt)
    ai = e * jnp.sin(a_im * dt)
    den = a_re * a_re + a_im * a_im
    fr = ((ar - 1.0) * a_re + ai * a_im) / den
    fi = (ai * a_re - (ar - 1.0) * a_im) / den
    bbr = fr[..., None] * b_re - fi[..., None] * b_im
    bbi = fr[..., None] * b_im + fi[..., None] * b_re

    def lanes(re, im):
        return jnp.stack([re.reshape(nj, gl * P), im.reshape(nj, gl * P)], axis=1).reshape(1, -1)

    a_row = lanes(ar, ar)
    as_row = lanes(-ai, ai)
    eye = jnp.eye(gl, dtype=F32)
    C = b_re.shape[-1]

    def bmat(bb):
        x = bb.reshape(nj, gl, P, C)
        return jnp.einsum('jgpc,gh->jgchp', x, eye).reshape(nj, gl * C, gl * P)

    bw = jnp.concatenate([bmat(bbr), bmat(bbi)], axis=2).astype(BF16)

    def cmat(cc):
        x = cc.reshape(nj, gl, C, P)
        return jnp.einsum('jgcp,gh->jgphc', x, eye).reshape(nj, gl * P, gl * C)

    cw = jnp.concatenate([cmat(c_re), cmat(-c_im)], axis=1).astype(BF16)
    return a_row, as_row, bw, cw


def _s5_state_to_lanes(s_re, s_im):
    B, G, P = s_re.shape
    nj = S5_LANE_GROUPS
    x = jnp.stack([s_re.reshape(B, nj, (G // nj) * P), s_im.reshape(B, nj, (G // nj) * P)], axis=2)
    return x.reshape(B // SUBLANES, SUBLANES, 2 * G * P)


def _s5_lanes_to_state(x, G, P):
    G8 = x.shape[0]
    nj = S5_LANE_GROUPS
    x = x.reshape(G8 * SUBLANES, nj, 2, G // nj, P)
    return x[:, :, 0].reshape(-1, G, P), x[:, :, 1].reshape(-1, G, P)


def _merge_kernel(has_pos, *refs):
    if has_pos:
        x_ref, pos_ref = refs[0], refs[1]
        refs = refs[2:]
    else:
        x_ref, pos_ref = refs[0], None
        refs = refs[1:]
    (ha_ref, hb_ref, sa_ref, sb_ref, mod_ref, nw_ref, wpa_ref, wpb_ref, wout_ref, wrh_ref, wrm_ref,
     br_ref, x1_ref, hp_ref, aff_ref) = refs
    x = x_ref[0]
    if has_pos:
        x = x + pos_ref[...]
    merged = sa_ref[0] * jnp.dot(ha_ref[0], wpa_ref[...], preferred_element_type=F32) \
        + sb_ref[0] * jnp.dot(hb_ref[0], wpb_ref[...], preferred_element_type=F32)
    out = jnp.dot(merged.astype(BF16), wout_ref[...], preferred_element_type=F32)
    x1 = x + mod_ref[0, 2:3, :] * out
    x1_ref[0] = x1
    h2 = _rms_mod(x1, nw_ref[...], mod_ref[0, 3:4, :], mod_ref[0, 4:5, :])
    tm = h2.shape[0]
    for sl in range(h2.shape[1] // LANES):
        hp_ref[0, pl.ds(sl, tm, stride=SUBLANES), :] = h2[:, sl * LANES:(sl + 1) * LANES]
    h2h, h2m, _ = _split3(h2)
    logits = jnp.dot(h2h, wrh_ref[...], preferred_element_type=F32) \
        + jnp.dot(h2m, wrh_ref[...], preferred_element_type=F32) \
        + jnp.dot(h2h, wrm_ref[...], preferred_element_type=F32)
    lt = logits.T[0:N_EXPERTS, :] + br_ref[...]
    mx = jnp.max(lt, axis=0, keepdims=True)
    ex = jnp.exp(lt - mx)
    aff_ref[0] = ex / jnp.sum(ex, axis=0, keepdims=True)


def _merge(x, pos, ha, hb, sa, sb, mod, mod_row_fn, nw, wts, tm):
    B, T, D = x.shape
    has_pos = pos is not None
    wpa, wpb, wout, wrh, wrm, br = wts
    assert D == SUBLANES * LANES, D
    const = lambda shape: pl.BlockSpec(shape, lambda b, i: (0,) * len(shape),
                                       pipeline_mode=pl.Buffered(1))
    tok = lambda width: pl.BlockSpec((1, tm, width), lambda b, i: (b, i, 0))
    in_specs = [tok(D)]
    args = [x]
    if has_pos:
        in_specs.append(pl.BlockSpec((tm, D), lambda b, i: (i, 0)))
        args.append(pos)
    in_specs += [tok(ha.shape[-1]), tok(hb.shape[-1]), tok(D), tok(D),
                 pl.BlockSpec((1, N_MOD, D), lambda b, i: (mod_row_fn(b), 0, 0)),
                 const((1, D)), const(wpa.shape), const(wpb.shape), const(wout.shape),
                 const(wrh.shape), const(wrm.shape), const(br.shape)]
    args += [ha, hb, sa, sb, mod, nw, wpa, wpb, wout, wrh, wrm, br]
    return pl.pallas_call(
        functools.partial(_merge_kernel, has_pos),
        grid=(B, T // tm),
        in_specs=in_specs,
        out_specs=[tok(D), pl.BlockSpec((1, tm * SUBLANES, LANES), lambda b, i: (b, i, 0)),
                   pl.BlockSpec((1, N_EXPERTS, tm), lambda b, i: (b, 0, i))],
        out_shape=[jax.ShapeDtypeStruct((B, T, D), F32),
                   jax.ShapeDtypeStruct((B, T * SUBLANES, LANES), F32),
                   jax.ShapeDtypeStruct((B, N_EXPERTS, T), F32)],
        compiler_params=_cparams(("arbitrary", "arbitrary")),
        name="merge",
    )(*args)


def _route_kernel(cap, aff_ref, idx_ref, val_ref, posm_s):
    aff = aff_ref[0]
    E, T = aff.shape
    thr_bits = jnp.zeros((E, 1), jnp.int32)
    for bit in range(30, -1, -1):
        cand = thr_bits | (1 << bit)
        cnt = jnp.sum((aff >= lax.bitcast_convert_type(cand, F32)).astype(F32), axis=1, keepdims=True)
        thr_bits = jnp.where(cnt >= cap, cand, thr_bits)
    thr = lax.bitcast_convert_type(thr_bits, F32)
    nxt = lax.bitcast_convert_type(thr_bits + 1, F32)
    gt = aff >= nxt
    eq = (aff >= thr) & jnp.logical_not(gt)
    need = cap - jnp.sum(gt.astype(F32), axis=1, keepdims=True)

    r = lax.broadcasted_iota(jnp.int32, (LANES, LANES), 0)
    c = lax.broadcasted_iota(jnp.int32, (LANES, LANES), 1)
    tri = (r < c).astype(BF16)

    def excl_cumsum(mk):
        outs = []
        carry = jnp.zeros((E, 1), F32)
        for kb in range(T // LANES):
            blk = mk[:, kb * LANES:(kb + 1) * LANES]
            outs.append(jnp.dot(blk.astype(BF16), tri, preferred_element_type=F32) + carry)
            carry = carry + jnp.sum(blk.astype(F32), axis=1, keepdims=True)
        return jnp.concatenate(outs, axis=1)

    tie_rank = excl_cumsum(eq)
    mask = gt | (eq & (tie_rank < need))
    pos = excl_cumsum(mask)
    posm_s[...] = jnp.where(mask, pos, -1.0)
    tio = lax.broadcasted_iota(jnp.int32, (1, T), 1).astype(F32)
    JB = min(cap, LANES)
    lane = lax.broadcasted_iota(jnp.int32, (JB, E), 1)
    for jb in range(cap // JB):
        jj = (lax.broadcasted_iota(jnp.int32, (JB, 1), 0) + jb * JB).astype(F32)

        def per_expert(e, carry):
            idx_t, val_t = carry
            hit = posm_s[pl.ds(e, 1), :] == jj
            ie = jnp.sum(jnp.where(hit, tio, 0.0), axis=1, keepdims=True)
            ve = jnp.sum(jnp.where(hit, aff_ref[0, pl.ds(e, 1), :], 0.0), axis=1, keepdims=True)
            return jnp.where(lane == e, ie, idx_t), jnp.where(lane == e, ve, val_t)

        idx_t, val_t = lax.fori_loop(0, E, per_expert,
                                     (jnp.zeros((JB, E), F32), jnp.zeros((JB, E), F32)))
        idx_ref[0, jb * JB:(jb + 1) * JB, :] = idx_t.astype(jnp.int32)
        val_ref[0, jb * JB:(jb + 1) * JB, :] = val_t


def _route(aff, cap):
    B, E, T = aff.shape
    return pl.pallas_call(
        functools.partial(_route_kernel, cap),
        grid=(B,),
        in_specs=[pl.BlockSpec((1, E, T), lambda b: (b, 0, 0))],
        out_specs=[pl.BlockSpec((1, cap, E), lambda b: (b, 0, 0)),
                   pl.BlockSpec((1, cap, E), lambda b: (b, 0, 0))],
        out_shape=[jax.ShapeDtypeStruct((B, cap, E), jnp.int32),
                   jax.ShapeDtypeStruct((B, cap, E), F32)],
        scratch_shapes=[pltpu.VMEM((E, T), F32)],
        compiler_params=_cparams(("arbitrary",)),
        name="route",
    )(aff)


def _moe_kernel(idx_ref, idxn_ref, tok_ref, val_ref, w1_ref, w3_ref, w2_ref, out_ref,
                xe_s, xb_s, ye_s, yt_s, acc_s, sem):
    b, e, f = pl.program_id(0), pl.program_id(1), pl.program_id(2)
    ne = pl.num_programs(1)
    cap = ye_s.shape[0]
    S = SUBLANES
    GR = 8

    def tile(ref, r):
        return ref.at[pl.ds(pl.multiple_of(r * S, S), S), :]

    def gather_group(iref, g):
        rows = [iref[0, 0, g * GR + i] for i in range(GR)]
        vals = [tile(tok_ref.at[0], rows[i])[...] for i in range(GR)]
        for i in range(GR):
            tile(xe_s, g * GR + i)[...] = vals[i]

    def scatter_group(iref, g):
        rows = [iref[0, 0, g * GR + i] for i in range(GR)]
        vals = [tile(acc_s, rows[i])[...] + tile(yt_s, g * GR + i)[...] for i in range(GR)]
        for i in range(GR):
            tile(acc_s, rows[i])[...] = vals[i]

    def ffn_half():
        xe = xb_s[...]
        h1 = jnp.dot(xe, w1_ref[0], preferred_element_type=F32)
        h3 = jnp.dot(xe, w3_ref[0], preferred_element_type=F32)
        hid = (h1 * _sigmoid(h1) * h3).astype(BF16)
        return jnp.dot(hid, w2_ref[0], preferred_element_type=F32)

    @pl.when((e == 0) & (f == 0))
    def _():
        acc_s[...] = jnp.zeros_like(acc_s)
        yt_s[...] = jnp.zeros_like(yt_s)
        lax.fori_loop(0, cap // GR, lambda g, c: (gather_group(idx_ref, g), c)[1], 0)

    @pl.when(f == 0)
    def _():
        xb_s[...] = jnp.concatenate([xe_s[pl.ds(sl, cap, stride=S), :] for sl in range(S)],
                                    axis=1).astype(BF16)
        ye_s[...] = ffn_half()
        for g in range(cap // GR):
            scatter_group(idxn_ref, g)

    @pl.when(f == 1)
    def _():
        yw = (ye_s[...] + ffn_half()) * val_ref[0, 0]
        for sl in range(S):
            yt_s[pl.ds(sl, cap, stride=S), :] = yw[:, sl * LANES:(sl + 1) * LANES]
        for g in range(cap // GR):
            gather_group(idxn_ref, g)

    @pl.when((e == ne - 1) & (f == 1))
    def _():
        lax.fori_loop(0, cap // GR, lambda g, c: (scatter_group(idx_ref, g), c)[1], 0)
        cp = pltpu.make_async_copy(acc_s, out_ref.at[b], sem)
        cp.start()
        cp.wait()


def _moe(idx, tok, val, w1, w3, w2):
    B, NR, _ = tok.shape
    E, D, FF = w1.shape
    cap = idx.shape[-1]
    nf = EXPERT_FF_SPLIT
    assert nf == 2, nf
    fh = FF // nf
    return pl.pallas_call(
        _moe_kernel,
        grid=(B, E, nf),
        in_specs=[pl.BlockSpec((1, 1, cap), lambda b, e, f: (b * E + e, 0, 0), memory_space=pltpu.SMEM),
                  pl.BlockSpec((1, 1, cap), lambda b, e, f: (b * E + jnp.clip(e - 1 + 2 * f, 0, E - 1), 0, 0),
                               memory_space=pltpu.SMEM),
                  pl.BlockSpec((1, NR, LANES), lambda b, e, f: (b, 0, 0), pipeline_mode=pl.Buffered(1)),
                  pl.BlockSpec((1, 1, cap, 1), lambda b, e, f: (b, e, 0, 0)),
                  pl.BlockSpec((1, D, fh), lambda b, e, f: (e, 0, f)),
                  pl.BlockSpec((1, D, fh), lambda b, e, f: (e, 0, f)),
                  pl.BlockSpec((1, fh, D), lambda b, e, f: (e, f, 0))],
        out_specs=pl.BlockSpec(memory_space=pl.ANY),
        out_shape=jax.ShapeDtypeStruct((B, NR, LANES), F32),
        scratch_shapes=[pltpu.VMEM((cap * SUBLANES, LANES), F32), pltpu.VMEM((cap, D), BF16),
                        pltpu.VMEM((cap, D), F32), pltpu.VMEM((cap * SUBLANES, LANES), F32),
                        pltpu.VMEM((NR, LANES), F32), pltpu.SemaphoreType.DMA(())],
        compiler_params=_cparams(("arbitrary", "arbitrary", "arbitrary")),
        name="moe",
    )(idx, idx, tok, val, w1, w3, w2)


def _final_kernel(x1_ref, moe_ref, mod_ref, nw_ref, y_ref):
    tm = x1_ref.shape[1]
    moe = jnp.concatenate([moe_ref[0, pl.ds(sl, tm, stride=SUBLANES), :] for sl in range(SUBLANES)], axis=1)
    x = x1_ref[0] + mod_ref[0, 5:6, :] * moe
    ms = jnp.mean(x * x, axis=-1, keepdims=True)
    y_ref[0] = x * lax.rsqrt(ms + EPS) * nw_ref[...]


def _final(x1, moe, mod, mod_row_fn, nw, tm):
    B, T, D = x1.shape
    tok = pl.BlockSpec((1, tm, D), lambda b, i: (b, i, 0))
    return pl.pallas_call(
        _final_kernel,
        grid=(B, T // tm),
        in_specs=[tok, pl.BlockSpec((1, tm * SUBLANES, LANES), lambda b, i: (b, i, 0)),
                  pl.BlockSpec((1, N_MOD, D), lambda b, i: (mod_row_fn(b), 0, 0)),
                  pl.BlockSpec((1, D), lambda b, i: (0, 0))],
        out_specs=tok,
        out_shape=jax.ShapeDtypeStruct((B, T, D), F32),
        compiler_params=_cparams(("arbitrary", "arbitrary")),
        name="final",
    )(x1, moe, mod, nw)


def _pos2d(T, D):
    rows = T // GRID_W
    r, cl = jnp.meshgrid(jnp.arange(rows, dtype=F32), jnp.arange(GRID_W, dtype=F32), indexing='ij')
    r, cl = r.reshape(-1), cl.reshape(-1)
    quarter = D // 4
    freqs = 1.0 / (10000.0 ** (jnp.arange(quarter, dtype=F32) / quarter))
    er, ec = r[:, None] * freqs, cl[:, None] * freqs
    return jnp.concatenate([jnp.sin(er), jnp.cos(er), jnp.sin(ec), jnp.cos(ec)], axis=-1)


def _stream(x, pos, mod, mod_row_fn, P, mstate, s5_x0, tm):
    B, T, D = x.shape
    L = min(MLSTM_CHUNK, T)
    q, k, v, o, gc, gr, gtot, u, sa, sb = _inproj(x, pos, mod, mod_row_fn, P['norm_mix'],
                                                   P['w_in_parts'], tm, L)
    res = _mlstm(q, k, v, o, gc, gr, gtot, P['head_norm'], mstate, L)
    ha, mout = res[0], res[1:]
    Lc = min(S5_CHUNK, T)
    yf, xf = _s5_pass(False, u, None, s5_x0[0], *P['s5_dir'][0], None, Lc)
    hb, xb = _s5_pass(True, u, yf, s5_x0[1], *P['s5_dir'][1], P['glu'], Lc)
    x1, hp, aff = _merge(x, pos, ha, hb, sa, sb, mod, mod_row_fn, P['norm_ffn'], P['merge_w'], tm)
    cap = CAPACITY_FACTOR * T // N_EXPERTS
    idx, val = _route(aff, cap)
    E = N_EXPERTS
    gb = max(1, min(B, MOE_TABLE_ROWS // T))
    nb = B // gb
    offs = (jnp.arange(B, dtype=jnp.int32) % gb) * T
    idx_f = (idx + offs[:, None, None]).reshape(nb, gb, cap, E).transpose(0, 3, 1, 2).reshape(nb * E, 1, gb * cap)
    val_f = val.reshape(nb, gb, cap, E).transpose(0, 3, 1, 2).reshape(nb, E, gb * cap, 1)
    moe = _moe(idx_f, hp.reshape(nb, gb * T * SUBLANES, LANES), val_f, *P['experts'])
    y = _final(x1, moe.reshape(B, T * SUBLANES, LANES), mod, mod_row_fn, P['norm_final'], tm)
    return y, mout, (xf, xb)


def kernel(x_prompt, x_sample, state_mlstm_C, state_mlstm_n, state_mlstm_m, state_s5_re, state_s5_im,
           c, c_ctx, w_ada, b_ada, norm_mix, norm_ffn, w_in, b_gate, head_norm,
           s5_a_re, s5_a_im, s5_log_dt, s5_b_re, s5_b_im, s5_c_re, s5_c_im, s5_d, w_glu, b_glu,
           w_pA, w_pB, w_out, w_router, b_router, w_e1, w_e3, w_e2, norm_final):
    Bp, Tp, D = x_prompt.shape
    Bs, Ts, _ = x_sample.shape
    depth = w_ada.shape[0]
    H, dh = MLSTM_HEADS, MLSTM_HEAD_DIM
    W = H * dh
    G, Pn = s5_a_re.shape[2], s5_a_re.shape[3]
    SW = G * S5_GROUP

    assert depth == 1, depth
    l = 0
    pos = _pos2d(Ts, D)
    nrow = Bs + 1
    rows = -(-nrow // SUBLANES) * SUBLANES
    cpad = jnp.zeros((rows, D), F32).at[0:Bs].set(c).at[Bs].set(c_ctx)
    mod = _adaln(cpad, w_ada[l], b_ada[l]).reshape(rows, N_MOD, D)

    wi = w_in[l]
    gcols = wi[:, 4 * W:4 * W + 4 * H].reshape(D, 4, H).transpose(0, 2, 1).reshape(D, 4 * H)
    wug = jnp.concatenate([wi[:, 4 * W + 4 * H:4 * W + 4 * H + SW], gcols,
                           jnp.zeros((D, LANES - 4 * H), F32)], axis=1).astype(BF16)
    bg = jnp.concatenate([b_gate[l].T.reshape(1, 4 * H), jnp.zeros((1, LANES - 4 * H), F32)], axis=1)
    w_in_parts = (wi[:, 0:3 * W].astype(BF16), wi[:, 3 * W:4 * W].astype(BF16), wug,
                  wi[:, 4 * W + 4 * H + SW:].astype(BF16), bg)
    s5_dir = [_s5_discretise(s5_a_re[l, d], s5_a_im[l, d], s5_log_dt[l, d], s5_b_re[l], s5_b_im[l],
                             s5_c_re[l, d], s5_c_im[l, d]) for d in range(2)]
    s5_dir = [(jnp.broadcast_to(a, (SUBLANES, a.shape[1])), jnp.broadcast_to(s, (SUBLANES, s.shape[1])), bw, cw)
              for (a, s, bw, cw) in s5_dir]
    wr = jnp.concatenate([w_router[l], jnp.zeros((D, LANES - N_EXPERTS), F32)], axis=1)
    wr_hi = wr.astype(BF16)
    wr_mid = (wr - wr_hi.astype(F32)).astype(BF16)
    P = {
        'norm_mix': norm_mix[l].reshape(1, D), 'norm_ffn': norm_ffn[l].reshape(1, D),
        'norm_final': norm_final.reshape(1, D),
        'w_in_parts': w_in_parts, 'head_norm': head_norm[l], 's5_dir': s5_dir,
        'glu': (s5_d[l].reshape(1, SW), w_glu[l].astype(BF16), b_glu[l].reshape(1, SW)),
        'merge_w': (w_pA[l].astype(BF16), w_pB[l].astype(BF16), w_out[l].astype(BF16), wr_hi, wr_mid,
                    b_router[l].reshape(N_EXPERTS, 1)),
        'experts': (w_e1[l].astype(BF16), w_e3[l].astype(BF16), w_e2[l].astype(BF16)),
    }
    zero_x0 = jnp.zeros((Bp // SUBLANES, SUBLANES, 2 * G * Pn), F32)
    yp, mout, (xf, xb) = _stream(x_prompt, None, mod, lambda b: Bs, P, None, (zero_x0, zero_x0),
                                 min(MLSTM_CHUNK, Tp))
    lat_x0 = [_s5_state_to_lanes(state_s5_re[:, l, d], state_s5_im[:, l, d]) for d in range(2)]
    ys, _, _ = _stream(x_sample, pos, mod, lambda b: b, P,
                       (state_mlstm_C[:, l], state_mlstm_n[:, l], state_mlstm_m[:, l]), lat_x0,
                       min(MLSTM_CHUNK, Ts))
    Cn, nn, mn = mout
    fre, fim = _s5_lanes_to_state(xf, G, Pn)
    bre, bim = _s5_lanes_to_state(xb, G, Pn)
    outs = (Cn, nn.reshape(Bp, 2, H, dh), mn.reshape(Bp, 2, H),
            jnp.stack([fre, bre], axis=1), jnp.stack([fim, bim], axis=1))
    out_C, out_n, out_m, out_re, out_im = (o[:, None] for o in outs)
    return (yp, ys, out_C, out_n, out_m, out_re, out_im)
```

```python
import functools

import jax
import jax.numpy as jnp
from jax import lax
from jax.experimental import pallas as pl
from jax.experimental.pallas import tpu as pltpu

F32 = jnp.float32
BF16 = jnp.bfloat16
EPS = 1e-6

MLSTM_HEADS = 4
MLSTM_HEAD_DIM = 256
N_EXPERTS = 16
CAPACITY_FACTOR = 2
S5_GROUP = 16
S5_STATE = 64
GRID_W = 64
N_MOD = 6

LANES = 128
SUBLANES = 8
VMEM_LIMIT_BYTES = 60000 * 1024

MLSTM_CHUNK = 256
S5_CHUNK = 64
S5_LANE_GROUPS = 4
S5_SEGMENTS = 2
EXPERT_FF_SPLIT = 2
MOE_TABLE_ROWS = 4096


def _cparams(sem):
    return pltpu.CompilerParams(dimension_semantics=sem, vmem_limit_bytes=VMEM_LIMIT_BYTES)


def _sigmoid(x):
    return jax.nn.sigmoid(x)


def _log_sigmoid(x):
    return jnp.minimum(x, 0.0) - jnp.log1p(jnp.exp(-jnp.abs(x)))


def _adaln_kernel(c_ref, w_ref, b_ref, o_ref):
    c = c_ref[...]
    s = c * _sigmoid(c)
    o_ref[...] = jnp.dot(s, w_ref[...], precision=lax.Precision.HIGHEST,
                         preferred_element_type=F32) + b_ref[...]


def _adaln(cpad, w, b):
    rows, d = cpad.shape
    n = w.shape[1]
    tn = 1536
    return pl.pallas_call(
        _adaln_kernel,
        grid=(n // tn,),
        in_specs=[pl.BlockSpec((rows, d), lambda j: (0, 0)),
                  pl.BlockSpec((d, tn), lambda j: (0, j)),
                  pl.BlockSpec((1, tn), lambda j: (0, j))],
        out_specs=pl.BlockSpec((rows, tn), lambda j: (0, j)),
        out_shape=jax.ShapeDtypeStruct((rows, n), F32),
        compiler_params=_cparams(("arbitrary",)),
        name="adaln",
    )(cpad, w, b.reshape(1, n))


def _rms_mod(x, nw, shift, scale):
    ms = jnp.mean(x * x, axis=-1, keepdims=True)
    y = x * lax.rsqrt(ms + EPS) * nw
    return y * (1.0 + scale) + shift


def _split3(x):
    hi = x.astype(BF16)
    r = x - hi.astype(F32)
    mid = r.astype(BF16)
    lo = (r - mid.astype(F32)).astype(BF16)
    return hi, mid, lo


def _inproj_kernel(has_pos, *refs):
    if has_pos:
        x_ref, pos_ref = refs[0], refs[1]
        refs = refs[2:]
    else:
        x_ref, pos_ref = refs[0], None
        refs = refs[1:]
    (mod_ref, nw_ref, wqkv_ref, wo_ref, wug_ref, wab_ref, bg_ref,
     q_ref, k_ref, v_ref, o_ref, gc_ref, gr_ref, gtot_ref, u_ref, sa_ref, sb_ref) = refs
    x = x_ref[0]
    if has_pos:
        x = x + pos_ref[...]
    h = _rms_mod(x, nw_ref[...], mod_ref[0, 0:1, :], mod_ref[0, 1:2, :])
    hb = h.astype(BF16)
    w = MLSTM_HEADS * MLSTM_HEAD_DIM
    qkv = jnp.dot(hb, wqkv_ref[...], preferred_element_type=F32)
    q_ref[0] = (qkv[:, 0:w] * (MLSTM_HEAD_DIM ** -0.5)).astype(BF16)
    k_ref[0] = qkv[:, w:2 * w].astype(BF16)
    vt = qkv[:, 2 * w:3 * w].T.astype(BF16)
    for hh in range(MLSTM_HEADS):
        v_ref[0, hh, 0] = vt[hh * MLSTM_HEAD_DIM:(hh + 1) * MLSTM_HEAD_DIM, :]
    o_ref[0] = jnp.dot(hb, wo_ref[...], preferred_element_type=F32)
    ug = jnp.dot(hb, wug_ref[...], preferred_element_type=F32)
    su = u_ref.shape[-1]
    u_ref[0] = ug[:, 0:su]
    g = ug[:, su:su + LANES] + bg_ref[...]
    col = lax.broadcasted_iota(jnp.int32, g.shape, 1)
    g = jnp.where((col % 2) == 1, _log_sigmoid(g), g)
    tm = g.shape[0]
    tril = (lax.broadcasted_iota(jnp.int32, (tm, tm), 1)
            <= lax.broadcasted_iota(jnp.int32, (tm, tm), 0)).astype(BF16)
    cs = sum(jnp.dot(tril, part, preferred_element_type=F32) for part in _split3(g))
    tot = cs[tm - 1:tm, :]
    cum = jnp.where((col % 4) < 2, cs, tot - cs + g)
    cum_l = pltpu.roll(cum, LANES - 1, axis=1)
    tot_l = pltpu.roll(jnp.broadcast_to(tot, g.shape), LANES - 1, axis=1)
    r = g - cum_l
    cumt = cum.T
    ggt = (tot_l - cum_l + g).T
    for hh in range(MLSTM_HEADS):
        gc_ref[0, hh] = r[:, 4 * hh:4 * hh + 4]
        gr_ref[0, hh, 0, 0:4, :] = cumt[4 * hh:4 * hh + 4, :]
        gr_ref[0, hh, 0, 4:8, :] = ggt[4 * hh:4 * hh + 4, :]
        gtot_ref[0, hh, 0] = tot[:, 4 * hh:4 * hh + 4]
    ab = jnp.dot(hb, wab_ref[...], preferred_element_type=F32)
    d = sa_ref.shape[-1]
    sa_ref[0] = _sigmoid(ab[:, 0:d])
    sb_ref[0] = _sigmoid(ab[:, d:2 * d])


def _inproj(x, pos, mod, mod_row_fn, nw, wts, tm, L):
    B, T, D = x.shape
    has_pos = pos is not None
    wqkv, wo, wug, wab, bg = wts
    su = wug.shape[1] - LANES
    assert tm == L, (tm, L)
    const = lambda shape: pl.BlockSpec(shape, lambda b, i: (0,) * len(shape),
                                       pipeline_mode=pl.Buffered(1))
    in_specs = [pl.BlockSpec((1, tm, D), lambda b, i: (b, i, 0))]
    args = [x]
    if has_pos:
        in_specs.append(pl.BlockSpec((tm, D), lambda b, i: (i, 0)))
        args.append(pos)
    in_specs += [pl.BlockSpec((1, N_MOD, D), lambda b, i: (mod_row_fn(b), 0, 0)),
                 const((1, D)), const(wqkv.shape), const(wo.shape), const(wug.shape),
                 const(wab.shape), const((1, LANES))]
    args += [mod, nw, wqkv, wo, wug, wab, bg]
    W = MLSTM_HEADS * MLSTM_HEAD_DIM
    tok = lambda width, dt: (jax.ShapeDtypeStruct((B, T, width), dt),
                             pl.BlockSpec((1, tm, width), lambda b, i: (b, i, 0)))
    outs = [tok(W, BF16), tok(W, BF16),
            (jax.ShapeDtypeStruct((B, MLSTM_HEADS, T // L, MLSTM_HEAD_DIM, L), BF16),
             pl.BlockSpec((1, MLSTM_HEADS, 1, MLSTM_HEAD_DIM, L), lambda b, i: (b, 0, i, 0, 0))),
            tok(W, F32),
            (jax.ShapeDtypeStruct((B, MLSTM_HEADS, T, 4), F32),
             pl.BlockSpec((1, MLSTM_HEADS, tm, 4), lambda b, i: (b, 0, i, 0))),
            (jax.ShapeDtypeStruct((B, MLSTM_HEADS, T // L, 8, L), F32),
             pl.BlockSpec((1, MLSTM_HEADS, 1, 8, L), lambda b, i: (b, 0, i, 0, 0))),
            (jax.ShapeDtypeStruct((B, MLSTM_HEADS, T // L, 1, 4), F32),
             pl.BlockSpec((1, MLSTM_HEADS, 1, 1, 4), lambda b, i: (b, 0, i, 0, 0))),
            tok(su, F32), tok(D, F32), tok(D, F32)]
    return pl.pallas_call(
        functools.partial(_inproj_kernel, has_pos),
        grid=(B, T // tm),
        in_specs=in_specs,
        out_specs=[o[1] for o in outs],
        out_shape=[o[0] for o in outs],
        compiler_params=_cparams(("arbitrary", "arbitrary")),
        name="inproj",
    )(*args)


MLSTM_AUG_ROWS = 16


def _mlstm_chunk(q, k, vt_aug, r_col, b_row, gg_row, total, C_ref, m_ref, d, valid_t):
    dh = q.shape[1]
    Ca = C_ref[d]
    m = m_ref[d]
    nt = (((1,), (1,)), ((), ()))
    rmask = jnp.where(valid_t, r_col, -jnp.inf)
    mx = jnp.maximum(m, jnp.max(rmask, axis=0, keepdims=True))
    p = jnp.exp(rmask - mx)
    w_inter = jnp.exp(m - mx)
    st = lax.dot_general(k, q, nt, preferred_element_type=F32) * p
    na = w_inter * lax.dot_general(Ca.astype(BF16), q, nt, preferred_element_type=F32) \
        + jnp.dot(vt_aug, st.astype(BF16), preferred_element_type=F32)
    rden = 1.0 / jnp.maximum(jnp.abs(na[dh:dh + 1, :]), jnp.exp(-(b_row + mx)))
    ht = na[0:dh, :] * rden
    m_new = jnp.maximum(total + m, jnp.max(gg_row, axis=1, keepdims=True))
    decay = jnp.exp(total + m - m_new)
    vw = (vt_aug.astype(F32) * jnp.exp(gg_row - m_new)).astype(BF16)
    C_ref[d] = decay * Ca + jnp.dot(vw, k, preferred_element_type=F32)
    m_ref[d] = m_new
    return ht


def _mlstm_kernel(zero_init, L, *refs):
    T, dh = refs[0].shape[1], refs[0].shape[2]
    if zero_init:
        (q_ref, k_ref, v_ref, o_ref, gc_ref, gr_ref, gtot_ref, hn_ref,
         out_ref, Co_ref, no_ref, mo_ref, C_s, m_s, hf_s, hb_s) = refs
        C_s[...] = jnp.zeros_like(C_s)
        m_s[...] = jnp.zeros_like(m_s)
    else:
        (q_ref, k_ref, v_ref, o_ref, gc_ref, gr_ref, gtot_ref, hn_ref, C0_ref, n0_ref, m0_ref,
         out_ref, C_s, m_s, hf_s, hb_s) = refs
        for d in range(2):
            C_s[d, 0:dh, :] = C0_ref[0, d, 0].T
            C_s[d, dh:dh + MLSTM_AUG_ROWS, :] = jnp.broadcast_to(n0_ref[0, d, 0], (MLSTM_AUG_ROWS, dh))
        m_s[...] = m0_ref[0, :, 0]
    nc = T // L
    row = lax.broadcasted_iota(jnp.int32, (L, L), 0)
    col = lax.broadcasted_iota(jnp.int32, (L, L), 1)
    masks_t = (row <= col, row >= col)
    ones = jnp.ones((MLSTM_AUG_ROWS, L), BF16)

    def body(c, carry):
        for d in range(2):
            cc = c if d == 0 else nc - 1 - c
            r0 = pl.multiple_of(cc * L, L)
            q = q_ref[0, pl.ds(r0, L), :]
            k = k_ref[0, pl.ds(r0, L), :]
            vt_aug = jnp.concatenate([v_ref[0, 0, cc], ones], axis=0)
            gc = gc_ref[0, 0, pl.ds(r0, L), :]
            gr = gr_ref[0, 0, cc]
            tot = gtot_ref[0, 0, cc]
            ht = _mlstm_chunk(q, k, vt_aug, gc[:, 2 * d:2 * d + 1], gr[2 * d + 1:2 * d + 2, :],
                              gr[4 + 2 * d:5 + 2 * d, :], tot[:, 2 * d + 1:2 * d + 2],
                              C_s, m_s, d, masks_t[d])
            (hf_s if d == 0 else hb_s)[cc] = ht
        return carry

    lax.fori_loop(0, nc, body, 0, unroll=2 if nc % 2 == 0 else 1)

    def fin(c, carry):
        r0 = pl.multiple_of(c * L, L)
        hm = (hf_s[c] + hb_s[c]).T * _sigmoid(o_ref[0, pl.ds(r0, L), :])
        hm = hm * lax.rsqrt(jnp.mean(hm * hm, axis=-1, keepdims=True) + EPS) * hn_ref[0]
        out_ref[0, pl.ds(r0, L), :] = hm.astype(out_ref.dtype)
        return carry

    lax.fori_loop(0, nc, fin, 0)
    if zero_init:
        for d in range(2):
            Co_ref[0, d, 0] = C_s[d, 0:dh, :].T
        no_ref[0, :, 0] = C_s[:, dh:dh + 1, :]
        mo_ref[0, :, 0] = m_s[...]


def _mlstm(q, k, v, o, gc, gr, gtot, head_norm, state, L):
    B, T, W = q.shape
    H, dh = MLSTM_HEADS, MLSTM_HEAD_DIM
    nc = T // L
    zero_init = state is None
    bh = lambda width: pl.BlockSpec((1, T, width), lambda b, h: (b, 0, h))
    in_specs = [bh(dh), bh(dh), pl.BlockSpec((1, 1, nc, dh, L), lambda b, h: (b, h, 0, 0, 0)), bh(dh),
                pl.BlockSpec((1, 1, T, 4), lambda b, h: (b, h, 0, 0)),
                pl.BlockSpec((1, 1, nc, 8, L), lambda b, h: (b, h, 0, 0, 0)),
                pl.BlockSpec((1, 1, nc, 1, 4), lambda b, h: (b, h, 0, 0, 0)),
                pl.BlockSpec((1, 1, dh), lambda b, h: (h, 0, 0))]
    args = [q, k, v, o, gc, gr, gtot, head_norm.reshape(H, 1, dh)]
    st_specs = [pl.BlockSpec((1, 2, 1, dh, dh), lambda b, h: (b, 0, h, 0, 0)),
                pl.BlockSpec((1, 2, 1, 1, dh), lambda b, h: (b, 0, h, 0, 0)),
                pl.BlockSpec((1, 2, 1, 1, 1), lambda b, h: (b, 0, h, 0, 0))]
    st_shapes = [jax.ShapeDtypeStruct((B, 2, H, dh, dh), F32),
                 jax.ShapeDtypeStruct((B, 2, H, 1, dh), F32),
                 jax.ShapeDtypeStruct((B, 2, H, 1, 1), F32)]
    out_specs = [bh(dh)]
    out_shape = [jax.ShapeDtypeStruct((B, T, W), BF16)]
    if zero_init:
        out_specs += st_specs
        out_shape += st_shapes
    else:
        C0, n0, m0 = state
        in_specs += st_specs
        args += [C0, n0.reshape(B, 2, H, 1, dh), m0.reshape(B, 2, H, 1, 1)]
    return pl.pallas_call(
        functools.partial(_mlstm_kernel, zero_init, L),
        grid=(B, H),
        in_specs=in_specs,
        out_specs=out_specs,
        out_shape=out_shape,
        scratch_shapes=[pltpu.VMEM((2, dh + MLSTM_AUG_ROWS, dh), F32), pltpu.VMEM((2, 1, 1), F32),
                        pltpu.VMEM((nc, dh, L), F32), pltpu.VMEM((nc, dh, L), F32)],
        compiler_params=_cparams(("arbitrary", "arbitrary")),
        name="mlstm",
    )(*args)


def _s5_kernel(backward, Lc, *refs):
    if backward:
        (u_ref, yf_ref, x0_ref, ar_ref, as_ref, bw_ref, cw_ref, d_ref, wg_ref, bgl_ref,
         out_ref, xl_ref, X_s, st_s, U_s, H_s) = refs
    else:
        (u_ref, x0_ref, ar_ref, as_ref, bw_ref, cw_ref, out_ref, xl_ref, X_s, st_s, U_s) = refs
    ci = pl.program_id(1)
    nj = S5_LANE_GROUPS
    half = X_s.shape[1] // (2 * nj)
    nreq = u_ref.shape[0]

    @pl.when(ci == 0)
    def _():
        st_s[...] = x0_ref[0]

    for r in range(nreq):
        for j in range(nj):
            U_s[j, pl.ds(r, Lc, stride=nreq), :] = u_ref[r, :, j * LANES:(j + 1) * LANES]
    nseg = S5_SEGMENTS
    seg_t = Lc // nseg
    seg_order = range(nseg - 1, -1, -1) if backward else range(nseg)

    def rows_of(sg):
        return slice(sg * seg_t * SUBLANES, (sg + 1) * seg_t * SUBLANES)

    for sg in seg_order:
        for j in range(nj):
            X_s[rows_of(sg), 2 * half * j:2 * half * (j + 1)] = jnp.dot(
                U_s[j, rows_of(sg), :].astype(BF16), bw_ref[j], preferred_element_type=F32)

    def swap(x):
        parts = []
        for j in range(nj):
            parts.append(x[:, 2 * half * j + half:2 * half * (j + 1)])
            parts.append(x[:, 2 * half * j:2 * half * j + half])
        return jnp.concatenate(parts, axis=1)

    x = st_s[...]
    for sg in seg_order:
        ts = range(sg * seg_t, (sg + 1) * seg_t)
        for t in (reversed(ts) if backward else ts):
            rs = slice(t * SUBLANES, (t + 1) * SUBLANES)
            x = ar_ref[...] * x + as_ref[...] * swap(x) + X_s[rs, :]
            X_s[rs, :] = x
        y = jnp.concatenate(
            [jnp.dot(X_s[rows_of(sg), 2 * half * j:2 * half * (j + 1)].astype(BF16), cw_ref[j],
                     preferred_element_type=F32) for j in range(nj)], axis=1)
        if backward:
            u = jnp.concatenate([U_s[j, rows_of(sg), :] for j in range(nj)], axis=1)
            z = yf_ref[0, rows_of(sg), :] + y + d_ref[...] * u
            gate = _sigmoid(jnp.dot(z.astype(BF16), wg_ref[...], preferred_element_type=F32) + bgl_ref[...])
            hb = jax.nn.gelu(z) * gate
            for j in range(nj):
                H_s[j, rows_of(sg), :] = hb[:, j * LANES:(j + 1) * LANES]
        else:
            out_ref[0, rows_of(sg), :] = y
    st_s[...] = x
    xl_ref[0] = x
    if backward:
        for r in range(nreq):
            for j in range(nj):
                out_ref[r, :, j * LANES:(j + 1) * LANES] = \
                    H_s[j, pl.ds(r, Lc, stride=nreq), :].astype(out_ref.dtype)


def _s5_pass(backward, u, yf, x0, ar, asg, bw, cw, glu, Lc):
    B, T, su = u.shape
    G8 = B // SUBLANES
    rows = Lc * SUBLANES
    nchunk = T // Lc
    SL = x0.shape[-1]
    nj = S5_LANE_GROUPS
    assert su == nj * LANES, su
    cidx = (lambda c: nchunk - 1 - c) if backward else (lambda c: c)
    reqspec = pl.BlockSpec((SUBLANES, Lc, su), lambda g, c: (g, cidx(c), 0))
    rowspec = pl.BlockSpec((1, rows, su), lambda g, c: (g, cidx(c), 0))
    const = lambda shape: pl.BlockSpec(shape, lambda g, c: (0,) * len(shape))
    in_specs = [reqspec]
    args = [u]
    if backward:
        in_specs.append(rowspec)
        args.append(yf)
    in_specs += [pl.BlockSpec((1, SUBLANES, SL), lambda g, c: (g, 0, 0)),
                 const((SUBLANES, SL)), const((SUBLANES, SL)), const(bw.shape), const(cw.shape)]
    args += [x0, ar, asg, bw, cw]
    scratch = [pltpu.VMEM((rows, SL), F32), pltpu.VMEM((SUBLANES, SL), F32),
               pltpu.VMEM((nj, rows, LANES), F32)]
    if backward:
        d, wg, bgl = glu
        in_specs += [const((1, su)), const(wg.shape), const((1, su))]
        args += [d, wg, bgl]
        scratch.append(pltpu.VMEM((nj, rows, LANES), F32))
        out0 = (jax.ShapeDtypeStruct((B, T, su), BF16), reqspec)
    else:
        out0 = (jax.ShapeDtypeStruct((G8, T * SUBLANES, su), F32), rowspec)
    return pl.pallas_call(
        functools.partial(_s5_kernel, backward, Lc),
        grid=(G8, nchunk),
        in_specs=in_specs,
        out_specs=[out0[1], pl.BlockSpec((1, SUBLANES, SL), lambda g, c: (g, 0, 0))],
        out_shape=[out0[0], jax.ShapeDtypeStruct((G8, SUBLANES, SL), F32)],
        scratch_shapes=scratch,
        compiler_params=_cparams(("arbitrary", "arbitrary")),
        name="s5_bwd" if backward else "s5_fwd",
    )(*args)


def _s5_discretise(a_re, a_im, log_dt, b_re, b_im, c_re, c_im):
    G, P = a_re.shape
    nj = S5_LANE_GROUPS
    gl = G // nj
    dt = jnp.exp(log_dt)[:, None]
    e = jnp.exp(a_re * dt)
    ar = e * jnp.cos(a_im * dt)
    ai = e * jnp.sin(a_im * dt)
    den = a_re * a_re + a_im * a_im
    fr = ((ar - 1.0) * a_re + ai * a_im) / den
    fi = (ai * a_re - (ar - 1.0) * a_im) / den
    bbr = fr[..., None] * b_re - fi[..., None] * b_im
    bbi = fr[..., None] * b_im + fi[..., None] * b_re

    def lanes(re, im):
        return jnp.stack([re.reshape(nj, gl * P), im.reshape(nj, gl * P)], axis=1).reshape(1, -1)

    a_row = lanes(ar, ar)
    as_row = lanes(-ai, ai)
    eye = jnp.eye(gl, dtype=F32)
    C = b_re.shape[-1]

    def bmat(bb):
        x = bb.reshape(nj, gl, P, C)
        return jnp.einsum('jgpc,gh->jgchp', x, eye).reshape(nj, gl * C, gl * P)

    bw = jnp.concatenate([bmat(bbr), bmat(bbi)], axis=2).astype(BF16)

    def cmat(cc):
        x = cc.reshape(nj, gl, C, P)
        return jnp.einsum('jgcp,gh->jgphc', x, eye).reshape(nj, gl * P, gl * C)

    cw = jnp.concatenate([cmat(c_re), cmat(-c_im)], axis=1).astype(BF16)
    return a_row, as_row, bw, cw


def _s5_state_to_lanes(s_re, s_im):
    B, G, P = s_re.shape
    nj = S5_LANE_GROUPS
    x = jnp.stack([s_re.reshape(B, nj, (G // nj) * P), s_im.reshape(B, nj, (G // nj) * P)], axis=2)
    return x.reshape(B // SUBLANES, SUBLANES, 2 * G * P)


def _s5_lanes_to_state(x, G, P):
    G8 = x.shape[0]
    nj = S5_LANE_GROUPS
    x = x.reshape(G8 * SUBLANES, nj, 2, G // nj, P)
    return x[:, :, 0].reshape(-1, G, P), x[:, :, 1].reshape(-1, G, P)


def _merge_kernel(has_pos, *refs):
    if has_pos:
        x_ref, pos_ref = refs[0], refs[1]
        refs = refs[2:]
    else:
        x_ref, pos_ref = refs[0], None
        refs = refs[1:]
    (ha_ref, hb_ref, sa_ref, sb_ref, mod_ref, nw_ref, wpa_ref, wpb_ref, wout_ref, wrh_ref, wrm_ref,
     br_ref, x1_ref, hp_ref, aff_ref) = refs
    x = x_ref[0]
    if has_pos:
        x = x + pos_ref[...]
    merged = sa_ref[0] * jnp.dot(ha_ref[0], wpa_ref[...], preferred_element_type=F32) \
        + sb_ref[0] * jnp.dot(hb_ref[0], wpb_ref[...], preferred_element_type=F32)
    out = jnp.dot(merged.astype(BF16), wout_ref[...], preferred_element_type=F32)
    x1 = x + mod_ref[0, 2:3, :] * out
    x1_ref[0] = x1
    h2 = _rms_mod(x1, nw_ref[...], mod_ref[0, 3:4, :], mod_ref[0, 4:5, :])
    tm = h2.shape[0]
    for sl in range(h2.shape[1] // LANES):
        hp_ref[0, pl.ds(sl, tm, stride=SUBLANES), :] = h2[:, sl * LANES:(sl + 1) * LANES]
    h2h, h2m, _ = _split3(h2)
    logits = jnp.dot(h2h, wrh_ref[...], preferred_element_type=F32) \
        + jnp.dot(h2m, wrh_ref[...], preferred_element_type=F32) \
        + jnp.dot(h2h, wrm_ref[...], preferred_element_type=F32)
    lt = logits.T[0:N_EXPERTS, :] + br_ref[...]
    mx = jnp.max(lt, axis=0, keepdims=True)
    ex = jnp.exp(lt - mx)
    aff_ref[0] = ex / jnp.sum(ex, axis=0, keepdims=True)


def _merge(x, pos, ha, hb, sa, sb, mod, mod_row_fn, nw, wts, tm):
    B, T, D = x.shape
    has_pos = pos is not None
    wpa, wpb, wout, wrh, wrm, br = wts
    assert D == SUBLANES * LANES, D
    const = lambda shape: pl.BlockSpec(shape, lambda b, i: (0,) * len(shape),
                                       pipeline_mode=pl.Buffered(1))
    tok = lambda width: pl.BlockSpec((1, tm, width), lambda b, i: (b, i, 0))
    in_specs = [tok(D)]
    args = [x]
    if has_pos:
        in_specs.append(pl.BlockSpec((tm, D), lambda b, i: (i, 0)))
        args.append(pos)
    in_specs += [tok(ha.shape[-1]), tok(hb.shape[-1]), tok(D), tok(D),
                 pl.BlockSpec((1, N_MOD, D), lambda b, i: (mod_row_fn(b), 0, 0)),
                 const((1, D)), const(wpa.shape), const(wpb.shape), const(wout.shape),
                 const(wrh.shape), const(wrm.shape), const(br.shape)]
    args += [ha, hb, sa, sb, mod, nw, wpa, wpb, wout, wrh, wrm, br]
    return pl.pallas_call(
        functools.partial(_merge_kernel, has_pos),
        grid=(B, T // tm),
        in_specs=in_specs,
        out_specs=[tok(D), pl.BlockSpec((1, tm * SUBLANES, LANES), lambda b, i: (b, i, 0)),
                   pl.BlockSpec((1, N_EXPERTS, tm), lambda b, i: (b, 0, i))],
        out_shape=[jax.ShapeDtypeStruct((B, T, D), F32),
                   jax.ShapeDtypeStruct((B, T * SUBLANES, LANES), F32),
                   jax.ShapeDtypeStruct((B, N_EXPERTS, T), F32)],
        compiler_params=_cparams(("arbitrary", "arbitrary")),
        name="merge",
    )(*args)


def _route_kernel(cap, aff_ref, idx_ref, val_ref, jb_s, jl_s):
    aff = aff_ref[0]
    E, T = aff.shape
    thr_bits = jnp.zeros((E, 1), jnp.int32)
    for bit in range(30, -1, -1):
        cand = thr_bits | (1 << bit)
        cnt = jnp.sum((aff >= lax.bitcast_convert_type(cand, F32)).astype(F32), axis=1, keepdims=True)
        thr_bits = jnp.where(cnt >= cap, cand, thr_bits)
    thr = lax.bitcast_convert_type(thr_bits, F32)
    nxt = lax.bitcast_convert_type(thr_bits + 1, F32)
    gt = aff >= nxt
    eq = (aff >= thr) & jnp.logical_not(gt)
    need = cap - jnp.sum(gt.astype(F32), axis=1, keepdims=True)

    r = lax.broadcasted_iota(jnp.int32, (LANES, LANES), 0)
    c = lax.broadcasted_iota(jnp.int32, (LANES, LANES), 1)
    tri = (r < c).astype(BF16)

    def excl_cumsum(mk):
        outs = []
        carry = jnp.zeros((E, 1), F32)
        for kb in range(T // LANES):
            blk = mk[:, kb * LANES:(kb + 1) * LANES]
            outs.append(jnp.dot(blk.astype(BF16), tri, preferred_element_type=F32) + carry)
            carry = carry + jnp.sum(blk.astype(F32), axis=1, keepdims=True)
        return jnp.concatenate(outs, axis=1)

    tie_rank = excl_cumsum(eq)
    mask = gt | (eq & (tie_rank < need))
    slot = jnp.where(mask, excl_cumsum(mask), -1.0).astype(jnp.int32)
    JL = min(cap, LANES)
    njb = cap // JL
    NC = 5
    WR = -(-(njb * NC) // SUBLANES) * SUBLANES
    jb_s[...] = slot >> (JL.bit_length() - 1)
    jl_s[...] = slot & (JL - 1)
    tio = lax.broadcasted_iota(jnp.int32, (1, T), 1)
    t_hi = (tio >> 6).astype(F32)
    t_lo = (tio & 63).astype(F32)
    jl_col = lax.broadcasted_iota(jnp.int32, (JL, 1), 0)
    wrow = lax.broadcasted_iota(jnp.int32, (WR, 1), 0)
    wjb = jnp.full((WR, 1), -2, jnp.int32)
    for jb in range(njb):
        wjb = jnp.where((wrow >= NC * jb) & (wrow < NC * (jb + 1)), jb, wjb)
    wcomp = wrow - NC * wjb
    lane = lax.broadcasted_iota(jnp.int32, (JL, E), 1)
    nt = (((1,), (1,)), ((), ()))

    def per_expert(e, carry):
        a1, a2, a3 = _split3(aff_ref[0, pl.ds(e, 1), :])
        comp = jnp.where(wcomp == 0, t_hi,
                         jnp.where(wcomp == 1, t_lo,
                                   jnp.where(wcomp == 2, a1.astype(F32),
                                             jnp.where(wcomp == 3, a2.astype(F32), a3.astype(F32)))))
        wt = jnp.where(jb_s[pl.ds(e, 1), :] == wjb, comp, 0.0).astype(BF16)
        onehot = jnp.where(jl_s[pl.ds(e, 1), :] == jl_col, 1.0, 0.0).astype(BF16)
        r = lax.dot_general(onehot, wt, nt, preferred_element_type=F32)
        out = []
        for jb in range(njb):
            c0 = jb * NC
            ie = r[:, c0:c0 + 1] * 64.0 + r[:, c0 + 1:c0 + 2]
            ve = r[:, c0 + 2:c0 + 3] + r[:, c0 + 3:c0 + 4] + r[:, c0 + 4:c0 + 5]
            out.append(jnp.where(lane == e, ie, carry[2 * jb]))
            out.append(jnp.where(lane == e, ve, carry[2 * jb + 1]))
        return tuple(out)

    res = lax.fori_loop(0, E, per_expert, tuple(jnp.zeros((JL, E), F32) for _ in range(2 * njb)),
                        unroll=2)
    for jb in range(njb):
        idx_ref[0, jb * JL:(jb + 1) * JL, :] = res[2 * jb].astype(jnp.int32)
        val_ref[0, jb * JL:(jb + 1) * JL, :] = res[2 * jb + 1]


def _route(aff, cap):
    B, E, T = aff.shape
    return pl.pallas_call(
        functools.partial(_route_kernel, cap),
        grid=(B,),
        in_specs=[pl.BlockSpec((1, E, T), lambda b: (b, 0, 0))],
        out_specs=[pl.BlockSpec((1, cap, E), lambda b: (b, 0, 0)),
                   pl.BlockSpec((1, cap, E), lambda b: (b, 0, 0))],
        out_shape=[jax.ShapeDtypeStruct((B, cap, E), jnp.int32),
                   jax.ShapeDtypeStruct((B, cap, E), F32)],
        scratch_shapes=[pltpu.VMEM((E, T), jnp.int32), pltpu.VMEM((E, T), jnp.int32)],
        compiler_params=_cparams(("arbitrary",)),
        name="route",
    )(aff)


def _moe_kernel(idx_ref, idxn_ref, tok_ref, val_ref, w1_ref, w3_ref, w2_ref, out_ref,
                xe_s, xb_s, ye_s, yt_s, acc_s, sem):
    b, e, f = pl.program_id(0), pl.program_id(1), pl.program_id(2)
    ne = pl.num_programs(1)
    cap = ye_s.shape[0]
    S = SUBLANES
    GR = 8

    def tile(ref, r):
        return ref.at[pl.ds(pl.multiple_of(r * S, S), S), :]

    def gather_group(iref, g):
        rows = [iref[0, 0, g * GR + i] for i in range(GR)]
        vals = [tile(tok_ref.at[0], rows[i])[...] for i in range(GR)]
        for i in range(GR):
            tile(xe_s, g * GR + i)[...] = vals[i]

    def scatter_group(iref, g):
        rows = [iref[0, 0, g * GR + i] for i in range(GR)]
        vals = [tile(acc_s, rows[i])[...] + tile(yt_s, g * GR + i)[...] for i in range(GR)]
        for i in range(GR):
            tile(acc_s, rows[i])[...] = vals[i]

    def ffn_half():
        xe = xb_s[...]
        h1 = jnp.dot(xe, w1_ref[0], preferred_element_type=F32)
        h3 = jnp.dot(xe, w3_ref[0], preferred_element_type=F32)
        hid = (h1 * _sigmoid(h1) * h3).astype(BF16)
        return jnp.dot(hid, w2_ref[0], preferred_element_type=F32)

    @pl.when((e == 0) & (f == 0))
    def _():
        acc_s[...] = jnp.zeros_like(acc_s)
        yt_s[...] = jnp.zeros_like(yt_s)
        lax.fori_loop(0, cap // GR, lambda g, c: (gather_group(idx_ref, g), c)[1], 0)

    @pl.when(f == 0)
    def _():
        xb_s[...] = jnp.concatenate([xe_s[pl.ds(sl, cap, stride=S), :] for sl in range(S)],
                                    axis=1).astype(BF16)
        ye_s[...] = ffn_half()
        for g in range(cap // GR):
            scatter_group(idxn_ref, g)

    @pl.when(f == 1)
    def _():
        yw = (ye_s[...] + ffn_half()) * val_ref[0, 0]
        for sl in range(S):
            yt_s[pl.ds(sl, cap, stride=S), :] = yw[:, sl * LANES:(sl + 1) * LANES]
        for g in range(cap // GR):
            gather_group(idxn_ref, g)

    @pl.when((e == ne - 1) & (f == 1))
    def _():
        lax.fori_loop(0, cap // GR, lambda g, c: (scatter_group(idx_ref, g), c)[1], 0)
        cp = pltpu.make_async_copy(acc_s, out_ref.at[b], sem)
        cp.start()
        cp.wait()


def _moe(idx, tok, val, w1, w3, w2):
    B, NR, _ = tok.shape
    E, D, FF = w1.shape
    cap = idx.shape[-1]
    nf = EXPERT_FF_SPLIT
    assert nf == 2, nf
    fh = FF // nf
    return pl.pallas_call(
        _moe_kernel,
        grid=(B, E, nf),
        in_specs=[pl.BlockSpec((1, 1, cap), lambda b, e, f: (b * E + e, 0, 0), memory_space=pltpu.SMEM),
                  pl.BlockSpec((1, 1, cap), lambda b, e, f: (b * E + jnp.clip(e - 1 + 2 * f, 0, E - 1), 0, 0),
                               memory_space=pltpu.SMEM),
                  pl.BlockSpec((1, NR, LANES), lambda b, e, f: (b, 0, 0), pipeline_mode=pl.Buffered(1)),
                  pl.BlockSpec((1, 1, cap, 1), lambda b, e, f: (b, e, 0, 0)),
                  pl.BlockSpec((1, D, fh), lambda b, e, f: (e, 0, f)),
                  pl.BlockSpec((1, D, fh), lambda b, e, f: (e, 0, f)),
                  pl.BlockSpec((1, fh, D), lambda b, e, f: (e, f, 0))],
        out_specs=pl.BlockSpec(memory_space=pl.ANY),
        out_shape=jax.ShapeDtypeStruct((B, NR, LANES), F32),
        scratch_shapes=[pltpu.VMEM((cap * SUBLANES, LANES), F32), pltpu.VMEM((cap, D), BF16),
                        pltpu.VMEM((cap, D), F32), pltpu.VMEM((cap * SUBLANES, LANES), F32),
                        pltpu.VMEM((NR, LANES), F32), pltpu.SemaphoreType.DMA(())],
        compiler_params=_cparams(("arbitrary", "arbitrary", "arbitrary")),
        name="moe",
    )(idx, idx, tok, val, w1, w3, w2)


def _final_kernel(x1_ref, moe_ref, mod_ref, nw_ref, y_ref):
    tm = x1_ref.shape[1]
    moe = jnp.concatenate([moe_ref[0, pl.ds(sl, tm, stride=SUBLANES), :] for sl in range(SUBLANES)], axis=1)
    x = x1_ref[0] + mod_ref[0, 5:6, :] * moe
    ms = jnp.mean(x * x, axis=-1, keepdims=True)
    y_ref[0] = x * lax.rsqrt(ms + EPS) * nw_ref[...]


def _final(x1, moe, mod, mod_row_fn, nw, tm):
    B, T, D = x1.shape
    tok = pl.BlockSpec((1, tm, D), lambda b, i: (b, i, 0))
    return pl.pallas_call(
        _final_kernel,
        grid=(B, T // tm),
        in_specs=[tok, pl.BlockSpec((1, tm * SUBLANES, LANES), lambda b, i: (b, i, 0)),
                  pl.BlockSpec((1, N_MOD, D), lambda b, i: (mod_row_fn(b), 0, 0)),
                  pl.BlockSpec((1, D), lambda b, i: (0, 0))],
        out_specs=tok,
        out_shape=jax.ShapeDtypeStruct((B, T, D), F32),
        compiler_params=_cparams(("arbitrary", "arbitrary")),
        name="final",
    )(x1, moe, mod, nw)


def _pos2d(T, D):
    rows = T // GRID_W
    quarter = D // 4
    freqs = 1.0 / (10000.0 ** (jnp.arange(quarter, dtype=F32) / quarter))
    er = jnp.arange(rows, dtype=F32)[:, None] * freqs
    ec = jnp.arange(GRID_W, dtype=F32)[:, None] * freqs
    rep = lambda a: jnp.repeat(a, GRID_W, axis=0)
    til = lambda a: jnp.tile(a, (rows, 1))
    return jnp.concatenate([rep(jnp.sin(er)), rep(jnp.cos(er)), til(jnp.sin(ec)), til(jnp.cos(ec))], axis=-1)


def _stream(x, pos, mod, mod_row_fn, P, mstate, s5_x0, tm):
    B, T, D = x.shape
    L = min(MLSTM_CHUNK, T)
    q, k, v, o, gc, gr, gtot, u, sa, sb = _inproj(x, pos, mod, mod_row_fn, P['norm_mix'],
                                                   P['w_in_parts'], tm, L)
    res = _mlstm(q, k, v, o, gc, gr, gtot, P['head_norm'], mstate, L)
    ha, mout = res[0], res[1:]
    Lc = min(S5_CHUNK, T)
    yf, xf = _s5_pass(False, u, None, s5_x0[0], *P['s5_dir'][0], None, Lc)
    hb, xb = _s5_pass(True, u, yf, s5_x0[1], *P['s5_dir'][1], P['glu'], Lc)
    x1, hp, aff = _merge(x, pos, ha, hb, sa, sb, mod, mod_row_fn, P['norm_ffn'], P['merge_w'], tm)
    cap = CAPACITY_FACTOR * T // N_EXPERTS
    idx, val = _route(aff, cap)
    E = N_EXPERTS
    gb = max(1, min(B, MOE_TABLE_ROWS // T))
    nb = B // gb
    offs = (jnp.arange(B, dtype=jnp.int32) % gb) * T
    idx_f = (idx + offs[:, None, None]).reshape(nb, gb, cap, E).transpose(0, 3, 1, 2).reshape(nb * E, 1, gb * cap)
    val_f = val.reshape(nb, gb, cap, E).transpose(0, 3, 1, 2).reshape(nb, E, gb * cap, 1)
    moe = _moe(idx_f, hp.reshape(nb, gb * T * SUBLANES, LANES), val_f, *P['experts'])
    y = _final(x1, moe.reshape(B, T * SUBLANES, LANES), mod, mod_row_fn, P['norm_final'], tm)
    return y, mout, (xf, xb)


def kernel(x_prompt, x_sample, state_mlstm_C, state_mlstm_n, state_mlstm_m, state_s5_re, state_s5_im,
           c, c_ctx, w_ada, b_ada, norm_mix, norm_ffn, w_in, b_gate, head_norm,
           s5_a_re, s5_a_im, s5_log_dt, s5_b_re, s5_b_im, s5_c_re, s5_c_im, s5_d, w_glu, b_glu,
           w_pA, w_pB, w_out, w_router, b_router, w_e1, w_e3, w_e2, norm_final):
    Bp, Tp, D = x_prompt.shape
    Bs, Ts, _ = x_sample.shape
    depth = w_ada.shape[0]
    H, dh = MLSTM_HEADS, MLSTM_HEAD_DIM
    W = H * dh
    G, Pn = s5_a_re.shape[2], s5_a_re.shape[3]
    SW = G * S5_GROUP

    assert depth == 1, depth
    l = 0
    pos = _pos2d(Ts, D)
    nrow = Bs + 1
    rows = -(-nrow // SUBLANES) * SUBLANES
    cpad = jnp.zeros((rows, D), F32).at[0:Bs].set(c).at[Bs].set(c_ctx)
    mod = _adaln(cpad, w_ada[l], b_ada[l]).reshape(rows, N_MOD, D)

    wi = w_in[l]
    gcols = wi[:, 4 * W:4 * W + 4 * H].reshape(D, 4, H).transpose(0, 2, 1).reshape(D, 4 * H)
    wug = jnp.concatenate([wi[:, 4 * W + 4 * H:4 * W + 4 * H + SW], gcols,
                           jnp.zeros((D, LANES - 4 * H), F32)], axis=1).astype(BF16)
    bg = jnp.concatenate([b_gate[l].T.reshape(1, 4 * H), jnp.zeros((1, LANES - 4 * H), F32)], axis=1)
    w_in_parts = (wi[:, 0:3 * W].astype(BF16), wi[:, 3 * W:4 * W].astype(BF16), wug,
                  wi[:, 4 * W + 4 * H + SW:].astype(BF16), bg)
    s5_dir = [_s5_discretise(s5_a_re[l, d], s5_a_im[l, d], s5_log_dt[l, d], s5_b_re[l], s5_b_im[l],
                             s5_c_re[l, d], s5_c_im[l, d]) for d in range(2)]
    s5_dir = [(jnp.broadcast_to(a, (SUBLANES, a.shape[1])), jnp.broadcast_to(s, (SUBLANES, s.shape[1])), bw, cw)
              for (a, s, bw, cw) in s5_dir]
    wr = jnp.concatenate([w_router[l], jnp.zeros((D, LANES - N_EXPERTS), F32)], axis=1)
    wr_hi = wr.astype(BF16)
    wr_mid = (wr - wr_hi.astype(F32)).astype(BF16)
    P = {
        'norm_mix': norm_mix[l].reshape(1, D), 'norm_ffn': norm_ffn[l].reshape(1, D),
        'norm_final': norm_final.reshape(1, D),
        'w_in_parts': w_in_parts, 'head_norm': head_norm[l], 's5_dir': s5_dir,
        'glu': (s5_d[l].reshape(1, SW), w_glu[l].astype(BF16), b_glu[l].reshape(1, SW)),
        'merge_w': (w_pA[l].astype(BF16), w_pB[l].astype(BF16), w_out[l].astype(BF16), wr_hi, wr_mid,
                    b_router[l].reshape(N_EXPERTS, 1)),
        'experts': (w_e1[l].astype(BF16), w_e3[l].astype(BF16), w_e2[l].astype(BF16)),
    }
    zero_x0 = jnp.zeros((Bp // SUBLANES, SUBLANES, 2 * G * Pn), F32)
    yp, mout, (xf, xb) = _stream(x_prompt, None, mod, lambda b: Bs, P, None, (zero_x0, zero_x0),
                                 min(MLSTM_CHUNK, Tp))
    lat_x0 = [_s5_state_to_lanes(state_s5_re[:, l, d], state_s5_im[:, l, d]) for d in range(2)]
    ys, _, _ = _stream(x_sample, pos, mod, lambda b: b, P,
                       (state_mlstm_C[:, l], state_mlstm_n[:, l], state_mlstm_m[:, l]), lat_x0,
                       min(MLSTM_CHUNK, Ts))
    Cn, nn, mn = mout
    fre, fim = _s5_lanes_to_state(xf, G, Pn)
    bre, bim = _s5_lanes_to_state(xb, G, Pn)
    outs = (Cn, nn.reshape(Bp, 2, H, dh), mn.reshape(Bp, 2, H),
            jnp.stack([fre, bre], axis=1), jnp.stack([fim, bim], axis=1))
    out_C, out_n, out_m, out_re, out_im = (o[:, None] for o in outs)
    return (yp, ys, out_C, out_n, out_m, out_re, out_im)
```

```python
import functools

import jax
import jax.numpy as jnp
from jax import lax
from jax.experimental import pallas as pl
from jax.experimental.pallas import tpu as pltpu

F32 = jnp.float32
BF16 = jnp.bfloat16
EPS = 1e-6

MLSTM_HEADS = 4
MLSTM_HEAD_DIM = 256
N_EXPERTS = 16
CAPACITY_FACTOR = 2
S5_GROUP = 16
S5_STATE = 64
GRID_W = 64
N_MOD = 6

LANES = 128
SUBLANES = 8
VMEM_LIMIT_BYTES = 60000 * 1024

MLSTM_CHUNK = 256
S5_CHUNK = 64
S5_LANE_GROUPS = 4
S5_SEGMENTS = 2
EXPERT_FF_SPLIT = 2
ROUTE_ROW_ELEMS = 16 * 4096
MOE_OUT_CHUNK = 256
MOE_TABLE_ROWS = 4096


def _cparams(sem):
    return pltpu.CompilerParams(dimension_semantics=sem, vmem_limit_bytes=VMEM_LIMIT_BYTES)


def _sigmoid(x):
    return jax.nn.sigmoid(x)


def _log_sigmoid(x):
    return jnp.minimum(x, 0.0) - jnp.log1p(jnp.exp(-jnp.abs(x)))


def _adaln_kernel(c_ref, w_ref, b_ref, o_ref):
    c = c_ref[...]
    s = c * _sigmoid(c)
    o_ref[...] = jnp.dot(s, w_ref[...], precision=lax.Precision.HIGHEST,
                         preferred_element_type=F32) + b_ref[...]


def _adaln(cpad, w, b):
    rows, d = cpad.shape
    n = w.shape[1]
    tn = 1536
    return pl.pallas_call(
        _adaln_kernel,
        grid=(n // tn,),
        in_specs=[pl.BlockSpec((rows, d), lambda j: (0, 0)),
                  pl.BlockSpec((d, tn), lambda j: (0, j)),
                  pl.BlockSpec((1, tn), lambda j: (0, j))],
        out_specs=pl.BlockSpec((rows, tn), lambda j: (0, j)),
        out_shape=jax.ShapeDtypeStruct((rows, n), F32),
        compiler_params=_cparams(("arbitrary",)),
        name="adaln",
    )(cpad, w, b.reshape(1, n))


def _rms_mod(x, nw, shift, scale):
    ms = jnp.mean(x * x, axis=-1, keepdims=True)
    y = x * lax.rsqrt(ms + EPS) * nw
    return y * (1.0 + scale) + shift


def _split3(x):
    hi = x.astype(BF16)
    r = x - hi.astype(F32)
    mid = r.astype(BF16)
    lo = (r - mid.astype(F32)).astype(BF16)
    return hi, mid, lo


def _inproj_kernel(has_pos, *refs):
    if has_pos:
        x_ref, pos_ref = refs[0], refs[1]
        refs = refs[2:]
    else:
        x_ref, pos_ref = refs[0], None
        refs = refs[1:]
    (mod_ref, nw_ref, wqkv_ref, wo_ref, wug_ref, wab_ref, bg_ref,
     q_ref, k_ref, v_ref, o_ref, gc_ref, gr_ref, gtot_ref, u_ref, sa_ref, sb_ref) = refs
    x = x_ref[0]
    if has_pos:
        x = x + pos_ref[...]
    h = _rms_mod(x, nw_ref[...], mod_ref[0, 0:1, :], mod_ref[0, 1:2, :])
    hb = h.astype(BF16)
    w = MLSTM_HEADS * MLSTM_HEAD_DIM
    qkv = jnp.dot(hb, wqkv_ref[...], preferred_element_type=F32)
    q_ref[0] = (qkv[:, 0:w] * (MLSTM_HEAD_DIM ** -0.5)).astype(BF16)
    k_ref[0] = qkv[:, w:2 * w].astype(BF16)
    vt = qkv[:, 2 * w:3 * w].T.astype(BF16)
    for hh in range(MLSTM_HEADS):
        v_ref[0, hh, 0] = vt[hh * MLSTM_HEAD_DIM:(hh + 1) * MLSTM_HEAD_DIM, :]
    o_ref[0] = jnp.dot(hb, wo_ref[...], preferred_element_type=F32)
    ug = jnp.dot(hb, wug_ref[...], preferred_element_type=F32)
    su = u_ref.shape[-1]
    u_ref[0] = ug[:, 0:su]
    g = ug[:, su:su + LANES] + bg_ref[...]
    col = lax.broadcasted_iota(jnp.int32, g.shape, 1)
    g = jnp.where((col % 2) == 1, _log_sigmoid(g), g)
    tm = g.shape[0]
    tril = (lax.broadcasted_iota(jnp.int32, (tm, tm), 1)
            <= lax.broadcasted_iota(jnp.int32, (tm, tm), 0)).astype(BF16)
    cs = sum(jnp.dot(tril, part, preferred_element_type=F32) for part in _split3(g))
    tot = cs[tm - 1:tm, :]
    cum = jnp.where((col % 4) < 2, cs, tot - cs + g)
    cum_l = pltpu.roll(cum, LANES - 1, axis=1)
    tot_l = pltpu.roll(jnp.broadcast_to(tot, g.shape), LANES - 1, axis=1)
    r = g - cum_l
    cumt = cum.T
    ggt = (tot_l - cum_l + g).T
    for hh in range(MLSTM_HEADS):
        gc_ref[0, hh] = r[:, 4 * hh:4 * hh + 4]
        gr_ref[0, hh, 0, 0:4, :] = cumt[4 * hh:4 * hh + 4, :]
        gr_ref[0, hh, 0, 4:8, :] = ggt[4 * hh:4 * hh + 4, :]
        gtot_ref[0, hh, 0] = tot[:, 4 * hh:4 * hh + 4]
    ab = jnp.dot(hb, wab_ref[...], preferred_element_type=F32)
    d = sa_ref.shape[-1]
    sa_ref[0] = _sigmoid(ab[:, 0:d])
    sb_ref[0] = _sigmoid(ab[:, d:2 * d])


def _inproj(x, pos, mod, mod_row_fn, nw, wts, tm, L):
    B, T, D = x.shape
    has_pos = pos is not None
    wqkv, wo, wug, wab, bg = wts
    su = wug.shape[1] - LANES
    assert tm == L, (tm, L)
    const = lambda shape: pl.BlockSpec(shape, lambda b, i: (0,) * len(shape),
                                       pipeline_mode=pl.Buffered(1))
    in_specs = [pl.BlockSpec((1, tm, D), lambda b, i: (b, i, 0))]
    args = [x]
    if has_pos:
        in_specs.append(pl.BlockSpec((tm, D), lambda b, i: (i, 0)))
        args.append(pos)
    in_specs += [pl.BlockSpec((1, N_MOD, D), lambda b, i: (mod_row_fn(b), 0, 0)),
                 const((1, D)), const(wqkv.shape), const(wo.shape), const(wug.shape),
                 const(wab.shape), const((1, LANES))]
    args += [mod, nw, wqkv, wo, wug, wab, bg]
    W = MLSTM_HEADS * MLSTM_HEAD_DIM
    tok = lambda width, dt: (jax.ShapeDtypeStruct((B, T, width), dt),
                             pl.BlockSpec((1, tm, width), lambda b, i: (b, i, 0)))
    outs = [tok(W, BF16), tok(W, BF16),
            (jax.ShapeDtypeStruct((B, MLSTM_HEADS, T // L, MLSTM_HEAD_DIM, L), BF16),
             pl.BlockSpec((1, MLSTM_HEADS, 1, MLSTM_HEAD_DIM, L), lambda b, i: (b, 0, i, 0, 0))),
            tok(W, F32),
            (jax.ShapeDtypeStruct((B, MLSTM_HEADS, T, 4), F32),
             pl.BlockSpec((1, MLSTM_HEADS, tm, 4), lambda b, i: (b, 0, i, 0))),
            (jax.ShapeDtypeStruct((B, MLSTM_HEADS, T // L, 8, L), F32),
             pl.BlockSpec((1, MLSTM_HEADS, 1, 8, L), lambda b, i: (b, 0, i, 0, 0))),
            (jax.ShapeDtypeStruct((B, MLSTM_HEADS, T // L, 1, 4), F32),
             pl.BlockSpec((1, MLSTM_HEADS, 1, 1, 4), lambda b, i: (b, 0, i, 0, 0))),
            tok(su, F32), tok(D, F32), tok(D, F32)]
    return pl.pallas_call(
        functools.partial(_inproj_kernel, has_pos),
        grid=(B, T // tm),
        in_specs=in_specs,
        out_specs=[o[1] for o in outs],
        out_shape=[o[0] for o in outs],
        compiler_params=_cparams(("arbitrary", "arbitrary")),
        name="inproj",
    )(*args)


MLSTM_AUG_ROWS = 16


def _mlstm_chunk(q, k, vt_aug, r_col, b_row, gg_row, total, C_ref, m_ref, d, valid_t):
    dh = q.shape[1]
    Ca = C_ref[d]
    m = m_ref[d]
    nt = (((1,), (1,)), ((), ()))
    rmask = jnp.where(valid_t, r_col, -jnp.inf)
    mx = jnp.maximum(m, jnp.max(rmask, axis=0, keepdims=True))
    p = jnp.exp(rmask - mx)
    w_inter = jnp.exp(m - mx)
    st = lax.dot_general(k, q, nt, preferred_element_type=F32) * p
    na = w_inter * lax.dot_general(Ca.astype(BF16), q, nt, preferred_element_type=F32) \
        + jnp.dot(vt_aug, st.astype(BF16), preferred_element_type=F32)
    rden = 1.0 / jnp.maximum(jnp.abs(na[dh:dh + 1, :]), jnp.exp(-(b_row + mx)))
    ht = na[0:dh, :] * rden
    m_new = jnp.maximum(total + m, jnp.max(gg_row, axis=1, keepdims=True))
    decay = jnp.exp(total + m - m_new)
    vw = (vt_aug.astype(F32) * jnp.exp(gg_row - m_new)).astype(BF16)
    C_ref[d] = decay * Ca + jnp.dot(vw, k, preferred_element_type=F32)
    m_ref[d] = m_new
    return ht


def _mlstm_kernel(zero_init, L, *refs):
    T, dh = refs[0].shape[1], refs[0].shape[2]
    if zero_init:
        (q_ref, k_ref, v_ref, o_ref, gc_ref, gr_ref, gtot_ref, hn_ref,
         out_ref, Co_ref, no_ref, mo_ref, C_s, m_s, hf_s, hb_s) = refs
        C_s[...] = jnp.zeros_like(C_s)
        m_s[...] = jnp.zeros_like(m_s)
    else:
        (q_ref, k_ref, v_ref, o_ref, gc_ref, gr_ref, gtot_ref, hn_ref, C0_ref, n0_ref, m0_ref,
         out_ref, C_s, m_s, hf_s, hb_s) = refs
        for d in range(2):
            C_s[d, 0:dh, :] = C0_ref[0, d, 0].T
            C_s[d, dh:dh + MLSTM_AUG_ROWS, :] = jnp.broadcast_to(n0_ref[0, d, 0], (MLSTM_AUG_ROWS, dh))
        m_s[...] = m0_ref[0, :, 0]
    nc = T // L
    row = lax.broadcasted_iota(jnp.int32, (L, L), 0)
    col = lax.broadcasted_iota(jnp.int32, (L, L), 1)
    masks_t = (row <= col, row >= col)
    ones = jnp.ones((MLSTM_AUG_ROWS, L), BF16)

    def body(c, carry):
        for d in range(2):
            cc = c if d == 0 else nc - 1 - c
            r0 = pl.multiple_of(cc * L, L)
            q = q_ref[0, pl.ds(r0, L), :]
            k = k_ref[0, pl.ds(r0, L), :]
            vt_aug = jnp.concatenate([v_ref[0, 0, cc], ones], axis=0)
            gc = gc_ref[0, 0, pl.ds(r0, L), :]
            gr = gr_ref[0, 0, cc]
            tot = gtot_ref[0, 0, cc]
            ht = _mlstm_chunk(q, k, vt_aug, gc[:, 2 * d:2 * d + 1], gr[2 * d + 1:2 * d + 2, :],
                              gr[4 + 2 * d:5 + 2 * d, :], tot[:, 2 * d + 1:2 * d + 2],
                              C_s, m_s, d, masks_t[d])
            (hf_s if d == 0 else hb_s)[cc] = ht
        return carry

    lax.fori_loop(0, nc, body, 0, unroll=2 if nc % 2 == 0 else 1)

    def fin(c, carry):
        r0 = pl.multiple_of(c * L, L)
        hm = (hf_s[c] + hb_s[c]).T * _sigmoid(o_ref[0, pl.ds(r0, L), :])
        hm = hm * lax.rsqrt(jnp.mean(hm * hm, axis=-1, keepdims=True) + EPS) * hn_ref[0]
        out_ref[0, pl.ds(r0, L), :] = hm.astype(out_ref.dtype)
        return carry

    lax.fori_loop(0, nc, fin, 0)
    if zero_init:
        for d in range(2):
            Co_ref[0, d, 0] = C_s[d, 0:dh, :].T
        no_ref[0, :, 0] = C_s[:, dh:dh + 1, :]
        mo_ref[0, :, 0] = m_s[...]


def _mlstm(q, k, v, o, gc, gr, gtot, head_norm, state, L):
    B, T, W = q.shape
    H, dh = MLSTM_HEADS, MLSTM_HEAD_DIM
    nc = T // L
    zero_init = state is None
    bh = lambda width: pl.BlockSpec((1, T, width), lambda b, h: (b, 0, h))
    in_specs = [bh(dh), bh(dh), pl.BlockSpec((1, 1, nc, dh, L), lambda b, h: (b, h, 0, 0, 0)), bh(dh),
                pl.BlockSpec((1, 1, T, 4), lambda b, h: (b, h, 0, 0)),
                pl.BlockSpec((1, 1, nc, 8, L), lambda b, h: (b, h, 0, 0, 0)),
                pl.BlockSpec((1, 1, nc, 1, 4), lambda b, h: (b, h, 0, 0, 0)),
                pl.BlockSpec((1, 1, dh), lambda b, h: (h, 0, 0))]
    args = [q, k, v, o, gc, gr, gtot, head_norm.reshape(H, 1, dh)]
    st_specs = [pl.BlockSpec((1, 2, 1, dh, dh), lambda b, h: (b, 0, h, 0, 0)),
                pl.BlockSpec((1, 2, 1, 1, dh), lambda b, h: (b, 0, h, 0, 0)),
                pl.BlockSpec((1, 2, 1, 1, 1), lambda b, h: (b, 0, h, 0, 0))]
    st_shapes = [jax.ShapeDtypeStruct((B, 2, H, dh, dh), F32),
                 jax.ShapeDtypeStruct((B, 2, H, 1, dh), F32),
                 jax.ShapeDtypeStruct((B, 2, H, 1, 1), F32)]
    out_specs = [bh(dh)]
    out_shape = [jax.ShapeDtypeStruct((B, T, W), BF16)]
    if zero_init:
        out_specs += st_specs
        out_shape += st_shapes
    else:
        C0, n0, m0 = state
        in_specs += st_specs
        args += [C0, n0.reshape(B, 2, H, 1, dh), m0.reshape(B, 2, H, 1, 1)]
    return pl.pallas_call(
        functools.partial(_mlstm_kernel, zero_init, L),
        grid=(B, H),
        in_specs=in_specs,
        out_specs=out_specs,
        out_shape=out_shape,
        scratch_shapes=[pltpu.VMEM((2, dh + MLSTM_AUG_ROWS, dh), F32), pltpu.VMEM((2, 1, 1), F32),
                        pltpu.VMEM((nc, dh, L), F32), pltpu.VMEM((nc, dh, L), F32)],
        compiler_params=_cparams(("arbitrary", "arbitrary")),
        name="mlstm",
    )(*args)


def _s5_kernel(backward, Lc, *refs):
    if backward:
        (u_ref, yf_ref, x0_ref, ar_ref, as_ref, bw_ref, cw_ref, d_ref, wg_ref, bgl_ref,
         out_ref, xl_ref, X_s, st_s, U_s, H_s) = refs
    else:
        (u_ref, x0_ref, ar_ref, as_ref, bw_ref, cw_ref, out_ref, xl_ref, X_s, st_s, U_s) = refs
    ci = pl.program_id(1)
    nj = S5_LANE_GROUPS
    half = X_s.shape[1] // (2 * nj)
    nreq = u_ref.shape[0]

    @pl.when(ci == 0)
    def _():
        st_s[...] = x0_ref[0]

    for r in range(nreq):
        for j in range(nj):
            U_s[j, pl.ds(r, Lc, stride=nreq), :] = u_ref[r, :, j * LANES:(j + 1) * LANES]
    nseg = S5_SEGMENTS
    seg_t = Lc // nseg
    seg_order = range(nseg - 1, -1, -1) if backward else range(nseg)

    def rows_of(sg):
        return slice(sg * seg_t * SUBLANES, (sg + 1) * seg_t * SUBLANES)

    for sg in seg_order:
        for j in range(nj):
            X_s[rows_of(sg), 2 * half * j:2 * half * (j + 1)] = jnp.dot(
                U_s[j, rows_of(sg), :].astype(BF16), bw_ref[j], preferred_element_type=F32)

    def swap(x):
        parts = []
        for j in range(nj):
            parts.append(x[:, 2 * half * j + half:2 * half * (j + 1)])
            parts.append(x[:, 2 * half * j:2 * half * j + half])
        return jnp.concatenate(parts, axis=1)

    x = st_s[...]
    for sg in seg_order:
        ts = range(sg * seg_t, (sg + 1) * seg_t)
        for t in (reversed(ts) if backward else ts):
            rs = slice(t * SUBLANES, (t + 1) * SUBLANES)
            x = ar_ref[...] * x + as_ref[...] * swap(x) + X_s[rs, :]
            X_s[rs, :] = x
        y = jnp.concatenate(
            [jnp.dot(X_s[rows_of(sg), 2 * half * j:2 * half * (j + 1)].astype(BF16), cw_ref[j],
                     preferred_element_type=F32) for j in range(nj)], axis=1)
        if backward:
            u = jnp.concatenate([U_s[j, rows_of(sg), :] for j in range(nj)], axis=1)
            z = yf_ref[0, rows_of(sg), :] + y + d_ref[...] * u
            gate = _sigmoid(jnp.dot(z.astype(BF16), wg_ref[...], preferred_element_type=F32) + bgl_ref[...])
            hb = jax.nn.gelu(z) * gate
            for j in range(nj):
                H_s[j, rows_of(sg), :] = hb[:, j * LANES:(j + 1) * LANES]
        else:
            out_ref[0, rows_of(sg), :] = y
    st_s[...] = x
    xl_ref[0] = x
    if backward:
        for r in range(nreq):
            for j in range(nj):
                out_ref[r, :, j * LANES:(j + 1) * LANES] = \
                    H_s[j, pl.ds(r, Lc, stride=nreq), :].astype(out_ref.dtype)


def _s5_pass(backward, u, yf, x0, ar, asg, bw, cw, glu, Lc):
    B, T, su = u.shape
    G8 = B // SUBLANES
    rows = Lc * SUBLANES
    nchunk = T // Lc
    SL = x0.shape[-1]
    nj = S5_LANE_GROUPS
    assert su == nj * LANES, su
    cidx = (lambda c: nchunk - 1 - c) if backward else (lambda c: c)
    reqspec = pl.BlockSpec((SUBLANES, Lc, su), lambda g, c: (g, cidx(c), 0))
    rowspec = pl.BlockSpec((1, rows, su), lambda g, c: (g, cidx(c), 0))
    const = lambda shape: pl.BlockSpec(shape, lambda g, c: (0,) * len(shape))
    in_specs = [reqspec]
    args = [u]
    if backward:
        in_specs.append(rowspec)
        args.append(yf)
    in_specs += [pl.BlockSpec((1, SUBLANES, SL), lambda g, c: (g, 0, 0)),
                 const((SUBLANES, SL)), const((SUBLANES, SL)), const(bw.shape), const(cw.shape)]
    args += [x0, ar, asg, bw, cw]
    scratch = [pltpu.VMEM((rows, SL), F32), pltpu.VMEM((SUBLANES, SL), F32),
               pltpu.VMEM((nj, rows, LANES), F32)]
    if backward:
        d, wg, bgl = glu
        in_specs += [const((1, su)), const(wg.shape), const((1, su))]
        args += [d, wg, bgl]
        scratch.append(pltpu.VMEM((nj, rows, LANES), F32))
        out0 = (jax.ShapeDtypeStruct((B, T, su), BF16), reqspec)
    else:
        out0 = (jax.ShapeDtypeStruct((G8, T * SUBLANES, su), F32), rowspec)
    return pl.pallas_call(
        functools.partial(_s5_kernel, backward, Lc),
        grid=(G8, nchunk),
        in_specs=in_specs,
        out_specs=[out0[1], pl.BlockSpec((1, SUBLANES, SL), lambda g, c: (g, 0, 0))],
        out_shape=[out0[0], jax.ShapeDtypeStruct((G8, SUBLANES, SL), F32)],
        scratch_shapes=scratch,
        compiler_params=_cparams(("arbitrary", "arbitrary")),
        name="s5_bwd" if backward else "s5_fwd",
    )(*args)


def _s5_discretise(a_re, a_im, log_dt, b_re, b_im, c_re, c_im):
    G, P = a_re.shape
    nj = S5_LANE_GROUPS
    gl = G // nj
    dt = jnp.exp(log_dt)[:, None]
    e = jnp.exp(a_re * dt)
    ar = e * jnp.cos(a_im * dt)
    ai = e * jnp.sin(a_im * dt)
    den = a_re * a_re + a_im * a_im
    fr = ((ar - 1.0) * a_re + ai * a_im) / den
    fi = (ai * a_re - (ar - 1.0) * a_im) / den
    bbr = fr[..., None] * b_re - fi[..., None] * b_im
    bbi = fr[..., None] * b_im + fi[..., None] * b_re

    def lanes(re, im):
        return jnp.stack([re.reshape(nj, gl * P), im.reshape(nj, gl * P)], axis=1).reshape(1, -1)

    a_row = lanes(ar, ar)
    as_row = lanes(-ai, ai)
    eye = jnp.eye(gl, dtype=F32)
    C = b_re.shape[-1]

    def bmat(bb):
        x = bb.reshape(nj, gl, P, C)
        return jnp.einsum('jgpc,gh->jgchp', x, eye).reshape(nj, gl * C, gl * P)

    bw = jnp.concatenate([bmat(bbr), bmat(bbi)], axis=2).astype(BF16)

    def cmat(cc):
        x = cc.reshape(nj, gl, C, P)
        return jnp.einsum('jgcp,gh->jgphc', x, eye).reshape(nj, gl * P, gl * C)

    cw = jnp.concatenate([cmat(c_re), cmat(-c_im)], axis=1).astype(BF16)
    return a_row, as_row, bw, cw


def _s5_state_to_lanes(s_re, s_im):
    B, G, P = s_re.shape
    nj = S5_LANE_GROUPS
    x = jnp.stack([s_re.reshape(B, nj, (G // nj) * P), s_im.reshape(B, nj, (G // nj) * P)], axis=2)
    return x.reshape(B // SUBLANES, SUBLANES, 2 * G * P)


def _s5_lanes_to_state(x, G, P):
    G8 = x.shape[0]
    nj = S5_LANE_GROUPS
    x = x.reshape(G8 * SUBLANES, nj, 2, G // nj, P)
    return x[:, :, 0].reshape(-1, G, P), x[:, :, 1].reshape(-1, G, P)


def _merge_kernel(has_pos, *refs):
    if has_pos:
        x_ref, pos_ref = refs[0], refs[1]
        refs = refs[2:]
    else:
        x_ref, pos_ref = refs[0], None
        refs = refs[1:]
    (ha_ref, hb_ref, sa_ref, sb_ref, mod_ref, nw_ref, wpa_ref, wpb_ref, wout_ref, wrh_ref, wrm_ref,
     br_ref, x1_ref, hp_ref, aff_ref) = refs
    x = x_ref[0]
    if has_pos:
        x = x + pos_ref[...]
    merged = sa_ref[0] * jnp.dot(ha_ref[0], wpa_ref[...], preferred_element_type=F32) \
        + sb_ref[0] * jnp.dot(hb_ref[0], wpb_ref[...], preferred_element_type=F32)
    out = jnp.dot(merged.astype(BF16), wout_ref[...], preferred_element_type=F32)
    x1 = x + mod_ref[0, 2:3, :] * out
    h2 = _rms_mod(x1, nw_ref[...], mod_ref[0, 3:4, :], mod_ref[0, 4:5, :])
    tm = h2.shape[0]
    for sl in range(h2.shape[1] // LANES):
        hp_ref[0, pl.ds(sl, tm, stride=SUBLANES), :] = h2[:, sl * LANES:(sl + 1) * LANES]
        x1_ref[0, pl.ds(sl, tm, stride=SUBLANES), :] = x1[:, sl * LANES:(sl + 1) * LANES]
    h2h, h2m, _ = _split3(h2)
    logits = jnp.dot(h2h, wrh_ref[...], preferred_element_type=F32) \
        + jnp.dot(h2m, wrh_ref[...], preferred_element_type=F32) \
        + jnp.dot(h2h, wrm_ref[...], preferred_element_type=F32)
    lt = logits.T[0:N_EXPERTS, :] + br_ref[...]
    mx = jnp.max(lt, axis=0, keepdims=True)
    ex = jnp.exp(lt - mx)
    aff_ref[0] = ex / jnp.sum(ex, axis=0, keepdims=True)


def _merge(x, pos, ha, hb, sa, sb, mod, mod_row_fn, nw, wts, tm):
    B, T, D = x.shape
    has_pos = pos is not None
    wpa, wpb, wout, wrh, wrm, br = wts
    assert D == SUBLANES * LANES, D
    const = lambda shape: pl.BlockSpec(shape, lambda b, i: (0,) * len(shape),
                                       pipeline_mode=pl.Buffered(1))
    tok = lambda width: pl.BlockSpec((1, tm, width), lambda b, i: (b, i, 0))
    in_specs = [tok(D)]
    args = [x]
    if has_pos:
        in_specs.append(pl.BlockSpec((tm, D), lambda b, i: (i, 0)))
        args.append(pos)
    in_specs += [tok(ha.shape[-1]), tok(hb.shape[-1]), tok(D), tok(D),
                 pl.BlockSpec((1, N_MOD, D), lambda b, i: (mod_row_fn(b), 0, 0)),
                 const((1, D)), const(wpa.shape), const(wpb.shape), const(wout.shape),
                 const(wrh.shape), const(wrm.shape), const(br.shape)]
    args += [ha, hb, sa, sb, mod, nw, wpa, wpb, wout, wrh, wrm, br]
    return pl.pallas_call(
        functools.partial(_merge_kernel, has_pos),
        grid=(B, T // tm),
        in_specs=in_specs,
        out_specs=[pl.BlockSpec((1, tm * SUBLANES, LANES), lambda b, i: (b, i, 0)),
                   pl.BlockSpec((1, tm * SUBLANES, LANES), lambda b, i: (b, i, 0)),
                   pl.BlockSpec((1, N_EXPERTS, tm), lambda b, i: (b, 0, i))],
        out_shape=[jax.ShapeDtypeStruct((B, T * SUBLANES, LANES), F32),
                   jax.ShapeDtypeStruct((B, T * SUBLANES, LANES), F32),
                   jax.ShapeDtypeStruct((B, N_EXPERTS, T), F32)],
        compiler_params=_cparams(("arbitrary", "arbitrary")),
        name="merge",
    )(*args)


def _route_kernel(cap, aff_ref, idx_ref, val_ref, jb_s, jl_s):
    aff = aff_ref[0]
    E, T = aff.shape
    thr_bits = jnp.zeros((E, 1), jnp.int32)
    for bit in range(30, -1, -1):
        cand = thr_bits | (1 << bit)
        cnt = jnp.sum((aff >= lax.bitcast_convert_type(cand, F32)).astype(F32), axis=1, keepdims=True)
        thr_bits = jnp.where(cnt >= cap, cand, thr_bits)
    thr = lax.bitcast_convert_type(thr_bits, F32)
    nxt = lax.bitcast_convert_type(thr_bits + 1, F32)
    gt = aff >= nxt
    eq = (aff >= thr) & jnp.logical_not(gt)
    need = cap - jnp.sum(gt.astype(F32), axis=1, keepdims=True)

    r = lax.broadcasted_iota(jnp.int32, (LANES, LANES), 0)
    c = lax.broadcasted_iota(jnp.int32, (LANES, LANES), 1)
    tri = (r < c).astype(BF16)

    def excl_cumsum(mk):
        outs = []
        carry = jnp.zeros((E, 1), F32)
        for kb in range(T // LANES):
            blk = mk[:, kb * LANES:(kb + 1) * LANES]
            outs.append(jnp.dot(blk.astype(BF16), tri, preferred_element_type=F32) + carry)
            carry = carry + jnp.sum(blk.astype(F32), axis=1, keepdims=True)
        return jnp.concatenate(outs, axis=1)

    tie_rank = excl_cumsum(eq)
    mask = gt | (eq & (tie_rank < need))
    slot = jnp.where(mask, excl_cumsum(mask), -1.0).astype(jnp.int32)
    JL = min(cap, LANES)
    njb = cap // JL
    NC = 5
    WR = -(-(njb * NC) // SUBLANES) * SUBLANES
    jb_s[...] = slot >> (JL.bit_length() - 1)
    jl_s[...] = slot & (JL - 1)
    tio = lax.broadcasted_iota(jnp.int32, (1, T), 1)
    t_hi = (tio >> 6).astype(F32)
    t_lo = (tio & 63).astype(F32)
    jl_col = lax.broadcasted_iota(jnp.int32, (JL, 1), 0)
    wrow = lax.broadcasted_iota(jnp.int32, (WR, 1), 0)
    wjb = jnp.full((WR, 1), -2, jnp.int32)
    for jb in range(njb):
        wjb = jnp.where((wrow >= NC * jb) & (wrow < NC * (jb + 1)), jb, wjb)
    wcomp = wrow - NC * wjb
    lane = lax.broadcasted_iota(jnp.int32, (JL, E), 1)
    nt = (((1,), (1,)), ((), ()))

    def per_expert(e, carry):
        a1, a2, a3 = _split3(aff_ref[0, pl.ds(e, 1), :])
        comp = jnp.where(wcomp == 0, t_hi,
                         jnp.where(wcomp == 1, t_lo,
                                   jnp.where(wcomp == 2, a1.astype(F32),
                                             jnp.where(wcomp == 3, a2.astype(F32), a3.astype(F32)))))
        wt = jnp.where(jb_s[pl.ds(e, 1), :] == wjb, comp, 0.0).astype(BF16)
        onehot = jnp.where(jl_s[pl.ds(e, 1), :] == jl_col, 1.0, 0.0).astype(BF16)
        r = lax.dot_general(onehot, wt, nt, preferred_element_type=F32)
        out = []
        for jb in range(njb):
            c0 = jb * NC
            ie = r[:, c0:c0 + 1] * 64.0 + r[:, c0 + 1:c0 + 2]
            ve = r[:, c0 + 2:c0 + 3] + r[:, c0 + 3:c0 + 4] + r[:, c0 + 4:c0 + 5]
            out.append(jnp.where(lane == e, ie, carry[2 * jb]))
            out.append(jnp.where(lane == e, ve, carry[2 * jb + 1]))
        return tuple(out)

    res = lax.fori_loop(0, E, per_expert, tuple(jnp.zeros((JL, E), F32) for _ in range(2 * njb)),
                        unroll=2)
    for jb in range(njb):
        idx_ref[0, jb * JL:(jb + 1) * JL, :] = res[2 * jb].astype(jnp.int32)
        val_ref[0, jb * JL:(jb + 1) * JL, :] = res[2 * jb + 1]


def _route(aff, cap):
    B, E0, T = aff.shape
    R = max(1, min(B, ROUTE_ROW_ELEMS // (E0 * T)))
    E = R * E0
    idx, val = pl.pallas_call(
        functools.partial(_route_kernel, cap),
        grid=(B // R,),
        in_specs=[pl.BlockSpec((1, E, T), lambda b: (b, 0, 0))],
        out_specs=[pl.BlockSpec((1, cap, E), lambda b: (b, 0, 0)),
                   pl.BlockSpec((1, cap, E), lambda b: (b, 0, 0))],
        out_shape=[jax.ShapeDtypeStruct((B // R, cap, E), jnp.int32),
                   jax.ShapeDtypeStruct((B // R, cap, E), F32)],
        scratch_shapes=[pltpu.VMEM((E, T), jnp.int32), pltpu.VMEM((E, T), jnp.int32)],
        compiler_params=_cparams(("arbitrary",)),
        name="route",
    )(aff.reshape(B // R, E, T))
    unbatch = lambda a: a.reshape(B // R, cap, R, E0).transpose(0, 2, 1, 3).reshape(B, cap, E0)
    return unbatch(idx), unbatch(val)


def _moe_kernel(idx_ref, idxn_ref, tok_ref, val_ref, g2_ref, nw_ref, w1_ref, w3_ref, w2_ref, x1_ref, out_ref,
                xe_s, xb_s, ye_s, yt_s, acc_s, yb_s, sem_in, sem_out):
    b, e, f = pl.program_id(0), pl.program_id(1), pl.program_id(2)
    ne = pl.num_programs(1)
    cap = ye_s.shape[0]
    S = SUBLANES
    GR = 8

    def tile(ref, r):
        return ref.at[pl.ds(pl.multiple_of(r * S, S), S), :]

    def gather_group(iref, g):
        rows = [iref[0, 0, g * GR + i] for i in range(GR)]
        vals = [tile(tok_ref.at[0], rows[i])[...] for i in range(GR)]
        for i in range(GR):
            tile(xe_s, g * GR + i)[...] = vals[i]

    def scatter_group(iref, g):
        rows = [iref[0, 0, g * GR + i] for i in range(GR)]
        vals = [tile(acc_s, rows[i])[...] + tile(yt_s, g * GR + i)[...] for i in range(GR)]
        for i in range(GR):
            tile(acc_s, rows[i])[...] = vals[i]

    def ffn_half():
        xe = xb_s[...]
        h1 = jnp.dot(xe, w1_ref[0], preferred_element_type=F32)
        h3 = jnp.dot(xe, w3_ref[0], preferred_element_type=F32)
        hid = (h1 * _sigmoid(h1) * h3).astype(BF16)
        return jnp.dot(hid, w2_ref[0], preferred_element_type=F32)

    @pl.when((e == 0) & (f == 0))
    def _():
        cp_in = pltpu.make_async_copy(x1_ref.at[b], acc_s, sem_in)
        cp_in.start()
        yt_s[...] = jnp.zeros_like(yt_s)
        lax.fori_loop(0, cap // GR, lambda g, c: (gather_group(idx_ref, g), c)[1], 0)
        cp_in.wait()

    @pl.when(f == 0)
    def _():
        xb_s[...] = jnp.concatenate([xe_s[pl.ds(sl, cap, stride=S), :] for sl in range(S)],
                                    axis=1).astype(BF16)
        ye_s[...] = ffn_half()
        for g in range(cap // GR):
            scatter_group(idxn_ref, g)

    @pl.when(f == 1)
    def _():
        yw = (ye_s[...] + ffn_half()) * val_ref[0, 0] * g2_ref[0]
        for sl in range(S):
            yt_s[pl.ds(sl, cap, stride=S), :] = yw[:, sl * LANES:(sl + 1) * LANES]
        for g in range(cap // GR):
            gather_group(idxn_ref, g)

    @pl.when((e == ne - 1) & (f == 1))
    def _():
        lax.fori_loop(0, cap // GR, lambda g, c: (scatter_group(idx_ref, g), c)[1], 0)
        ch = yb_s.shape[1]
        nch = acc_s.shape[0] // (ch * S)
        copies = []
        for c in range(nch):
            slot = c % 2
            if c >= 2:
                copies[c - 2].wait()
            x = jnp.concatenate([acc_s[pl.ds(c * ch * S + sl, ch, stride=S), :] for sl in range(S)], axis=1)
            ms = jnp.mean(x * x, axis=-1, keepdims=True)
            yb_s[slot] = x * lax.rsqrt(ms + EPS) * nw_ref[...]
            cp = pltpu.make_async_copy(yb_s.at[slot], out_ref.at[b, pl.ds(c * ch, ch), :], sem_out.at[slot])
            cp.start()
            copies.append(cp)
        for c in range(max(nch - 2, 0), nch):
            copies[c].wait()


def _moe(idx, tok, x1, val, gate2, nw, w1, w3, w2):
    B, NR, _ = tok.shape
    E, D, FF = w1.shape
    cap = idx.shape[-1]
    nf = EXPERT_FF_SPLIT
    assert nf == 2, nf
    fh = FF // nf
    NT = NR // SUBLANES
    ch = min(MOE_OUT_CHUNK, NT)
    return pl.pallas_call(
        _moe_kernel,
        grid=(B, E, nf),
        in_specs=[pl.BlockSpec((1, 1, cap), lambda b, e, f: (b * E + e, 0, 0), memory_space=pltpu.SMEM),
                  pl.BlockSpec((1, 1, cap), lambda b, e, f: (b * E + jnp.clip(e - 1 + 2 * f, 0, E - 1), 0, 0),
                               memory_space=pltpu.SMEM),
                  pl.BlockSpec((1, NR, LANES), lambda b, e, f: (b, 0, 0), pipeline_mode=pl.Buffered(1)),
                  pl.BlockSpec((1, 1, cap, 1), lambda b, e, f: (b, e, 0, 0)),
                  pl.BlockSpec((1, 1, D), lambda b, e, f: (b, 0, 0)),
                  pl.BlockSpec((1, D), lambda b, e, f: (0, 0)),
                  pl.BlockSpec((1, D, fh), lambda b, e, f: (e, 0, f)),
                  pl.BlockSpec((1, D, fh), lambda b, e, f: (e, 0, f)),
                  pl.BlockSpec((1, fh, D), lambda b, e, f: (e, f, 0)),
                  pl.BlockSpec(memory_space=pl.ANY)],
        out_specs=pl.BlockSpec(memory_space=pl.ANY),
        out_shape=jax.ShapeDtypeStruct((B, NT, D), F32),
        scratch_shapes=[pltpu.VMEM((cap * SUBLANES, LANES), F32), pltpu.VMEM((cap, D), BF16),
                        pltpu.VMEM((cap, D), F32), pltpu.VMEM((cap * SUBLANES, LANES), F32),
                        pltpu.VMEM((NR, LANES), F32), pltpu.VMEM((2, ch, D), F32),
                        pltpu.SemaphoreType.DMA(()), pltpu.SemaphoreType.DMA((2,))],
        compiler_params=_cparams(("arbitrary", "arbitrary", "arbitrary")),
        name="moe",
    )(idx, idx, tok, val, gate2, nw, w1, w3, w2, x1)


def _pos2d(T, D):
    rows = T // GRID_W
    quarter = D // 4
    freqs = 1.0 / (10000.0 ** (jnp.arange(quarter, dtype=F32) / quarter))
    er = jnp.arange(rows, dtype=F32)[:, None] * freqs
    ec = jnp.arange(GRID_W, dtype=F32)[:, None] * freqs
    rep = lambda a: jnp.repeat(a, GRID_W, axis=0)
    til = lambda a: jnp.tile(a, (rows, 1))
    return jnp.concatenate([rep(jnp.sin(er)), rep(jnp.cos(er)), til(jnp.sin(ec)), til(jnp.cos(ec))], axis=-1)


def _stream(x, pos, mod, mod_row_fn, P, mstate, s5_x0, tm):
    B, T, D = x.shape
    L = min(MLSTM_CHUNK, T)
    q, k, v, o, gc, gr, gtot, u, sa, sb = _inproj(x, pos, mod, mod_row_fn, P['norm_mix'],
                                                   P['w_in_parts'], tm, L)
    res = _mlstm(q, k, v, o, gc, gr, gtot, P['head_norm'], mstate, L)
    ha, mout = res[0], res[1:]
    Lc = min(S5_CHUNK, T)
    yf, xf = _s5_pass(False, u, None, s5_x0[0], *P['s5_dir'][0], None, Lc)
    hb, xb = _s5_pass(True, u, yf, s5_x0[1], *P['s5_dir'][1], P['glu'], Lc)
    x1, hp, aff = _merge(x, pos, ha, hb, sa, sb, mod, mod_row_fn, P['norm_ffn'], P['merge_w'], tm)
    cap = CAPACITY_FACTOR * T // N_EXPERTS
    idx, val = _route(aff, cap)
    E = N_EXPERTS
    gb = max(1, min(B, MOE_TABLE_ROWS // T))
    if any(mod_row_fn(r) != mod_row_fn(r - r % gb) for r in range(B)):
        gb = 1
    nb = B // gb
    offs = (jnp.arange(B, dtype=jnp.int32) % gb) * T
    idx_f = (idx + offs[:, None, None]).reshape(nb, gb, cap, E).transpose(0, 3, 1, 2).reshape(nb * E, 1, gb * cap)
    val_f = val.reshape(nb, gb, cap, E).transpose(0, 3, 1, 2).reshape(nb, E, gb * cap, 1)
    gate2 = jnp.stack([mod[mod_row_fn(t * gb), N_MOD - 1] for t in range(nb)])[:, None, :]
    tiles = lambda a: a.reshape(nb, gb * T * SUBLANES, LANES)
    y = _moe(idx_f, tiles(hp), tiles(x1), val_f, gate2, P['norm_final'], *P['experts']).reshape(B, T, D)
    return y, mout, (xf, xb)


def kernel(x_prompt, x_sample, state_mlstm_C, state_mlstm_n, state_mlstm_m, state_s5_re, state_s5_im,
           c, c_ctx, w_ada, b_ada, norm_mix, norm_ffn, w_in, b_gate, head_norm,
           s5_a_re, s5_a_im, s5_log_dt, s5_b_re, s5_b_im, s5_c_re, s5_c_im, s5_d, w_glu, b_glu,
           w_pA, w_pB, w_out, w_router, b_router, w_e1, w_e3, w_e2, norm_final):
    Bp, Tp, D = x_prompt.shape
    Bs, Ts, _ = x_sample.shape
    depth = w_ada.shape[0]
    H, dh = MLSTM_HEADS, MLSTM_HEAD_DIM
    W = H * dh
    G, Pn = s5_a_re.shape[2], s5_a_re.shape[3]
    SW = G * S5_GROUP

    assert depth == 1, depth
    l = 0
    pos = _pos2d(Ts, D)
    nrow = Bs + 1
    rows = -(-nrow // SUBLANES) * SUBLANES
    cpad = jnp.zeros((rows, D), F32).at[0:Bs].set(c).at[Bs].set(c_ctx)
    mod = _adaln(cpad, w_ada[l], b_ada[l]).reshape(rows, N_MOD, D)

    wi = w_in[l]
    gcols = wi[:, 4 * W:4 * W + 4 * H].reshape(D, 4, H).transpose(0, 2, 1).reshape(D, 4 * H)
    wug = jnp.concatenate([wi[:, 4 * W + 4 * H:4 * W + 4 * H + SW], gcols,
                           jnp.zeros((D, LANES - 4 * H), F32)], axis=1).astype(BF16)
    bg = jnp.concatenate([b_gate[l].T.reshape(1, 4 * H), jnp.zeros((1, LANES - 4 * H), F32)], axis=1)
    w_in_parts = (wi[:, 0:3 * W].astype(BF16), wi[:, 3 * W:4 * W].astype(BF16), wug,
                  wi[:, 4 * W + 4 * H + SW:].astype(BF16), bg)
    s5_dir = [_s5_discretise(s5_a_re[l, d], s5_a_im[l, d], s5_log_dt[l, d], s5_b_re[l], s5_b_im[l],
                             s5_c_re[l, d], s5_c_im[l, d]) for d in range(2)]
    s5_dir = [(jnp.broadcast_to(a, (SUBLANES, a.shape[1])), jnp.broadcast_to(s, (SUBLANES, s.shape[1])), bw, cw)
              for (a, s, bw, cw) in s5_dir]
    wr = jnp.concatenate([w_router[l], jnp.zeros((D, LANES - N_EXPERTS), F32)], axis=1)
    wr_hi = wr.astype(BF16)
    wr_mid = (wr - wr_hi.astype(F32)).astype(BF16)
    P = {
        'norm_mix': norm_mix[l].reshape(1, D), 'norm_ffn': norm_ffn[l].reshape(1, D),
        'norm_final': norm_final.reshape(1, D),
        'w_in_parts': w_in_parts, 'head_norm': head_norm[l], 's5_dir': s5_dir,
        'glu': (s5_d[l].reshape(1, SW), w_glu[l].astype(BF16), b_glu[l].reshape(1, SW)),
        'merge_w': (w_pA[l].astype(BF16), w_pB[l].astype(BF16), w_out[l].astype(BF16), wr_hi, wr_mid,
                    b_router[l].reshape(N_EXPERTS, 1)),
        'experts': (w_e1[l].astype(BF16), w_e3[l].astype(BF16), w_e2[l].astype(BF16)),
    }
    zero_x0 = jnp.zeros((Bp // SUBLANES, SUBLANES, 2 * G * Pn), F32)
    yp, mout, (xf, xb) = _stream(x_prompt, None, mod, lambda b: Bs, P, None, (zero_x0, zero_x0),
                                 min(MLSTM_CHUNK, Tp))
    lat_x0 = [_s5_state_to_lanes(state_s5_re[:, l, d], state_s5_im[:, l, d]) for d in range(2)]
    ys, _, _ = _stream(x_sample, pos, mod, lambda b: b, P,
                       (state_mlstm_C[:, l], state_mlstm_n[:, l], state_mlstm_m[:, l]), lat_x0,
                       min(MLSTM_CHUNK, Ts))
    Cn, nn, mn = mout
    fre, fim = _s5_lanes_to_state(xf, G, Pn)
    bre, bim = _s5_lanes_to_state(xb, G, Pn)
    outs = (Cn, nn.reshape(Bp, 2, H, dh), mn.reshape(Bp, 2, H),
            jnp.stack([fre, bre], axis=1), jnp.stack([fim, bim], axis=1))
    out_C, out_n, out_m, out_re, out_im = (o[:, None] for o in outs)
    return (yp, ys, out_C, out_n, out_m, out_re, out_im)
```

```python
import functools

import jax
import jax.numpy as jnp
from jax import lax
from jax.experimental import pallas as pl
from jax.experimental.pallas import tpu as pltpu

F32 = jnp.float32
BF16 = jnp.bfloat16
EPS = 1e-6

MLSTM_HEADS = 4
MLSTM_HEAD_DIM = 256
N_EXPERTS = 16
CAPACITY_FACTOR = 2
S5_GROUP = 16
S5_STATE = 64
GRID_W = 64
N_MOD = 6

LANES = 128
SUBLANES = 8
VMEM_LIMIT_BYTES = 60000 * 1024

MLSTM_CHUNK = 256
S5_CHUNK = 64
S5_LANE_GROUPS = 4
S5_SEGMENTS = 2
EXPERT_FF_SPLIT = 2
ROUTE_ROW_ELEMS = 16 * 4096
MOE_OUT_CHUNK = 256
MOE_TABLE_ROWS = 4096


def _cparams(sem):
    return pltpu.CompilerParams(dimension_semantics=sem, vmem_limit_bytes=VMEM_LIMIT_BYTES)


def _sigmoid(x):
    return jax.nn.sigmoid(x)


def _log_sigmoid(x):
    return jnp.minimum(x, 0.0) - jnp.log1p(jnp.exp(-jnp.abs(x)))


def _adaln_kernel(c_ref, w_ref, b_ref, o_ref):
    c = c_ref[...]
    s = c * _sigmoid(c)
    o_ref[...] = jnp.dot(s, w_ref[...], precision=lax.Precision.HIGHEST,
                         preferred_element_type=F32) + b_ref[...]


def _adaln(cpad, w, b):
    rows, d = cpad.shape
    n = w.shape[1]
    tn = 1536
    return pl.pallas_call(
        _adaln_kernel,
        grid=(n // tn,),
        in_specs=[pl.BlockSpec((rows, d), lambda j: (0, 0)),
                  pl.BlockSpec((d, tn), lambda j: (0, j)),
                  pl.BlockSpec((1, tn), lambda j: (0, j))],
        out_specs=pl.BlockSpec((rows, tn), lambda j: (0, j)),
        out_shape=jax.ShapeDtypeStruct((rows, n), F32),
        compiler_params=_cparams(("arbitrary",)),
        name="adaln",
    )(cpad, w, b.reshape(1, n))


def _rms_mod(x, nw, shift, scale):
    ms = jnp.mean(x * x, axis=-1, keepdims=True)
    y = x * lax.rsqrt(ms + EPS) * nw
    return y * (1.0 + scale) + shift


def _split3(x):
    hi = x.astype(BF16)
    r = x - hi.astype(F32)
    mid = r.astype(BF16)
    lo = (r - mid.astype(F32)).astype(BF16)
    return hi, mid, lo


def _inproj_kernel(has_pos, *refs):
    if has_pos:
        x_ref, pos_ref = refs[0], refs[1]
        refs = refs[2:]
    else:
        x_ref, pos_ref = refs[0], None
        refs = refs[1:]
    (mod_ref, nw_ref, wqkv_ref, wo_ref, wug_ref, wab_ref, bg_ref,
     q_ref, k_ref, v_ref, o_ref, gc_ref, gr_ref, gtot_ref, u_ref, sa_ref, sb_ref) = refs
    x = x_ref[0]
    if has_pos:
        x = x + pos_ref[...]
    h = _rms_mod(x, nw_ref[...], mod_ref[0, 0:1, :], mod_ref[0, 1:2, :])
    hb = h.astype(BF16)
    w = MLSTM_HEADS * MLSTM_HEAD_DIM
    qkv = jnp.dot(hb, wqkv_ref[...], preferred_element_type=F32)
    q_ref[0] = (qkv[:, 0:w] * (MLSTM_HEAD_DIM ** -0.5)).astype(BF16)
    k_ref[0] = qkv[:, w:2 * w].astype(BF16)
    vt = qkv[:, 2 * w:3 * w].T.astype(BF16)
    for hh in range(MLSTM_HEADS):
        v_ref[0, hh, 0] = vt[hh * MLSTM_HEAD_DIM:(hh + 1) * MLSTM_HEAD_DIM, :]
    o_ref[0] = jnp.dot(hb, wo_ref[...], preferred_element_type=F32)
    ug = jnp.dot(hb, wug_ref[...], preferred_element_type=F32)
    su = u_ref.shape[-1]
    u_ref[0] = ug[:, 0:su]
    g = ug[:, su:su + LANES] + bg_ref[...]
    col = lax.broadcasted_iota(jnp.int32, g.shape, 1)
    g = jnp.where((col % 2) == 1, _log_sigmoid(g), g)
    tm = g.shape[0]
    tril = (lax.broadcasted_iota(jnp.int32, (tm, tm), 1)
            <= lax.broadcasted_iota(jnp.int32, (tm, tm), 0)).astype(BF16)
    cs = sum(jnp.dot(tril, part, preferred_element_type=F32) for part in _split3(g))
    tot = cs[tm - 1:tm, :]
    cum = jnp.where((col % 4) < 2, cs, tot - cs + g)
    cum_l = pltpu.roll(cum, LANES - 1, axis=1)
    tot_l = pltpu.roll(jnp.broadcast_to(tot, g.shape), LANES - 1, axis=1)
    r = g - cum_l
    cumt = cum.T
    ggt = (tot_l - cum_l + g).T
    for hh in range(MLSTM_HEADS):
        gc_ref[0, hh] = r[:, 4 * hh:4 * hh + 4]
        gr_ref[0, hh, 0, 0:4, :] = cumt[4 * hh:4 * hh + 4, :]
        gr_ref[0, hh, 0, 4:8, :] = ggt[4 * hh:4 * hh + 4, :]
        gtot_ref[0, hh, 0] = tot[:, 4 * hh:4 * hh + 4]
    ab = jnp.dot(hb, wab_ref[...], preferred_element_type=F32)
    d = sa_ref.shape[-1]
    sa_ref[0] = _sigmoid(ab[:, 0:d])
    sb_ref[0] = _sigmoid(ab[:, d:2 * d])


def _inproj(x, pos, mod, mod_row_fn, nw, wts, tm, L):
    B, T, D = x.shape
    has_pos = pos is not None
    wqkv, wo, wug, wab, bg = wts
    su = wug.shape[1] - LANES
    assert tm == L, (tm, L)
    const = lambda shape: pl.BlockSpec(shape, lambda b, i: (0,) * len(shape),
                                       pipeline_mode=pl.Buffered(1))
    in_specs = [pl.BlockSpec((1, tm, D), lambda b, i: (b, i, 0))]
    args = [x]
    if has_pos:
        in_specs.append(pl.BlockSpec((tm, D), lambda b, i: (i, 0)))
        args.append(pos)
    in_specs += [pl.BlockSpec((1, N_MOD, D), lambda b, i: (mod_row_fn(b), 0, 0)),
                 const((1, D)), const(wqkv.shape), const(wo.shape), const(wug.shape),
                 const(wab.shape), const((1, LANES))]
    args += [mod, nw, wqkv, wo, wug, wab, bg]
    W = MLSTM_HEADS * MLSTM_HEAD_DIM
    tok = lambda width, dt: (jax.ShapeDtypeStruct((B, T, width), dt),
                             pl.BlockSpec((1, tm, width), lambda b, i: (b, i, 0)))
    outs = [tok(W, BF16), tok(W, BF16),
            (jax.ShapeDtypeStruct((B, MLSTM_HEADS, T // L, MLSTM_HEAD_DIM, L), BF16),
             pl.BlockSpec((1, MLSTM_HEADS, 1, MLSTM_HEAD_DIM, L), lambda b, i: (b, 0, i, 0, 0))),
            tok(W, F32),
            (jax.ShapeDtypeStruct((B, MLSTM_HEADS, T, 4), F32),
             pl.BlockSpec((1, MLSTM_HEADS, tm, 4), lambda b, i: (b, 0, i, 0))),
            (jax.ShapeDtypeStruct((B, MLSTM_HEADS, T // L, 8, L), F32),
             pl.BlockSpec((1, MLSTM_HEADS, 1, 8, L), lambda b, i: (b, 0, i, 0, 0))),
            (jax.ShapeDtypeStruct((B, MLSTM_HEADS, T // L, 1, 4), F32),
             pl.BlockSpec((1, MLSTM_HEADS, 1, 1, 4), lambda b, i: (b, 0, i, 0, 0))),
            tok(su, F32), tok(D, F32), tok(D, F32)]
    return pl.pallas_call(
        functools.partial(_inproj_kernel, has_pos),
        grid=(B, T // tm),
        in_specs=in_specs,
        out_specs=[o[1] for o in outs],
        out_shape=[o[0] for o in outs],
        compiler_params=_cparams(("arbitrary", "arbitrary")),
        name="inproj",
    )(*args)


MLSTM_AUG_ROWS = 16


def _mlstm_chunk(q, k, vt_aug, r_col, b_row, gg_row, total, C_ref, m_ref, d, valid_t):
    dh = q.shape[1]
    Ca = C_ref[d]
    m = m_ref[d]
    nt = (((1,), (1,)), ((), ()))
    rmask = jnp.where(valid_t, r_col, -jnp.inf)
    mx = jnp.maximum(m, jnp.max(rmask, axis=0, keepdims=True))
    p = jnp.exp(rmask - mx)
    w_inter = jnp.exp(m - mx)
    st = lax.dot_general(k, q, nt, preferred_element_type=F32) * p
    na = w_inter * lax.dot_general(Ca.astype(BF16), q, nt, preferred_element_type=F32) \
        + jnp.dot(vt_aug, st.astype(BF16), preferred_element_type=F32)
    rden = 1.0 / jnp.maximum(jnp.abs(na[dh:dh + 1, :]), jnp.exp(-(b_row + mx)))
    ht = na[0:dh, :] * rden
    m_new = jnp.maximum(total + m, jnp.max(gg_row, axis=1, keepdims=True))
    decay = jnp.exp(total + m - m_new)
    vw = (vt_aug.astype(F32) * jnp.exp(gg_row - m_new)).astype(BF16)
    C_ref[d] = decay * Ca + jnp.dot(vw, k, preferred_element_type=F32)
    m_ref[d] = m_new
    return ht


def _mlstm_kernel(zero_init, L, *refs):
    T, dh = refs[0].shape[1], refs[0].shape[2]
    if zero_init:
        (q_ref, k_ref, v_ref, o_ref, gc_ref, gr_ref, gtot_ref, hn_ref,
         out_ref, Co_ref, no_ref, mo_ref, C_s, m_s, hf_s, hb_s) = refs
        C_s[...] = jnp.zeros_like(C_s)
        m_s[...] = jnp.zeros_like(m_s)
    else:
        (q_ref, k_ref, v_ref, o_ref, gc_ref, gr_ref, gtot_ref, hn_ref, C0_ref, n0_ref, m0_ref,
         out_ref, C_s, m_s, hf_s, hb_s) = refs
        for d in range(2):
            C_s[d, 0:dh, :] = C0_ref[0, d, 0].T
            C_s[d, dh:dh + MLSTM_AUG_ROWS, :] = jnp.broadcast_to(n0_ref[0, d, 0], (MLSTM_AUG_ROWS, dh))
        m_s[...] = m0_ref[0, :, 0]
    nc = T // L
    row = lax.broadcasted_iota(jnp.int32, (L, L), 0)
    col = lax.broadcasted_iota(jnp.int32, (L, L), 1)
    masks_t = (row <= col, row >= col)
    ones = jnp.ones((MLSTM_AUG_ROWS, L), BF16)

    def body(c, carry):
        for d in range(2):
            cc = c if d == 0 else nc - 1 - c
            r0 = pl.multiple_of(cc * L, L)
            q = q_ref[0, pl.ds(r0, L), :]
            k = k_ref[0, pl.ds(r0, L), :]
            vt_aug = jnp.concatenate([v_ref[0, 0, cc], ones], axis=0)
            gc = gc_ref[0, 0, pl.ds(r0, L), :]
            gr = gr_ref[0, 0, cc]
            tot = gtot_ref[0, 0, cc]
            ht = _mlstm_chunk(q, k, vt_aug, gc[:, 2 * d:2 * d + 1], gr[2 * d + 1:2 * d + 2, :],
                              gr[4 + 2 * d:5 + 2 * d, :], tot[:, 2 * d + 1:2 * d + 2],
                              C_s, m_s, d, masks_t[d])
            (hf_s if d == 0 else hb_s)[cc] = ht
        return carry

    lax.fori_loop(0, nc, body, 0, unroll=4 if nc % 4 == 0 else 1)

    def fin(c, carry):
        r0 = pl.multiple_of(c * L, L)
        hm = (hf_s[c] + hb_s[c]).T * _sigmoid(o_ref[0, pl.ds(r0, L), :])
        hm = hm * lax.rsqrt(jnp.mean(hm * hm, axis=-1, keepdims=True) + EPS) * hn_ref[0]
        out_ref[0, pl.ds(r0, L), :] = hm.astype(out_ref.dtype)
        return carry

    lax.fori_loop(0, nc, fin, 0)
    if zero_init:
        for d in range(2):
            Co_ref[0, d, 0] = C_s[d, 0:dh, :].T
        no_ref[0, :, 0] = C_s[:, dh:dh + 1, :]
        mo_ref[0, :, 0] = m_s[...]


def _mlstm(q, k, v, o, gc, gr, gtot, head_norm, state, L):
    B, T, W = q.shape
    H, dh = MLSTM_HEADS, MLSTM_HEAD_DIM
    nc = T // L
    zero_init = state is None
    bh = lambda width: pl.BlockSpec((1, T, width), lambda b, h: (b, 0, h))
    in_specs = [bh(dh), bh(dh), pl.BlockSpec((1, 1, nc, dh, L), lambda b, h: (b, h, 0, 0, 0)), bh(dh),
                pl.BlockSpec((1, 1, T, 4), lambda b, h: (b, h, 0, 0)),
                pl.BlockSpec((1, 1, nc, 8, L), lambda b, h: (b, h, 0, 0, 0)),
                pl.BlockSpec((1, 1, nc, 1, 4), lambda b, h: (b, h, 0, 0, 0)),
                pl.BlockSpec((1, 1, dh), lambda b, h: (h, 0, 0))]
    args = [q, k, v, o, gc, gr, gtot, head_norm.reshape(H, 1, dh)]
    st_specs = [pl.BlockSpec((1, 2, 1, dh, dh), lambda b, h: (b, 0, h, 0, 0)),
                pl.BlockSpec((1, 2, 1, 1, dh), lambda b, h: (b, 0, h, 0, 0)),
                pl.BlockSpec((1, 2, 1, 1, 1), lambda b, h: (b, 0, h, 0, 0))]
    st_shapes = [jax.ShapeDtypeStruct((B, 2, H, dh, dh), F32),
                 jax.ShapeDtypeStruct((B, 2, H, 1, dh), F32),
                 jax.ShapeDtypeStruct((B, 2, H, 1, 1), F32)]
    out_specs = [bh(dh)]
    out_shape = [jax.ShapeDtypeStruct((B, T, W), BF16)]
    if zero_init:
        out_specs += st_specs
        out_shape += st_shapes
    else:
        C0, n0, m0 = state
        in_specs += st_specs
        args += [C0, n0.reshape(B, 2, H, 1, dh), m0.reshape(B, 2, H, 1, 1)]
    return pl.pallas_call(
        functools.partial(_mlstm_kernel, zero_init, L),
        grid=(B, H),
        in_specs=in_specs,
        out_specs=out_specs,
        out_shape=out_shape,
        scratch_shapes=[pltpu.VMEM((2, dh + MLSTM_AUG_ROWS, dh), F32), pltpu.VMEM((2, 1, 1), F32),
                        pltpu.VMEM((nc, dh, L), F32), pltpu.VMEM((nc, dh, L), F32)],
        compiler_params=_cparams(("arbitrary", "arbitrary")),
        name="mlstm",
    )(*args)


def _s5_kernel(backward, Lc, *refs):
    if backward:
        (u_ref, yf_ref, x0_ref, ar_ref, as_ref, bw_ref, cw_ref, d_ref, wg_ref, bgl_ref,
         out_ref, xl_ref, X_s, st_s, U_s, H_s) = refs
    else:
        (u_ref, x0_ref, ar_ref, as_ref, bw_ref, cw_ref, out_ref, xl_ref, X_s, st_s, U_s) = refs
    ci = pl.program_id(1)
    nj = S5_LANE_GROUPS
    half = X_s.shape[1] // (2 * nj)
    nreq = u_ref.shape[0]

    @pl.when(ci == 0)
    def _():
        st_s[...] = x0_ref[0]

    for r in range(nreq):
        for j in range(nj):
            U_s[j, pl.ds(r, Lc, stride=nreq), :] = u_ref[r, :, j * LANES:(j + 1) * LANES]
    nseg = S5_SEGMENTS
    seg_t = Lc // nseg
    seg_order = range(nseg - 1, -1, -1) if backward else range(nseg)

    def rows_of(sg):
        return slice(sg * seg_t * SUBLANES, (sg + 1) * seg_t * SUBLANES)

    for sg in seg_order:
        for j in range(nj):
            X_s[rows_of(sg), 2 * half * j:2 * half * (j + 1)] = jnp.dot(
                U_s[j, rows_of(sg), :].astype(BF16), bw_ref[j], preferred_element_type=F32)

    def swap(x):
        parts = []
        for j in range(nj):
            parts.append(x[:, 2 * half * j + half:2 * half * (j + 1)])
            parts.append(x[:, 2 * half * j:2 * half * j + half])
        return jnp.concatenate(parts, axis=1)

    x = st_s[...]
    for sg in seg_order:
        ts = range(sg * seg_t, (sg + 1) * seg_t)
        for t in (reversed(ts) if backward else ts):
            rs = slice(t * SUBLANES, (t + 1) * SUBLANES)
            x = ar_ref[...] * x + as_ref[...] * swap(x) + X_s[rs, :]
            X_s[rs, :] = x
        y = jnp.concatenate(
            [jnp.dot(X_s[rows_of(sg), 2 * half * j:2 * half * (j + 1)].astype(BF16), cw_ref[j],
                     preferred_element_type=F32) for j in range(nj)], axis=1)
        if backward:
            u = jnp.concatenate([U_s[j, rows_of(sg), :] for j in range(nj)], axis=1)
            z = yf_ref[0, rows_of(sg), :] + y + d_ref[...] * u
            gate = _sigmoid(jnp.dot(z.astype(BF16), wg_ref[...], preferred_element_type=F32) + bgl_ref[...])
            hb = jax.nn.gelu(z) * gate
            for j in range(nj):
                H_s[j, rows_of(sg), :] = hb[:, j * LANES:(j + 1) * LANES]
        else:
            out_ref[0, rows_of(sg), :] = y
    st_s[...] = x
    xl_ref[0] = x
    if backward:
        for r in range(nreq):
            for j in range(nj):
                out_ref[r, :, j * LANES:(j + 1) * LANES] = \
                    H_s[j, pl.ds(r, Lc, stride=nreq), :].astype(out_ref.dtype)


def _s5_pass(backward, u, yf, x0, ar, asg, bw, cw, glu, Lc):
    B, T, su = u.shape
    G8 = B // SUBLANES
    rows = Lc * SUBLANES
    nchunk = T // Lc
    SL = x0.shape[-1]
    nj = S5_LANE_GROUPS
    assert su == nj * LANES, su
    cidx = (lambda c: nchunk - 1 - c) if backward else (lambda c: c)
    reqspec = pl.BlockSpec((SUBLANES, Lc, su), lambda g, c: (g, cidx(c), 0))
    rowspec = pl.BlockSpec((1, rows, su), lambda g, c: (g, cidx(c), 0))
    const = lambda shape: pl.BlockSpec(shape, lambda g, c: (0,) * len(shape))
    in_specs = [reqspec]
    args = [u]
    if backward:
        in_specs.append(rowspec)
        args.append(yf)
    in_specs += [pl.BlockSpec((1, SUBLANES, SL), lambda g, c: (g, 0, 0)),
                 const((SUBLANES, SL)), const((SUBLANES, SL)), const(bw.shape), const(cw.shape)]
    args += [x0, ar, asg, bw, cw]
    scratch = [pltpu.VMEM((rows, SL), F32), pltpu.VMEM((SUBLANES, SL), F32),
               pltpu.VMEM((nj, rows, LANES), F32)]
    if backward:
        d, wg, bgl = glu
        in_specs += [const((1, su)), const(wg.shape), const((1, su))]
        args += [d, wg, bgl]
        scratch.append(pltpu.VMEM((nj, rows, LANES), F32))
        out0 = (jax.ShapeDtypeStruct((B, T, su), BF16), reqspec)
    else:
        out0 = (jax.ShapeDtypeStruct((G8, T * SUBLANES, su), F32), rowspec)
    return pl.pallas_call(
        functools.partial(_s5_kernel, backward, Lc),
        grid=(G8, nchunk),
        in_specs=in_specs,
        out_specs=[out0[1], pl.BlockSpec((1, SUBLANES, SL), lambda g, c: (g, 0, 0))],
        out_shape=[out0[0], jax.ShapeDtypeStruct((G8, SUBLANES, SL), F32)],
        scratch_shapes=scratch,
        compiler_params=_cparams(("arbitrary", "arbitrary")),
        name="s5_bwd" if backward else "s5_fwd",
    )(*args)


def _s5_discretise(a_re, a_im, log_dt, b_re, b_im, c_re, c_im):
    G, P = a_re.shape
    nj = S5_LANE_GROUPS
    gl = G // nj
    dt = jnp.exp(log_dt)[:, None]
    e = jnp.exp(a_re * dt)
    ar = e * jnp.cos(a_im * dt)
    ai = e * jnp.sin(a_im * dt)
    den = a_re * a_re + a_im * a_im
    fr = ((ar - 1.0) * a_re + ai * a_im) / den
    fi = (ai * a_re - (ar - 1.0) * a_im) / den
    bbr = fr[..., None] * b_re - fi[..., None] * b_im
    bbi = fr[..., None] * b_im + fi[..., None] * b_re

    def lanes(re, im):
        return jnp.stack([re.reshape(nj, gl * P), im.reshape(nj, gl * P)], axis=1).reshape(1, -1)

    a_row = lanes(ar, ar)
    as_row = lanes(-ai, ai)
    eye = jnp.eye(gl, dtype=F32)
    C = b_re.shape[-1]

    def bmat(bb):
        x = bb.reshape(nj, gl, P, C)
        return jnp.einsum('jgpc,gh->jgchp', x, eye).reshape(nj, gl * C, gl * P)

    bw = jnp.concatenate([bmat(bbr), bmat(bbi)], axis=2).astype(BF16)

    def cmat(cc):
        x = cc.reshape(nj, gl, C, P)
        return jnp.einsum('jgcp,gh->jgphc', x, eye).reshape(nj, gl * P, gl * C)

    cw = jnp.concatenate([cmat(c_re), cmat(-c_im)], axis=1).astype(BF16)
    return a_row, as_row, bw, cw


def _s5_state_to_lanes(s_re, s_im):
    B, G, P = s_re.shape
    nj = S5_LANE_GROUPS
    x = jnp.stack([s_re.reshape(B, nj, (G // nj) * P), s_im.reshape(B, nj, (G // nj) * P)], axis=2)
    return x.reshape(B // SUBLANES, SUBLANES, 2 * G * P)


def _s5_lanes_to_state(x, G, P):
    G8 = x.shape[0]
    nj = S5_LANE_GROUPS
    x = x.reshape(G8 * SUBLANES, nj, 2, G // nj, P)
    return x[:, :, 0].reshape(-1, G, P), x[:, :, 1].reshape(-1, G, P)


def _merge_kernel(has_pos, *refs):
    if has_pos:
        x_ref, pos_ref = refs[0], refs[1]
        refs = refs[2:]
    else:
        x_ref, pos_ref = refs[0], None
        refs = refs[1:]
    (ha_ref, hb_ref, sa_ref, sb_ref, mod_ref, nw_ref, wpa_ref, wpb_ref, wout_ref, wrh_ref, wrm_ref,
     br_ref, x1_ref, hp_ref, aff_ref) = refs
    x = x_ref[0]
    if has_pos:
        x = x + pos_ref[...]
    merged = sa_ref[0] * jnp.dot(ha_ref[0], wpa_ref[...], preferred_element_type=F32) \
        + sb_ref[0] * jnp.dot(hb_ref[0], wpb_ref[...], preferred_element_type=F32)
    out = jnp.dot(merged.astype(BF16), wout_ref[...], preferred_element_type=F32)
    x1 = x + mod_ref[0, 2:3, :] * out
    h2 = _rms_mod(x1, nw_ref[...], mod_ref[0, 3:4, :], mod_ref[0, 4:5, :])
    tm = h2.shape[0]
    for sl in range(h2.shape[1] // LANES):
        hp_ref[0, pl.ds(sl, tm, stride=SUBLANES), :] = h2[:, sl * LANES:(sl + 1) * LANES]
        x1_ref[0, pl.ds(sl, tm, stride=SUBLANES), :] = x1[:, sl * LANES:(sl + 1) * LANES]
    h2h, h2m, _ = _split3(h2)
    logits = jnp.dot(h2h, wrh_ref[...], preferred_element_type=F32) \
        + jnp.dot(h2m, wrh_ref[...], preferred_element_type=F32) \
        + jnp.dot(h2h, wrm_ref[...], preferred_element_type=F32)
    lt = logits.T[0:N_EXPERTS, :] + br_ref[...]
    mx = jnp.max(lt, axis=0, keepdims=True)
    ex = jnp.exp(lt - mx)
    aff_ref[0] = ex / jnp.sum(ex, axis=0, keepdims=True)


def _merge(x, pos, ha, hb, sa, sb, mod, mod_row_fn, nw, wts, tm):
    B, T, D = x.shape
    has_pos = pos is not None
    wpa, wpb, wout, wrh, wrm, br = wts
    assert D == SUBLANES * LANES, D
    const = lambda shape: pl.BlockSpec(shape, lambda b, i: (0,) * len(shape),
                                       pipeline_mode=pl.Buffered(1))
    tok = lambda width: pl.BlockSpec((1, tm, width), lambda b, i: (b, i, 0))
    in_specs = [tok(D)]
    args = [x]
    if has_pos:
        in_specs.append(pl.BlockSpec((tm, D), lambda b, i: (i, 0)))
        args.append(pos)
    in_specs += [tok(ha.shape[-1]), tok(hb.shape[-1]), tok(D), tok(D),
                 pl.BlockSpec((1, N_MOD, D), lambda b, i: (mod_row_fn(b), 0, 0)),
                 const((1, D)), const(wpa.shape), const(wpb.shape), const(wout.shape),
                 const(wrh.shape), const(wrm.shape), const(br.shape)]
    args += [ha, hb, sa, sb, mod, nw, wpa, wpb, wout, wrh, wrm, br]
    return pl.pallas_call(
        functools.partial(_merge_kernel, has_pos),
        grid=(B, T // tm),
        in_specs=in_specs,
        out_specs=[pl.BlockSpec((1, tm * SUBLANES, LANES), lambda b, i: (b, i, 0)),
                   pl.BlockSpec((1, tm * SUBLANES, LANES), lambda b, i: (b, i, 0)),
                   pl.BlockSpec((1, N_EXPERTS, tm), lambda b, i: (b, 0, i))],
        out_shape=[jax.ShapeDtypeStruct((B, T * SUBLANES, LANES), F32),
                   jax.ShapeDtypeStruct((B, T * SUBLANES, LANES), F32),
                   jax.ShapeDtypeStruct((B, N_EXPERTS, T), F32)],
        compiler_params=_cparams(("arbitrary", "arbitrary")),
        name="merge",
    )(*args)


def _route_kernel(cap, aff_ref, idx_ref, val_ref, jb_s, jl_s):
    aff = aff_ref[0]
    E, T = aff.shape
    thr_bits = jnp.zeros((E, 1), jnp.int32)
    for bit in range(30, -1, -1):
        cand = thr_bits | (1 << bit)
        cnt = jnp.sum((aff >= lax.bitcast_convert_type(cand, F32)).astype(F32), axis=1, keepdims=True)
        thr_bits = jnp.where(cnt >= cap, cand, thr_bits)
    thr = lax.bitcast_convert_type(thr_bits, F32)
    nxt = lax.bitcast_convert_type(thr_bits + 1, F32)
    gt = aff >= nxt
    eq = (aff >= thr) & jnp.logical_not(gt)
    need = cap - jnp.sum(gt.astype(F32), axis=1, keepdims=True)

    r = lax.broadcasted_iota(jnp.int32, (LANES, LANES), 0)
    c = lax.broadcasted_iota(jnp.int32, (LANES, LANES), 1)
    tri = (r < c).astype(BF16)

    def excl_cumsum(mk):
        outs = []
        carry = jnp.zeros((E, 1), F32)
        for kb in range(T // LANES):
            blk = mk[:, kb * LANES:(kb + 1) * LANES]
            outs.append(jnp.dot(blk.astype(BF16), tri, preferred_element_type=F32) + carry)
            carry = carry + jnp.sum(blk.astype(F32), axis=1, keepdims=True)
        return jnp.concatenate(outs, axis=1)

    tie_rank = excl_cumsum(eq)
    mask = gt | (eq & (tie_rank < need))
    slot = jnp.where(mask, excl_cumsum(mask), -1.0).astype(jnp.int32)
    JL = min(cap, LANES)
    njb = cap // JL
    NC = 5
    WR = -(-(njb * NC) // SUBLANES) * SUBLANES
    jb_s[...] = slot >> (JL.bit_length() - 1)
    jl_s[...] = slot & (JL - 1)
    tio = lax.broadcasted_iota(jnp.int32, (1, T), 1)
    t_hi = (tio >> 6).astype(F32)
    t_lo = (tio & 63).astype(F32)
    jl_col = lax.broadcasted_iota(jnp.int32, (JL, 1), 0)
    wrow = lax.broadcasted_iota(jnp.int32, (WR, 1), 0)
    wjb = jnp.full((WR, 1), -2, jnp.int32)
    for jb in range(njb):
        wjb = jnp.where((wrow >= NC * jb) & (wrow < NC * (jb + 1)), jb, wjb)
    wcomp = wrow - NC * wjb
    lane = lax.broadcasted_iota(jnp.int32, (JL, E), 1)
    nt = (((1,), (1,)), ((), ()))

    def per_expert(e, carry):
        a1, a2, a3 = _split3(aff_ref[0, pl.ds(e, 1), :])
        comp = jnp.where(wcomp == 0, t_hi,
                         jnp.where(wcomp == 1, t_lo,
                                   jnp.where(wcomp == 2, a1.astype(F32),
                                             jnp.where(wcomp == 3, a2.astype(F32), a3.astype(F32)))))
        wt = jnp.where(jb_s[pl.ds(e, 1), :] == wjb, comp, 0.0).astype(BF16)
        onehot = jnp.where(jl_s[pl.ds(e, 1), :] == jl_col, 1.0, 0.0).astype(BF16)
        r = lax.dot_general(onehot, wt, nt, preferred_element_type=F32)
        out = []
        for jb in range(njb):
            c0 = jb * NC
            ie = r[:, c0:c0 + 1] * 64.0 + r[:, c0 + 1:c0 + 2]
            ve = r[:, c0 + 2:c0 + 3] + r[:, c0 + 3:c0 + 4] + r[:, c0 + 4:c0 + 5]
            out.append(jnp.where(lane == e, ie, carry[2 * jb]))
            out.append(jnp.where(lane == e, ve, carry[2 * jb + 1]))
        return tuple(out)

    res = lax.fori_loop(0, E, per_expert, tuple(jnp.zeros((JL, E), F32) for _ in range(2 * njb)),
                        unroll=2)
    for jb in range(njb):
        idx_ref[0, jb * JL:(jb + 1) * JL, :] = res[2 * jb].astype(jnp.int32)
        val_ref[0, jb * JL:(jb + 1) * JL, :] = res[2 * jb + 1]


def _route(aff, cap):
    B, E0, T = aff.shape
    R = max(1, min(B, ROUTE_ROW_ELEMS // (E0 * T)))
    E = R * E0
    idx, val = pl.pallas_call(
        functools.partial(_route_kernel, cap),
        grid=(B // R,),
        in_specs=[pl.BlockSpec((1, E, T), lambda b: (b, 0, 0))],
        out_specs=[pl.BlockSpec((1, cap, E), lambda b: (b, 0, 0)),
                   pl.BlockSpec((1, cap, E), lambda b: (b, 0, 0))],
        out_shape=[jax.ShapeDtypeStruct((B // R, cap, E), jnp.int32),
                   jax.ShapeDtypeStruct((B // R, cap, E), F32)],
        scratch_shapes=[pltpu.VMEM((E, T), jnp.int32), pltpu.VMEM((E, T), jnp.int32)],
        compiler_params=_cparams(("arbitrary",)),
        name="route",
    )(aff.reshape(B // R, E, T))
    unbatch = lambda a: a.reshape(B // R, cap, R, E0).transpose(0, 2, 1, 3).reshape(B, cap, E0)
    return unbatch(idx), unbatch(val)


def _moe_kernel(idx_ref, idxn_ref, val_ref, valn_ref, tok_ref, g2_ref, nw_ref, w1_ref, w3_ref, w2_ref, x1_ref, out_ref,
                xe_s, xb_s, ye_s, yt_s, acc_s, yb_s, sem_in, sem_out):
    b, e, f = pl.program_id(0), pl.program_id(1), pl.program_id(2)
    ne = pl.num_programs(1)
    cap = ye_s.shape[0]
    S = SUBLANES
    GR = 8

    def tile(ref, r):
        return ref.at[pl.ds(pl.multiple_of(r * S, S), S), :]

    def gather_group(iref, g):
        rows = [iref[0, 0, g * GR + i] for i in range(GR)]
        vals = [tile(tok_ref.at[0], rows[i])[...] for i in range(GR)]
        for i in range(GR):
            tile(xe_s, g * GR + i)[...] = vals[i]

    def scatter_group(iref, vref, g):
        rows = [iref[0, 0, g * GR + i] for i in range(GR)]
        vals = [tile(acc_s, rows[i])[...] + tile(yt_s, g * GR + i)[...] * vref[0, 0, g * GR + i]
                for i in range(GR)]
        for i in range(GR):
            tile(acc_s, rows[i])[...] = vals[i]

    def ffn_half():
        xe = xb_s[...]
        h1 = jnp.dot(xe, w1_ref[0], preferred_element_type=F32)
        h3 = jnp.dot(xe, w3_ref[0], preferred_element_type=F32)
        hid = (h1 * _sigmoid(h1) * h3).astype(BF16)
        return jnp.dot(hid, w2_ref[0], preferred_element_type=F32)

    @pl.when((e == 0) & (f == 0))
    def _():
        cp_in = pltpu.make_async_copy(x1_ref.at[b], acc_s, sem_in)
        cp_in.start()
        yt_s[...] = jnp.zeros_like(yt_s)
        lax.fori_loop(0, cap // GR, lambda g, c: (gather_group(idx_ref, g), c)[1], 0)
        cp_in.wait()

    @pl.when(f == 0)
    def _():
        xb_s[...] = jnp.concatenate([xe_s[pl.ds(sl, cap, stride=S), :] for sl in range(S)],
                                    axis=1).astype(BF16)
        ye_s[...] = ffn_half()
        for g in range(cap // GR):
            scatter_group(idxn_ref, valn_ref, g)

    @pl.when(f == 1)
    def _():
        yw = (ye_s[...] + ffn_half()) * g2_ref[0]
        for sl in range(S):
            yt_s[pl.ds(sl, cap, stride=S), :] = yw[:, sl * LANES:(sl + 1) * LANES]
        for g in range(cap // GR):
            gather_group(idxn_ref, g)

    @pl.when((e == ne - 1) & (f == 1))
    def _():
        lax.fori_loop(0, cap // GR, lambda g, c: (scatter_group(idx_ref, val_ref, g), c)[1], 0)
        ch = yb_s.shape[1]
        nch = acc_s.shape[0] // (ch * S)
        copies = []
        for c in range(nch):
            slot = c % 2
            if c >= 2:
                copies[c - 2].wait()
            x = jnp.concatenate([acc_s[pl.ds(c * ch * S + sl, ch, stride=S), :] for sl in range(S)], axis=1)
            ms = jnp.mean(x * x, axis=-1, keepdims=True)
            yb_s[slot] = x * lax.rsqrt(ms + EPS) * nw_ref[...]
            cp = pltpu.make_async_copy(yb_s.at[slot], out_ref.at[b, pl.ds(c * ch, ch), :], sem_out.at[slot])
            cp.start()
            copies.append(cp)
        for c in range(max(nch - 2, 0), nch):
            copies[c].wait()


def _moe(idx, tok, x1, val, gate2, nw, w1, w3, w2):
    B, NR, _ = tok.shape
    E, D, FF = w1.shape
    cap = idx.shape[-1]
    nf = EXPERT_FF_SPLIT
    assert nf == 2, nf
    fh = FF // nf
    NT = NR // SUBLANES
    ch = min(MOE_OUT_CHUNK, NT)
    own = pl.BlockSpec((1, 1, cap), lambda b, e, f: (b * E + e, 0, 0), memory_space=pltpu.SMEM)
    nbr = pl.BlockSpec((1, 1, cap), lambda b, e, f: (b * E + jnp.clip(e - 1 + 2 * f, 0, E - 1), 0, 0),
                       memory_space=pltpu.SMEM)
    return pl.pallas_call(
        _moe_kernel,
        grid=(B, E, nf),
        in_specs=[own, nbr, own, nbr,
                  pl.BlockSpec((1, NR, LANES), lambda b, e, f: (b, 0, 0), pipeline_mode=pl.Buffered(1)),
                  pl.BlockSpec((1, 1, D), lambda b, e, f: (b, 0, 0)),
                  pl.BlockSpec((1, D), lambda b, e, f: (0, 0)),
                  pl.BlockSpec((1, D, fh), lambda b, e, f: (e, 0, f)),
                  pl.BlockSpec((1, D, fh), lambda b, e, f: (e, 0, f)),
                  pl.BlockSpec((1, fh, D), lambda b, e, f: (e, f, 0)),
                  pl.BlockSpec(memory_space=pl.ANY)],
        out_specs=pl.BlockSpec(memory_space=pl.ANY),
        out_shape=jax.ShapeDtypeStruct((B, NT, D), F32),
        scratch_shapes=[pltpu.VMEM((cap * SUBLANES, LANES), F32), pltpu.VMEM((cap, D), BF16),
                        pltpu.VMEM((cap, D), F32), pltpu.VMEM((cap * SUBLANES, LANES), F32),
                        pltpu.VMEM((NR, LANES), F32), pltpu.VMEM((2, ch, D), F32),
                        pltpu.SemaphoreType.DMA(()), pltpu.SemaphoreType.DMA((2,))],
        compiler_params=_cparams(("arbitrary", "arbitrary", "arbitrary")),
        name="moe",
    )(idx, idx, val, val, tok, gate2, nw, w1, w3, w2, x1)


def _pos2d(T, D):
    rows = T // GRID_W
    quarter = D // 4
    freqs = 1.0 / (10000.0 ** (jnp.arange(quarter, dtype=F32) / quarter))
    er = jnp.arange(rows, dtype=F32)[:, None] * freqs
    ec = jnp.arange(GRID_W, dtype=F32)[:, None] * freqs
    rep = lambda a: jnp.repeat(a, GRID_W, axis=0)
    til = lambda a: jnp.tile(a, (rows, 1))
    return jnp.concatenate([rep(jnp.sin(er)), rep(jnp.cos(er)), til(jnp.sin(ec)), til(jnp.cos(ec))], axis=-1)


def _stream(x, pos, mod, mod_row_fn, P, mstate, s5_x0, tm):
    B, T, D = x.shape
    L = min(MLSTM_CHUNK, T)
    q, k, v, o, gc, gr, gtot, u, sa, sb = _inproj(x, pos, mod, mod_row_fn, P['norm_mix'],
                                                   P['w_in_parts'], tm, L)
    res = _mlstm(q, k, v, o, gc, gr, gtot, P['head_norm'], mstate, L)
    ha, mout = res[0], res[1:]
    Lc = min(S5_CHUNK, T)
    yf, xf = _s5_pass(False, u, None, s5_x0[0], *P['s5_dir'][0], None, Lc)
    hb, xb = _s5_pass(True, u, yf, s5_x0[1], *P['s5_dir'][1], P['glu'], Lc)
    x1, hp, aff = _merge(x, pos, ha, hb, sa, sb, mod, mod_row_fn, P['norm_ffn'], P['merge_w'], tm)
    cap = CAPACITY_FACTOR * T // N_EXPERTS
    idx, val = _route(aff, cap)
    E = N_EXPERTS
    gb = max(1, min(B, MOE_TABLE_ROWS // T))
    if any(mod_row_fn(r) != mod_row_fn(r - r % gb) for r in range(B)):
        gb = 1
    nb = B // gb
    offs = (jnp.arange(B, dtype=jnp.int32) % gb) * T
    idx_f = (idx + offs[:, None, None]).reshape(nb, gb, cap, E).transpose(0, 3, 1, 2).reshape(nb * E, 1, gb * cap)
    val_f = val.reshape(nb, gb, cap, E).transpose(0, 3, 1, 2).reshape(nb * E, 1, gb * cap)
    gate2 = jnp.stack([mod[mod_row_fn(t * gb), N_MOD - 1] for t in range(nb)])[:, None, :]
    tiles = lambda a: a.reshape(nb, gb * T * SUBLANES, LANES)
    y = _moe(idx_f, tiles(hp), tiles(x1), val_f, gate2, P['norm_final'], *P['experts']).reshape(B, T, D)
    return y, mout, (xf, xb)


def kernel(x_prompt, x_sample, state_mlstm_C, state_mlstm_n, state_mlstm_m, state_s5_re, state_s5_im,
           c, c_ctx, w_ada, b_ada, norm_mix, norm_ffn, w_in, b_gate, head_norm,
           s5_a_re, s5_a_im, s5_log_dt, s5_b_re, s5_b_im, s5_c_re, s5_c_im, s5_d, w_glu, b_glu,
           w_pA, w_pB, w_out, w_router, b_router, w_e1, w_e3, w_e2, norm_final):
    Bp, Tp, D = x_prompt.shape
    Bs, Ts, _ = x_sample.shape
    depth = w_ada.shape[0]
    H, dh = MLSTM_HEADS, MLSTM_HEAD_DIM
    W = H * dh
    G, Pn = s5_a_re.shape[2], s5_a_re.shape[3]
    SW = G * S5_GROUP

    assert depth == 1, depth
    l = 0
    pos = _pos2d(Ts, D)
    nrow = Bs + 1
    rows = -(-nrow // SUBLANES) * SUBLANES
    cpad = jnp.zeros((rows, D), F32).at[0:Bs].set(c).at[Bs].set(c_ctx)
    mod = _adaln(cpad, w_ada[l], b_ada[l]).reshape(rows, N_MOD, D)

    wi = w_in[l]
    gcols = wi[:, 4 * W:4 * W + 4 * H].reshape(D, 4, H).transpose(0, 2, 1).reshape(D, 4 * H)
    wug = jnp.concatenate([wi[:, 4 * W + 4 * H:4 * W + 4 * H + SW], gcols,
                           jnp.zeros((D, LANES - 4 * H), F32)], axis=1).astype(BF16)
    bg = jnp.concatenate([b_gate[l].T.reshape(1, 4 * H), jnp.zeros((1, LANES - 4 * H), F32)], axis=1)
    w_in_parts = (wi[:, 0:3 * W].astype(BF16), wi[:, 3 * W:4 * W].astype(BF16), wug,
                  wi[:, 4 * W + 4 * H + SW:].astype(BF16), bg)
    s5_dir = [_s5_discretise(s5_a_re[l, d], s5_a_im[l, d], s5_log_dt[l, d], s5_b_re[l], s5_b_im[l],
                             s5_c_re[l, d], s5_c_im[l, d]) for d in range(2)]
    s5_dir = [(jnp.broadcast_to(a, (SUBLANES, a.shape[1])), jnp.broadcast_to(s, (SUBLANES, s.shape[1])), bw, cw)
              for (a, s, bw, cw) in s5_dir]
    wr = jnp.concatenate([w_router[l], jnp.zeros((D, LANES - N_EXPERTS), F32)], axis=1)
    wr_hi = wr.astype(BF16)
    wr_mid = (wr - wr_hi.astype(F32)).astype(BF16)
    P = {
        'norm_mix': norm_mix[l].reshape(1, D), 'norm_ffn': norm_ffn[l].reshape(1, D),
        'norm_final': norm_final.reshape(1, D),
        'w_in_parts': w_in_parts, 'head_norm': head_norm[l], 's5_dir': s5_dir,
        'glu': (s5_d[l].reshape(1, SW), w_glu[l].astype(BF16), b_glu[l].reshape(1, SW)),
        'merge_w': (w_pA[l].astype(BF16), w_pB[l].astype(BF16), w_out[l].astype(BF16), wr_hi, wr_mid,
                    b_router[l].reshape(N_EXPERTS, 1)),
        'experts': (w_e1[l].astype(BF16), w_e3[l].astype(BF16), w_e2[l].astype(BF16)),
    }
    zero_x0 = jnp.zeros((Bp // SUBLANES, SUBLANES, 2 * G * Pn), F32)
    yp, mout, (xf, xb) = _stream(x_prompt, None, mod, lambda b: Bs, P, None, (zero_x0, zero_x0),
                                 min(MLSTM_CHUNK, Tp))
    lat_x0 = [_s5_state_to_lanes(state_s5_re[:, l, d], state_s5_im[:, l, d]) for d in range(2)]
    ys, _, _ = _stream(x_sample, pos, mod, lambda b: b, P,
                       (state_mlstm_C[:, l], state_mlstm_n[:, l], state_mlstm_m[:, l]), lat_x0,
                       min(MLSTM_CHUNK, Ts))
    Cn, nn, mn = mout
    fre, fim = _s5_lanes_to_state(xf, G, Pn)
    bre, bim = _s5_lanes_to_state(xb, G, Pn)
    outs = (Cn, nn.reshape(Bp, 2, H, dh), mn.reshape(Bp, 2, H),
            jnp.stack([fre, bre], axis=1), jnp.stack([fim, bim], axis=1))
    out_C, out_n, out_m, out_re, out_im = (o[:, None] for o in outs)
    return (yp, ys, out_C, out_n, out_m, out_re, out_im)
```

```python
import functools

import jax
import jax.numpy as jnp
from jax import lax
from jax.experimental import pallas as pl
from jax.experimental.pallas import tpu as pltpu

F32 = jnp.float32
BF16 = jnp.bfloat16
EPS = 1e-6

MLSTM_HEADS = 4
MLSTM_HEAD_DIM = 256
N_EXPERTS = 16
CAPACITY_FACTOR = 2
S5_GROUP = 16
S5_STATE = 64
GRID_W = 64
N_MOD = 6

LANES = 128
SUBLANES = 8
VMEM_LIMIT_BYTES = 60000 * 1024

MLSTM_CHUNK = 256
S5_CHUNK = 128
S5_LANE_GROUPS = 4
S5_SEGMENTS = 4
EXPERT_FF_SPLIT = 2
ROUTE_ROW_ELEMS = 16 * 4096
MOE_OUT_CHUNK = 256
MOE_TABLE_ROWS = 4096


def _cparams(sem):
    return pltpu.CompilerParams(dimension_semantics=sem, vmem_limit_bytes=VMEM_LIMIT_BYTES)


def _sigmoid(x):
    return jax.nn.sigmoid(x)


def _log_sigmoid(x):
    return jnp.minimum(x, 0.0) - jnp.log1p(jnp.exp(-jnp.abs(x)))


def _adaln_kernel(c_ref, w_ref, b_ref, o_ref):
    c = c_ref[...]
    s = c * _sigmoid(c)
    o_ref[...] = jnp.dot(s, w_ref[...], precision=lax.Precision.HIGHEST,
                         preferred_element_type=F32) + b_ref[...]


def _adaln(cpad, w, b):
    rows, d = cpad.shape
    n = w.shape[1]
    tn = 1536
    return pl.pallas_call(
        _adaln_kernel,
        grid=(n // tn,),
        in_specs=[pl.BlockSpec((rows, d), lambda j: (0, 0)),
                  pl.BlockSpec((d, tn), lambda j: (0, j)),
                  pl.BlockSpec((1, tn), lambda j: (0, j))],
        out_specs=pl.BlockSpec((rows, tn), lambda j: (0, j)),
        out_shape=jax.ShapeDtypeStruct((rows, n), F32),
        compiler_params=_cparams(("arbitrary",)),
        name="adaln",
    )(cpad, w, b.reshape(1, n))


def _rms_mod(x, nw, shift, scale):
    ms = jnp.mean(x * x, axis=-1, keepdims=True)
    y = x * lax.rsqrt(ms + EPS) * nw
    return y * (1.0 + scale) + shift


def _split3(x):
    hi = x.astype(BF16)
    r = x - hi.astype(F32)
    mid = r.astype(BF16)
    lo = (r - mid.astype(F32)).astype(BF16)
    return hi, mid, lo


def _inproj_kernel(has_pos, *refs):
    if has_pos:
        x_ref, pos_ref = refs[0], refs[1]
        refs = refs[2:]
    else:
        x_ref, pos_ref = refs[0], None
        refs = refs[1:]
    (mod_ref, nw_ref, wqkv_ref, wo_ref, wug_ref, wab_ref, bg_ref,
     q_ref, k_ref, v_ref, o_ref, gc_ref, gr_ref, gtot_ref, u_ref, sa_ref, sb_ref) = refs
    x = x_ref[0]
    if has_pos:
        x = x + pos_ref[...]
    h = _rms_mod(x, nw_ref[...], mod_ref[0, 0:1, :], mod_ref[0, 1:2, :])
    hb = h.astype(BF16)
    w = MLSTM_HEADS * MLSTM_HEAD_DIM
    qkv = jnp.dot(hb, wqkv_ref[...], preferred_element_type=F32)
    q_ref[0] = (qkv[:, 0:w] * (MLSTM_HEAD_DIM ** -0.5)).astype(BF16)
    k_ref[0] = qkv[:, w:2 * w].astype(BF16)
    vt = qkv[:, 2 * w:3 * w].T.astype(BF16)
    for hh in range(MLSTM_HEADS):
        v_ref[0, hh, 0] = vt[hh * MLSTM_HEAD_DIM:(hh + 1) * MLSTM_HEAD_DIM, :]
    ot = jnp.dot(hb, wo_ref[...], preferred_element_type=F32).T
    for hh in range(MLSTM_HEADS):
        o_ref[0, hh, 0] = ot[hh * MLSTM_HEAD_DIM:(hh + 1) * MLSTM_HEAD_DIM, :]
    ug = jnp.dot(hb, wug_ref[...], preferred_element_type=F32)
    su = u_ref.shape[-1]
    u_ref[0] = ug[:, 0:su]
    g = ug[:, su:su + LANES] + bg_ref[...]
    col = lax.broadcasted_iota(jnp.int32, g.shape, 1)
    g = jnp.where((col % 2) == 1, _log_sigmoid(g), g)
    tm = g.shape[0]
    tril = (lax.broadcasted_iota(jnp.int32, (tm, tm), 1)
            <= lax.broadcasted_iota(jnp.int32, (tm, tm), 0)).astype(BF16)
    cs = sum(jnp.dot(tril, part, preferred_element_type=F32) for part in _split3(g))
    tot = cs[tm - 1:tm, :]
    cum = jnp.where((col % 4) < 2, cs, tot - cs + g)
    cum_l = pltpu.roll(cum, LANES - 1, axis=1)
    tot_l = pltpu.roll(jnp.broadcast_to(tot, g.shape), LANES - 1, axis=1)
    r = g - cum_l
    cumt = cum.T
    ggt = (tot_l - cum_l + g).T
    for hh in range(MLSTM_HEADS):
        gc_ref[0, hh] = r[:, 4 * hh:4 * hh + 4]
        gr_ref[0, hh, 0, 0:4, :] = cumt[4 * hh:4 * hh + 4, :]
        gr_ref[0, hh, 0, 4:8, :] = ggt[4 * hh:4 * hh + 4, :]
        gtot_ref[0, hh, 0] = tot[:, 4 * hh:4 * hh + 4]
    ab = jnp.dot(hb, wab_ref[...], preferred_element_type=F32)
    d = sa_ref.shape[-1]
    sa_ref[0] = _sigmoid(ab[:, 0:d])
    sb_ref[0] = _sigmoid(ab[:, d:2 * d])


def _inproj(x, pos, mod, mod_row_fn, nw, wts, tm, L):
    B, T, D = x.shape
    has_pos = pos is not None
    wqkv, wo, wug, wab, bg = wts
    su = wug.shape[1] - LANES
    assert tm == L, (tm, L)
    const = lambda shape: pl.BlockSpec(shape, lambda b, i: (0,) * len(shape),
                                       pipeline_mode=pl.Buffered(1))
    in_specs = [pl.BlockSpec((1, tm, D), lambda b, i: (b, i, 0))]
    args = [x]
    if has_pos:
        in_specs.append(pl.BlockSpec((tm, D), lambda b, i: (i, 0)))
        args.append(pos)
    in_specs += [pl.BlockSpec((1, N_MOD, D), lambda b, i: (mod_row_fn(b), 0, 0)),
                 const((1, D)), const(wqkv.shape), const(wo.shape), const(wug.shape),
                 const(wab.shape), const((1, LANES))]
    args += [mod, nw, wqkv, wo, wug, wab, bg]
    W = MLSTM_HEADS * MLSTM_HEAD_DIM
    tok = lambda width, dt: (jax.ShapeDtypeStruct((B, T, width), dt),
                             pl.BlockSpec((1, tm, width), lambda b, i: (b, i, 0)))
    outs = [tok(W, BF16), tok(W, BF16),
            (jax.ShapeDtypeStruct((B, MLSTM_HEADS, T // L, MLSTM_HEAD_DIM, L), BF16),
             pl.BlockSpec((1, MLSTM_HEADS, 1, MLSTM_HEAD_DIM, L), lambda b, i: (b, 0, i, 0, 0))),
            (jax.ShapeDtypeStruct((B, MLSTM_HEADS, T // L, MLSTM_HEAD_DIM, L), F32),
             pl.BlockSpec((1, MLSTM_HEADS, 1, MLSTM_HEAD_DIM, L), lambda b, i: (b, 0, i, 0, 0))),
            (jax.ShapeDtypeStruct((B, MLSTM_HEADS, T, 4), F32),
             pl.BlockSpec((1, MLSTM_HEADS, tm, 4), lambda b, i: (b, 0, i, 0))),
            (jax.ShapeDtypeStruct((B, MLSTM_HEADS, T // L, 8, L), F32),
             pl.BlockSpec((1, MLSTM_HEADS, 1, 8, L), lambda b, i: (b, 0, i, 0, 0))),
            (jax.ShapeDtypeStruct((B, MLSTM_HEADS, T // L, 1, 4), F32),
             pl.BlockSpec((1, MLSTM_HEADS, 1, 1, 4), lambda b, i: (b, 0, i, 0, 0))),
            tok(su, F32), tok(D, F32), tok(D, F32)]
    return pl.pallas_call(
        functools.partial(_inproj_kernel, has_pos),
        grid=(B, T // tm),
        in_specs=in_specs,
        out_specs=[o[1] for o in outs],
        out_shape=[o[0] for o in outs],
        compiler_params=_cparams(("arbitrary", "arbitrary")),
        name="inproj",
    )(*args)


MLSTM_AUG_ROWS = 16


def _mlstm_chunk(q, k, vt_aug, r_col, b_row, gg_row, total, C_ref, m_ref, d, valid_t):
    dh = q.shape[1]
    Ca = C_ref[d]
    m = m_ref[d]
    nt = (((1,), (1,)), ((), ()))
    rmask = jnp.where(valid_t, r_col, -jnp.inf)
    mx = jnp.maximum(m, jnp.max(rmask, axis=0, keepdims=True))
    p = jnp.exp(rmask - mx)
    w_inter = jnp.exp(m - mx)
    st = lax.dot_general(k, q, nt, preferred_element_type=F32) * p
    na = w_inter * lax.dot_general(Ca.astype(BF16), q, nt, preferred_element_type=F32) \
        + jnp.dot(vt_aug, st.astype(BF16), preferred_element_type=F32)
    rden = 1.0 / jnp.maximum(jnp.abs(na[dh:dh + 1, :]), jnp.exp(-(b_row + mx)))
    ht = na[0:dh, :] * rden
    m_new = jnp.maximum(total + m, jnp.max(gg_row, axis=1, keepdims=True))
    decay = jnp.exp(total + m - m_new)
    vw = (vt_aug.astype(F32) * jnp.exp(gg_row - m_new)).astype(BF16)
    C_ref[d] = decay * Ca + jnp.dot(vw, k, preferred_element_type=F32)
    m_ref[d] = m_new
    return ht


def _mlstm_kernel(zero_init, L, *refs):
    T, dh = refs[0].shape[1], refs[0].shape[2]
    if zero_init:
        (q_ref, k_ref, v_ref, o_ref, gc_ref, gr_ref, gtot_ref, hn_ref,
         out_ref, Co_ref, no_ref, mo_ref, C_s, m_s, hf_s, hb_s) = refs
        C_s[...] = jnp.zeros_like(C_s)
        m_s[...] = jnp.zeros_like(m_s)
    else:
        (q_ref, k_ref, v_ref, o_ref, gc_ref, gr_ref, gtot_ref, hn_ref, C0_ref, n0_ref, m0_ref,
         out_ref, C_s, m_s, hf_s, hb_s) = refs
        for d in range(2):
            C_s[d, 0:dh, :] = C0_ref[0, d, 0].T
            C_s[d, dh:dh + MLSTM_AUG_ROWS, :] = jnp.broadcast_to(n0_ref[0, d, 0], (MLSTM_AUG_ROWS, dh))
        m_s[...] = m0_ref[0, :, 0]
    nc = T // L
    row = lax.broadcasted_iota(jnp.int32, (L, L), 0)
    col = lax.broadcasted_iota(jnp.int32, (L, L), 1)
    masks_t = (row <= col, row >= col)
    ones = jnp.ones((MLSTM_AUG_ROWS, L), BF16)

    def body(c, carry):
        for d in range(2):
            cc = c if d == 0 else nc - 1 - c
            r0 = pl.multiple_of(cc * L, L)
            q = q_ref[0, pl.ds(r0, L), :]
            k = k_ref[0, pl.ds(r0, L), :]
            vt_aug = jnp.concatenate([v_ref[0, 0, cc], ones], axis=0)
            gc = gc_ref[0, 0, pl.ds(r0, L), :]
            gr = gr_ref[0, 0, cc]
            tot = gtot_ref[0, 0, cc]
            ht = _mlstm_chunk(q, k, vt_aug, gc[:, 2 * d:2 * d + 1], gr[2 * d + 1:2 * d + 2, :],
                              gr[4 + 2 * d:5 + 2 * d, :], tot[:, 2 * d + 1:2 * d + 2],
                              C_s, m_s, d, masks_t[d])
            (hf_s if d == 0 else hb_s)[cc] = ht
        return carry

    lax.fori_loop(0, nc, body, 0, unroll=4 if nc % 4 == 0 else 1)

    hn = jnp.concatenate([hn_ref[0]] * (L // LANES), axis=1)

    def fin(c, carry):
        hm = (hf_s[c] + hb_s[c]) * _sigmoid(o_ref[0, 0, c])
        hm = hm * lax.rsqrt(jnp.mean(hm * hm, axis=0, keepdims=True) + EPS) * hn
        out_ref[0, 0, c] = hm.astype(out_ref.dtype)
        return carry

    lax.fori_loop(0, nc, fin, 0)
    if zero_init:
        for d in range(2):
            Co_ref[0, d, 0] = C_s[d, 0:dh, :].T
        no_ref[0, :, 0] = C_s[:, dh:dh + 1, :]
        mo_ref[0, :, 0] = m_s[...]


def _mlstm(q, k, v, o, gc, gr, gtot, head_norm, state, L):
    B, T, W = q.shape
    H, dh = MLSTM_HEADS, MLSTM_HEAD_DIM
    nc = T // L
    zero_init = state is None
    bh = lambda width: pl.BlockSpec((1, T, width), lambda b, h: (b, 0, h))
    per_head_t = pl.BlockSpec((1, 1, nc, dh, L), lambda b, h: (b, h, 0, 0, 0))
    in_specs = [bh(dh), bh(dh), per_head_t, per_head_t,
                pl.BlockSpec((1, 1, T, 4), lambda b, h: (b, h, 0, 0)),
                pl.BlockSpec((1, 1, nc, 8, L), lambda b, h: (b, h, 0, 0, 0)),
                pl.BlockSpec((1, 1, nc, 1, 4), lambda b, h: (b, h, 0, 0, 0)),
                pl.BlockSpec((1, dh, LANES), lambda b, h: (h, 0, 0))]
    hn_cols = jnp.broadcast_to(head_norm.reshape(H, dh, 1), (H, dh, LANES))
    args = [q, k, v, o, gc, gr, gtot, hn_cols]
    st_specs = [pl.BlockSpec((1, 2, 1, dh, dh), lambda b, h: (b, 0, h, 0, 0)),
                pl.BlockSpec((1, 2, 1, 1, dh), lambda b, h: (b, 0, h, 0, 0)),
                pl.BlockSpec((1, 2, 1, 1, 1), lambda b, h: (b, 0, h, 0, 0))]
    st_shapes = [jax.ShapeDtypeStruct((B, 2, H, dh, dh), F32),
                 jax.ShapeDtypeStruct((B, 2, H, 1, dh), F32),
                 jax.ShapeDtypeStruct((B, 2, H, 1, 1), F32)]
    out_specs = [per_head_t]
    out_shape = [jax.ShapeDtypeStruct((B, H, nc, dh, L), BF16)]
    if zero_init:
        out_specs += st_specs
        out_shape += st_shapes
    else:
        C0, n0, m0 = state
        in_specs += st_specs
        args += [C0, n0.reshape(B, 2, H, 1, dh), m0.reshape(B, 2, H, 1, 1)]
    return pl.pallas_call(
        functools.partial(_mlstm_kernel, zero_init, L),
        grid=(B, H),
        in_specs=in_specs,
        out_specs=out_specs,
        out_shape=out_shape,
        scratch_shapes=[pltpu.VMEM((2, dh + MLSTM_AUG_ROWS, dh), F32), pltpu.VMEM((2, 1, 1), F32),
                        pltpu.VMEM((nc, dh, L), F32), pltpu.VMEM((nc, dh, L), F32)],
        compiler_params=_cparams(("arbitrary", "arbitrary")),
        name="mlstm",
    )(*args)


def _s5_kernel(backward, Lc, *refs):
    if backward:
        (u_ref, yf_ref, x0_ref, ar_ref, as_ref, bw_ref, cw_ref, d_ref, wg_ref, bgl_ref,
         out_ref, xl_ref, X_s, st_s, U_s, H_s) = refs
    else:
        (u_ref, x0_ref, ar_ref, as_ref, bw_ref, cw_ref, out_ref, xl_ref, X_s, st_s, U_s) = refs
    ci = pl.program_id(1)
    nj = S5_LANE_GROUPS
    half = X_s.shape[1] // (2 * nj)
    nreq = u_ref.shape[0]

    @pl.when(ci == 0)
    def _():
        st_s[...] = x0_ref[0]

    for r in range(nreq):
        for j in range(nj):
            U_s[j, pl.ds(r, Lc, stride=nreq), :] = u_ref[r, :, j * LANES:(j + 1) * LANES]
    nseg = S5_SEGMENTS
    seg_t = Lc // nseg
    seg_order = range(nseg - 1, -1, -1) if backward else range(nseg)

    def rows_of(sg):
        return slice(sg * seg_t * SUBLANES, (sg + 1) * seg_t * SUBLANES)

    for sg in seg_order:
        for j in range(nj):
            X_s[rows_of(sg), 2 * half * j:2 * half * (j + 1)] = jnp.dot(
                U_s[j, rows_of(sg), :].astype(BF16), bw_ref[j], preferred_element_type=F32)

    def swap(x):
        parts = []
        for j in range(nj):
            parts.append(x[:, 2 * half * j + half:2 * half * (j + 1)])
            parts.append(x[:, 2 * half * j:2 * half * j + half])
        return jnp.concatenate(parts, axis=1)

    x = st_s[...]
    for sg in seg_order:
        ts = range(sg * seg_t, (sg + 1) * seg_t)
        for t in (reversed(ts) if backward else ts):
            rs = slice(t * SUBLANES, (t + 1) * SUBLANES)
            x = ar_ref[...] * x + as_ref[...] * swap(x) + X_s[rs, :]
            X_s[rs, :] = x
        y = jnp.concatenate(
            [jnp.dot(X_s[rows_of(sg), 2 * half * j:2 * half * (j + 1)].astype(BF16), cw_ref[j],
                     preferred_element_type=F32) for j in range(nj)], axis=1)
        if backward:
            u = jnp.concatenate([U_s[j, rows_of(sg), :] for j in range(nj)], axis=1)
            z = yf_ref[0, rows_of(sg), :] + y + d_ref[...] * u
            gate = _sigmoid(jnp.dot(z.astype(BF16), wg_ref[...], preferred_element_type=F32) + bgl_ref[...])
            hb = jax.nn.gelu(z) * gate
            for j in range(nj):
                H_s[j, rows_of(sg), :] = hb[:, j * LANES:(j + 1) * LANES]
        else:
            out_ref[0, rows_of(sg), :] = y
    st_s[...] = x
    xl_ref[0] = x
    if backward:
        for r in range(nreq):
            for j in range(nj):
                out_ref[r, :, j * LANES:(j + 1) * LANES] = \
                    H_s[j, pl.ds(r, Lc, stride=nreq), :].astype(out_ref.dtype)


def _s5_pass(backward, u, yf, x0, ar, asg, bw, cw, glu, Lc):
    B, T, su = u.shape
    G8 = B // SUBLANES
    rows = Lc * SUBLANES
    nchunk = T // Lc
    SL = x0.shape[-1]
    nj = S5_LANE_GROUPS
    assert su == nj * LANES, su
    cidx = (lambda c: nchunk - 1 - c) if backward else (lambda c: c)
    reqspec = pl.BlockSpec((SUBLANES, Lc, su), lambda g, c: (g, cidx(c), 0))
    rowspec = pl.BlockSpec((1, rows, su), lambda g, c: (g, cidx(c), 0))
    const = lambda shape: pl.BlockSpec(shape, lambda g, c: (0,) * len(shape))
    in_specs = [reqspec]
    args = [u]
    if backward:
        in_specs.append(rowspec)
        args.append(yf)
    in_specs += [pl.BlockSpec((1, SUBLANES, SL), lambda g, c: (g, 0, 0)),
                 const((SUBLANES, SL)), const((SUBLANES, SL)), const(bw.shape), const(cw.shape)]
    args += [x0, ar, asg, bw, cw]
    scratch = [pltpu.VMEM((rows, SL), F32), pltpu.VMEM((SUBLANES, SL), F32),
               pltpu.VMEM((nj, rows, LANES), F32)]
    if backward:
        d, wg, bgl = glu
        in_specs += [const((1, su)), const(wg.shape), const((1, su))]
        args += [d, wg, bgl]
        scratch.append(pltpu.VMEM((nj, rows, LANES), F32))
        out0 = (jax.ShapeDtypeStruct((B, T, su), BF16), reqspec)
    else:
        out0 = (jax.ShapeDtypeStruct((G8, T * SUBLANES, su), F32), rowspec)
    return pl.pallas_call(
        functools.partial(_s5_kernel, backward, Lc),
        grid=(G8, nchunk),
        in_specs=in_specs,
        out_specs=[out0[1], pl.BlockSpec((1, SUBLANES, SL), lambda g, c: (g, 0, 0))],
        out_shape=[out0[0], jax.ShapeDtypeStruct((G8, SUBLANES, SL), F32)],
        scratch_shapes=scratch,
        compiler_params=_cparams(("arbitrary", "arbitrary")),
        name="s5_bwd" if backward else "s5_fwd",
    )(*args)


def _s5_discretise(a_re, a_im, log_dt, b_re, b_im, c_re, c_im):
    G, P = a_re.shape
    nj = S5_LANE_GROUPS
    gl = G // nj
    dt = jnp.exp(log_dt)[:, None]
    e = jnp.exp(a_re * dt)
    ar = e * jnp.cos(a_im * dt)
    ai = e * jnp.sin(a_im * dt)
    den = a_re * a_re + a_im * a_im
    fr = ((ar - 1.0) * a_re + ai * a_im) / den
    fi = (ai * a_re - (ar - 1.0) * a_im) / den
    bbr = fr[..., None] * b_re - fi[..., None] * b_im
    bbi = fr[..., None] * b_im + fi[..., None] * b_re

    def lanes(re, im):
        return jnp.stack([re.reshape(nj, gl * P), im.reshape(nj, gl * P)], axis=1).reshape(1, -1)

    a_row = lanes(ar, ar)
    as_row = lanes(-ai, ai)
    eye = jnp.eye(gl, dtype=F32)
    C = b_re.shape[-1]

    def bmat(bb):
        x = bb.reshape(nj, gl, P, C)
        return jnp.einsum('jgpc,gh->jgchp', x, eye).reshape(nj, gl * C, gl * P)

    bw = jnp.concatenate([bmat(bbr), bmat(bbi)], axis=2).astype(BF16)

    def cmat(cc):
        x = cc.reshape(nj, gl, C, P)
        return jnp.einsum('jgcp,gh->jgphc', x, eye).reshape(nj, gl * P, gl * C)

    cw = jnp.concatenate([cmat(c_re), cmat(-c_im)], axis=1).astype(BF16)
    return a_row, as_row, bw, cw


def _s5_state_to_lanes(s_re, s_im):
    B, G, P = s_re.shape
    nj = S5_LANE_GROUPS
    x = jnp.stack([s_re.reshape(B, nj, (G // nj) * P), s_im.reshape(B, nj, (G // nj) * P)], axis=2)
    return x.reshape(B // SUBLANES, SUBLANES, 2 * G * P)


def _s5_lanes_to_state(x, G, P):
    G8 = x.shape[0]
    nj = S5_LANE_GROUPS
    x = x.reshape(G8 * SUBLANES, nj, 2, G // nj, P)
    return x[:, :, 0].reshape(-1, G, P), x[:, :, 1].reshape(-1, G, P)


def _merge_kernel(has_pos, *refs):
    if has_pos:
        x_ref, pos_ref = refs[0], refs[1]
        refs = refs[2:]
    else:
        x_ref, pos_ref = refs[0], None
        refs = refs[1:]
    (ha_ref, hb_ref, sa_ref, sb_ref, mod_ref, nw_ref, wpa_ref, wpb_ref, wout_ref, wrh_ref, wrm_ref,
     br_ref, x1_ref, hp_ref, aff_ref) = refs
    x = x_ref[0]
    if has_pos:
        x = x + pos_ref[...]
    tn = (((0,), (0,)), ((), ()))
    dh = ha_ref.shape[3]
    pa = sum(lax.dot_general(ha_ref[0, hh, 0], wpa_ref[hh * dh:(hh + 1) * dh, :], tn,
                             preferred_element_type=F32) for hh in range(ha_ref.shape[1]))
    merged = sa_ref[0] * pa \
        + sb_ref[0] * jnp.dot(hb_ref[0], wpb_ref[...], preferred_element_type=F32)
    out = jnp.dot(merged.astype(BF16), wout_ref[...], preferred_element_type=F32)
    x1 = x + mod_ref[0, 2:3, :] * out
    h2 = _rms_mod(x1, nw_ref[...], mod_ref[0, 3:4, :], mod_ref[0, 4:5, :])
    tm = h2.shape[0]
    for sl in range(h2.shape[1] // LANES):
        hp_ref[0, pl.ds(sl, tm, stride=SUBLANES), :] = h2[:, sl * LANES:(sl + 1) * LANES]
        x1_ref[0, pl.ds(sl, tm, stride=SUBLANES), :] = x1[:, sl * LANES:(sl + 1) * LANES]
    h2h, h2m, _ = _split3(h2)
    logits = jnp.dot(h2h, wrh_ref[...], preferred_element_type=F32) \
        + jnp.dot(h2m, wrh_ref[...], preferred_element_type=F32) \
        + jnp.dot(h2h, wrm_ref[...], preferred_element_type=F32)
    lt = logits.T[0:N_EXPERTS, :] + br_ref[...]
    mx = jnp.max(lt, axis=0, keepdims=True)
    ex = jnp.exp(lt - mx)
    aff_ref[0] = ex / jnp.sum(ex, axis=0, keepdims=True)


def _merge(x, pos, ha, hb, sa, sb, mod, mod_row_fn, nw, wts, tm):
    B, T, D = x.shape
    has_pos = pos is not None
    wpa, wpb, wout, wrh, wrm, br = wts
    assert D == SUBLANES * LANES, D
    const = lambda shape: pl.BlockSpec(shape, lambda b, i: (0,) * len(shape),
                                       pipeline_mode=pl.Buffered(1))
    tok = lambda width: pl.BlockSpec((1, tm, width), lambda b, i: (b, i, 0))
    in_specs = [tok(D)]
    args = [x]
    if has_pos:
        in_specs.append(pl.BlockSpec((tm, D), lambda b, i: (i, 0)))
        args.append(pos)
    assert ha.shape[-1] == tm, (ha.shape, tm)
    in_specs += [pl.BlockSpec((1,) + ha.shape[1:2] + (1,) + ha.shape[3:], lambda b, i: (b, 0, i, 0, 0)),
                 tok(hb.shape[-1]), tok(D), tok(D),
                 pl.BlockSpec((1, N_MOD, D), lambda b, i: (mod_row_fn(b), 0, 0)),
                 const((1, D)), const(wpa.shape), const(wpb.shape), const(wout.shape),
                 const(wrh.shape), const(wrm.shape), const(br.shape)]
    args += [ha, hb, sa, sb, mod, nw, wpa, wpb, wout, wrh, wrm, br]
    return pl.pallas_call(
        functools.partial(_merge_kernel, has_pos),
        grid=(B, T // tm),
        in_specs=in_specs,
        out_specs=[pl.BlockSpec((1, tm * SUBLANES, LANES), lambda b, i: (b, i, 0)),
                   pl.BlockSpec((1, tm * SUBLANES, LANES), lambda b, i: (b, i, 0)),
                   pl.BlockSpec((1, N_EXPERTS, tm), lambda b, i: (b, 0, i))],
        out_shape=[jax.ShapeDtypeStruct((B, T * SUBLANES, LANES), F32),
                   jax.ShapeDtypeStruct((B, T * SUBLANES, LANES), F32),
                   jax.ShapeDtypeStruct((B, N_EXPERTS, T), F32)],
        compiler_params=_cparams(("arbitrary", "arbitrary")),
        name="merge",
    )(*args)


def _route_kernel(cap, aff_ref, idx_ref, val_ref, jb_s, jl_s):
    aff = aff_ref[0]
    E, T = aff.shape
    thr_bits = jnp.zeros((E, 1), jnp.int32)
    for bit in range(30, -1, -1):
        cand = thr_bits | (1 << bit)
        cnt = jnp.sum((aff >= lax.bitcast_convert_type(cand, F32)).astype(F32), axis=1, keepdims=True)
        thr_bits = jnp.where(cnt >= cap, cand, thr_bits)
    thr = lax.bitcast_convert_type(thr_bits, F32)
    nxt = lax.bitcast_convert_type(thr_bits + 1, F32)
    gt = aff >= nxt
    eq = (aff >= thr) & jnp.logical_not(gt)
    need = cap - jnp.sum(gt.astype(F32), axis=1, keepdims=True)

    r = lax.broadcasted_iota(jnp.int32, (LANES, LANES), 0)
    c = lax.broadcasted_iota(jnp.int32, (LANES, LANES), 1)
    tri = (r < c).astype(BF16)

    def excl_cumsum(mk):
        outs = []
        carry = jnp.zeros((E, 1), F32)
        for kb in range(T // LANES):
            blk = mk[:, kb * LANES:(kb + 1) * LANES]
            outs.append(jnp.dot(blk.astype(BF16), tri, preferred_element_type=F32) + carry)
            carry = carry + jnp.sum(blk.astype(F32), axis=1, keepdims=True)
        return jnp.concatenate(outs, axis=1)

    tie_rank = excl_cumsum(eq)
    mask = gt | (eq & (tie_rank < need))
    slot = jnp.where(mask, excl_cumsum(mask), -1.0).astype(jnp.int32)
    JL = min(cap, LANES)
    njb = cap // JL
    NC = 5
    WR = -(-(njb * NC) // SUBLANES) * SUBLANES
    jb_s[...] = slot >> (JL.bit_length() - 1)
    jl_s[...] = slot & (JL - 1)
    tio = lax.broadcasted_iota(jnp.int32, (1, T), 1)
    t_hi = (tio >> 6).astype(F32)
    t_lo = (tio & 63).astype(F32)
    jl_col = lax.broadcasted_iota(jnp.int32, (JL, 1), 0)
    wrow = lax.broadcasted_iota(jnp.int32, (WR, 1), 0)
    wjb = jnp.full((WR, 1), -2, jnp.int32)
    for jb in range(njb):
        wjb = jnp.where((wrow >= NC * jb) & (wrow < NC * (jb + 1)), jb, wjb)
    wcomp = wrow - NC * wjb
    lane = lax.broadcasted_iota(jnp.int32, (JL, E), 1)
    nt = (((1,), (1,)), ((), ()))

    def per_expert(e, carry):
        a1, a2, a3 = _split3(aff_ref[0, pl.ds(e, 1), :])
        comp = jnp.where(wcomp == 0, t_hi,
                         jnp.where(wcomp == 1, t_lo,
                                   jnp.where(wcomp == 2, a1.astype(F32),
                                             jnp.where(wcomp == 3, a2.astype(F32), a3.astype(F32)))))
        wt = jnp.where(jb_s[pl.ds(e, 1), :] == wjb, comp, 0.0).astype(BF16)
        onehot = jnp.where(jl_s[pl.ds(e, 1), :] == jl_col, 1.0, 0.0).astype(BF16)
        r = lax.dot_general(onehot, wt, nt, preferred_element_type=F32)
        out = []
        for jb in range(njb):
            c0 = jb * NC
            ie = r[:, c0:c0 + 1] * 64.0 + r[:, c0 + 1:c0 + 2]
            ve = r[:, c0 + 2:c0 + 3] + r[:, c0 + 3:c0 + 4] + r[:, c0 + 4:c0 + 5]
            out.append(jnp.where(lane == e, ie, carry[2 * jb]))
            out.append(jnp.where(lane == e, ve, carry[2 * jb + 1]))
        return tuple(out)

    res = lax.fori_loop(0, E, per_expert, tuple(jnp.zeros((JL, E), F32) for _ in range(2 * njb)),
                        unroll=2)
    for jb in range(njb):
        idx_ref[0, jb * JL:(jb + 1) * JL, :] = res[2 * jb].astype(jnp.int32)
        val_ref[0, jb * JL:(jb + 1) * JL, :] = res[2 * jb + 1]


def _route(aff, cap):
    B, E0, T = aff.shape
    R = max(1, min(B, ROUTE_ROW_ELEMS // (E0 * T)))
    E = R * E0
    idx, val = pl.pallas_call(
        functools.partial(_route_kernel, cap),
        grid=(B // R,),
        in_specs=[pl.BlockSpec((1, E, T), lambda b: (b, 0, 0))],
        out_specs=[pl.BlockSpec((1, cap, E), lambda b: (b, 0, 0)),
                   pl.BlockSpec((1, cap, E), lambda b: (b, 0, 0))],
        out_shape=[jax.ShapeDtypeStruct((B // R, cap, E), jnp.int32),
                   jax.ShapeDtypeStruct((B // R, cap, E), F32)],
        scratch_shapes=[pltpu.VMEM((E, T), jnp.int32), pltpu.VMEM((E, T), jnp.int32)],
        compiler_params=_cparams(("arbitrary",)),
        name="route",
    )(aff.reshape(B // R, E, T))
    unbatch = lambda a: a.reshape(B // R, cap, R, E0).transpose(0, 2, 1, 3).reshape(B, cap, E0)
    return unbatch(idx), unbatch(val)


def _moe_kernel(idx_ref, idxn_ref, val_ref, valn_ref, tok_ref, g2_ref, nw_ref, w1_ref, w3_ref, w2_ref, x1_ref, out_ref,
                xe_s, xb_s, ye_s, yt_s, acc_s, yb_s, sem_in, sem_out):
    b, e, f = pl.program_id(0), pl.program_id(1), pl.program_id(2)
    ne = pl.num_programs(1)
    cap = ye_s.shape[0]
    S = SUBLANES
    GR = 8

    def tile(ref, r):
        return ref.at[pl.ds(pl.multiple_of(r * S, S), S), :]

    def gather_group(iref, g):
        rows = [iref[0, 0, g * GR + i] for i in range(GR)]
        vals = [tile(tok_ref.at[0], rows[i])[...] for i in range(GR)]
        for i in range(GR):
            tile(xe_s, g * GR + i)[...] = vals[i]

    def scatter_group(iref, vref, g):
        rows = [iref[0, 0, g * GR + i] for i in range(GR)]
        vals = [tile(acc_s, rows[i])[...] + tile(yt_s, g * GR + i)[...] * vref[0, 0, g * GR + i]
                for i in range(GR)]
        for i in range(GR):
            tile(acc_s, rows[i])[...] = vals[i]

    def ffn_half():
        xe = xb_s[...]
        h1 = jnp.dot(xe, w1_ref[0], preferred_element_type=F32)
        h3 = jnp.dot(xe, w3_ref[0], preferred_element_type=F32)
        hid = (h1 * _sigmoid(h1) * h3).astype(BF16)
        return jnp.dot(hid, w2_ref[0], preferred_element_type=F32)

    @pl.when((e == 0) & (f == 0))
    def _():
        cp_in = pltpu.make_async_copy(x1_ref.at[b], acc_s, sem_in)
        cp_in.start()
        yt_s[...] = jnp.zeros_like(yt_s)
        lax.fori_loop(0, cap // GR, lambda g, c: (gather_group(idx_ref, g), c)[1], 0)
        cp_in.wait()

    @pl.when(f == 0)
    def _():
        xb_s[...] = jnp.concatenate([xe_s[pl.ds(sl, cap, stride=S), :] for sl in range(S)],
                                    axis=1).astype(BF16)
        ye_s[...] = ffn_half()
        for g in range(cap // GR):
            scatter_group(idxn_ref, valn_ref, g)

    @pl.when(f == 1)
    def _():
        yw = (ye_s[...] + ffn_half()) * g2_ref[0]
        for sl in range(S):
            yt_s[pl.ds(sl, cap, stride=S), :] = yw[:, sl * LANES:(sl + 1) * LANES]
        for g in range(cap // GR):
            gather_group(idxn_ref, g)

    @pl.when((e == ne - 1) & (f == 1))
    def _():
        lax.fori_loop(0, cap // GR, lambda g, c: (scatter_group(idx_ref, val_ref, g), c)[1], 0)
        ch = yb_s.shape[1]
        nch = acc_s.shape[0] // (ch * S)
        copies = []
        for c in range(nch):
            slot = c % 2
            if c >= 2:
                copies[c - 2].wait()
            x = jnp.concatenate([acc_s[pl.ds(c * ch * S + sl, ch, stride=S), :] for sl in range(S)], axis=1)
            ms = jnp.mean(x * x, axis=-1, keepdims=True)
            yb_s[slot] = x * lax.rsqrt(ms + EPS) * nw_ref[...]
            cp = pltpu.make_async_copy(yb_s.at[slot], out_ref.at[b, pl.ds(c * ch, ch), :], sem_out.at[slot])
            cp.start()
            copies.append(cp)
        for c in range(max(nch - 2, 0), nch):
            copies[c].wait()


def _moe(idx, tok, x1, val, gate2, nw, w1, w3, w2):
    B, NR, _ = tok.shape
    E, D, FF = w1.shape
    cap = idx.shape[-1]
    nf = EXPERT_FF_SPLIT
    assert nf == 2, nf
    fh = FF // nf
    NT = NR // SUBLANES
    ch = min(MOE_OUT_CHUNK, NT)
    own = pl.BlockSpec((1, 1, cap), lambda b, e, f: (b * E + e, 0, 0), memory_space=pltpu.SMEM)
    nbr = pl.BlockSpec((1, 1, cap), lambda b, e, f: (b * E + jnp.clip(e - 1 + 2 * f, 0, E - 1), 0, 0),
                       memory_space=pltpu.SMEM)
    return pl.pallas_call(
        _moe_kernel,
        grid=(B, E, nf),
        in_specs=[own, nbr, own, nbr,
                  pl.BlockSpec((1, NR, LANES), lambda b, e, f: (b, 0, 0), pipeline_mode=pl.Buffered(1)),
                  pl.BlockSpec((1, 1, D), lambda b, e, f: (b, 0, 0)),
                  pl.BlockSpec((1, D), lambda b, e, f: (0, 0)),
                  pl.BlockSpec((1, D, fh), lambda b, e, f: (e, 0, f)),
                  pl.BlockSpec((1, D, fh), lambda b, e, f: (e, 0, f)),
                  pl.BlockSpec((1, fh, D), lambda b, e, f: (e, f, 0)),
                  pl.BlockSpec(memory_space=pl.ANY)],
        out_specs=pl.BlockSpec(memory_space=pl.ANY),
        out_shape=jax.ShapeDtypeStruct((B, NT, D), F32),
        scratch_shapes=[pltpu.VMEM((cap * SUBLANES, LANES), F32), pltpu.VMEM((cap, D), BF16),
                        pltpu.VMEM((cap, D), F32), pltpu.VMEM((cap * SUBLANES, LANES), F32),
                        pltpu.VMEM((NR, LANES), F32), pltpu.VMEM((2, ch, D), F32),
                        pltpu.SemaphoreType.DMA(()), pltpu.SemaphoreType.DMA((2,))],
        compiler_params=_cparams(("arbitrary", "arbitrary", "arbitrary")),
        name="moe",
    )(idx, idx, val, val, tok, gate2, nw, w1, w3, w2, x1)


def _pos2d(T, D):
    rows = T // GRID_W
    quarter = D // 4
    freqs = 1.0 / (10000.0 ** (jnp.arange(quarter, dtype=F32) / quarter))
    er = jnp.arange(rows, dtype=F32)[:, None] * freqs
    ec = jnp.arange(GRID_W, dtype=F32)[:, None] * freqs
    rep = lambda a: jnp.repeat(a, GRID_W, axis=0)
    til = lambda a: jnp.tile(a, (rows, 1))
    return jnp.concatenate([rep(jnp.sin(er)), rep(jnp.cos(er)), til(jnp.sin(ec)), til(jnp.cos(ec))], axis=-1)


def _stream(x, pos, mod, mod_row_fn, P, mstate, s5_x0, tm):
    B, T, D = x.shape
    L = min(MLSTM_CHUNK, T)
    q, k, v, o, gc, gr, gtot, u, sa, sb = _inproj(x, pos, mod, mod_row_fn, P['norm_mix'],
                                                   P['w_in_parts'], tm, L)
    res = _mlstm(q, k, v, o, gc, gr, gtot, P['head_norm'], mstate, L)
    ha, mout = res[0], res[1:]
    Lc = min(S5_CHUNK, T)
    yf, xf = _s5_pass(False, u, None, s5_x0[0], *P['s5_dir'][0], None, Lc)
    hb, xb = _s5_pass(True, u, yf, s5_x0[1], *P['s5_dir'][1], P['glu'], Lc)
    x1, hp, aff = _merge(x, pos, ha, hb, sa, sb, mod, mod_row_fn, P['norm_ffn'], P['merge_w'], tm)
    cap = CAPACITY_FACTOR * T // N_EXPERTS
    idx, val = _route(aff, cap)
    E = N_EXPERTS
    gb = max(1, min(B, MOE_TABLE_ROWS // T))
    if any(mod_row_fn(r) != mod_row_fn(r - r % gb) for r in range(B)):
        gb = 1
    nb = B // gb
    offs = (jnp.arange(B, dtype=jnp.int32) % gb) * T
    idx_f = (idx + offs[:, None, None]).reshape(nb, gb, cap, E).transpose(0, 3, 1, 2).reshape(nb * E, 1, gb * cap)
    val_f = val.reshape(nb, gb, cap, E).transpose(0, 3, 1, 2).reshape(nb * E, 1, gb * cap)
    gate2 = jnp.stack([mod[mod_row_fn(t * gb), N_MOD - 1] for t in range(nb)])[:, None, :]
    tiles = lambda a: a.reshape(nb, gb * T * SUBLANES, LANES)
    y = _moe(idx_f, tiles(hp), tiles(x1), val_f, gate2, P['norm_final'], *P['experts']).reshape(B, T, D)
    return y, mout, (xf, xb)


def kernel(x_prompt, x_sample, state_mlstm_C, state_mlstm_n, state_mlstm_m, state_s5_re, state_s5_im,
           c, c_ctx, w_ada, b_ada, norm_mix, norm_ffn, w_in, b_gate, head_norm,
           s5_a_re, s5_a_im, s5_log_dt, s5_b_re, s5_b_im, s5_c_re, s5_c_im, s5_d, w_glu, b_glu,
           w_pA, w_pB, w_out, w_router, b_router, w_e1, w_e3, w_e2, norm_final):
    Bp, Tp, D = x_prompt.shape
    Bs, Ts, _ = x_sample.shape
    depth = w_ada.shape[0]
    H, dh = MLSTM_HEADS, MLSTM_HEAD_DIM
    W = H * dh
    G, Pn = s5_a_re.shape[2], s5_a_re.shape[3]
    SW = G * S5_GROUP

    assert depth == 1, depth
    l = 0
    pos = _pos2d(Ts, D)
    nrow = Bs + 1
    rows = -(-nrow // SUBLANES) * SUBLANES
    cpad = jnp.zeros((rows, D), F32).at[0:Bs].set(c).at[Bs].set(c_ctx)
    mod = _adaln(cpad, w_ada[l], b_ada[l]).reshape(rows, N_MOD, D)

    wi = w_in[l]
    gcols = wi[:, 4 * W:4 * W + 4 * H].reshape(D, 4, H).transpose(0, 2, 1).reshape(D, 4 * H)
    wug = jnp.concatenate([wi[:, 4 * W + 4 * H:4 * W + 4 * H + SW], gcols,
                           jnp.zeros((D, LANES - 4 * H), F32)], axis=1).astype(BF16)
    bg = jnp.concatenate([b_gate[l].T.reshape(1, 4 * H), jnp.zeros((1, LANES - 4 * H), F32)], axis=1)
    w_in_parts = (wi[:, 0:3 * W].astype(BF16), wi[:, 3 * W:4 * W].astype(BF16), wug,
                  wi[:, 4 * W + 4 * H + SW:].astype(BF16), bg)
    s5_dir = [_s5_discretise(s5_a_re[l, d], s5_a_im[l, d], s5_log_dt[l, d], s5_b_re[l], s5_b_im[l],
                             s5_c_re[l, d], s5_c_im[l, d]) for d in range(2)]
    s5_dir = [(jnp.broadcast_to(a, (SUBLANES, a.shape[1])), jnp.broadcast_to(s, (SUBLANES, s.shape[1])), bw, cw)
              for (a, s, bw, cw) in s5_dir]
    wr = jnp.concatenate([w_router[l], jnp.zeros((D, LANES - N_EXPERTS), F32)], axis=1)
    wr_hi = wr.astype(BF16)
    wr_mid = (wr - wr_hi.astype(F32)).astype(BF16)
    P = {
        'norm_mix': norm_mix[l].reshape(1, D), 'norm_ffn': norm_ffn[l].reshape(1, D),
        'norm_final': norm_final.reshape(1, D),
        'w_in_parts': w_in_parts, 'head_norm': head_norm[l], 's5_dir': s5_dir,
        'glu': (s5_d[l].reshape(1, SW), w_glu[l].astype(BF16), b_glu[l].reshape(1, SW)),
        'merge_w': (w_pA[l].astype(BF16), w_pB[l].astype(BF16), w_out[l].astype(BF16), wr_hi, wr_mid,
                    b_router[l].reshape(N_EXPERTS, 1)),
        'experts': (w_e1[l].astype(BF16), w_e3[l].astype(BF16), w_e2[l].astype(BF16)),
    }
    zero_x0 = jnp.zeros((Bp // SUBLANES, SUBLANES, 2 * G * Pn), F32)
    yp, mout, (xf, xb) = _stream(x_prompt, None, mod, lambda b: Bs, P, None, (zero_x0, zero_x0),
                                 min(MLSTM_CHUNK, Tp))
    lat_x0 = [_s5_state_to_lanes(state_s5_re[:, l, d], state_s5_im[:, l, d]) for d in range(2)]
    ys, _, _ = _stream(x_sample, pos, mod, lambda b: b, P,
                       (state_mlstm_C[:, l], state_mlstm_n[:, l], state_mlstm_m[:, l]), lat_x0,
                       min(MLSTM_CHUNK, Ts))
    Cn, nn, mn = mout
    fre, fim = _s5_lanes_to_state(xf, G, Pn)
    bre, bim = _s5_lanes_to_state(xb, G, Pn)
    outs = (Cn, nn.reshape(Bp, 2, H, dh), mn.reshape(Bp, 2, H),
            jnp.stack([fre, bre], axis=1), jnp.stack([fim, bim], axis=1))
    out_C, out_n, out_m, out_re, out_im = (o[:, None] for o in outs)
    return (yp, ys, out_C, out_n, out_m, out_re, out_im)
```

```python
import functools

import jax
import jax.numpy as jnp
from jax import lax
from jax.experimental import pallas as pl
from jax.experimental.pallas import tpu as pltpu

F32 = jnp.float32
BF16 = jnp.bfloat16
EPS = 1e-6

MLSTM_HEADS = 4
MLSTM_HEAD_DIM = 256
N_EXPERTS = 16
CAPACITY_FACTOR = 2
S5_GROUP = 16
S5_STATE = 64
GRID_W = 64
N_MOD = 6

LANES = 128
SUBLANES = 8
VMEM_LIMIT_BYTES = 60000 * 1024

MLSTM_CHUNK = 256
S5_CHUNK = 128
S5_LANE_GROUPS = 4
S5_SEGMENTS = 4
EXPERT_FF_SPLIT = 2
ROUTE_ROW_ELEMS = 16 * 4096
MOE_ROW_GROUP = 8
MOE_OUT_CHUNK = 256
MOE_TABLE_ROWS = 4096


def _cparams(sem):
    return pltpu.CompilerParams(dimension_semantics=sem, vmem_limit_bytes=VMEM_LIMIT_BYTES)


def _sigmoid(x):
    return jax.nn.sigmoid(x)


def _log_sigmoid(x):
    return jnp.minimum(x, 0.0) - jnp.log1p(jnp.exp(-jnp.abs(x)))


def _adaln_kernel(c_ref, w_ref, b_ref, o_ref):
    c = c_ref[...]
    s = c * _sigmoid(c)
    o_ref[...] = jnp.dot(s, w_ref[...], precision=lax.Precision.HIGHEST,
                         preferred_element_type=F32) + b_ref[...]


def _adaln(cpad, w, b):
    rows, d = cpad.shape
    n = w.shape[1]
    tn = 1536
    return pl.pallas_call(
        _adaln_kernel,
        grid=(n // tn,),
        in_specs=[pl.BlockSpec((rows, d), lambda j: (0, 0)),
                  pl.BlockSpec((d, tn), lambda j: (0, j)),
                  pl.BlockSpec((1, tn), lambda j: (0, j))],
        out_specs=pl.BlockSpec((rows, tn), lambda j: (0, j)),
        out_shape=jax.ShapeDtypeStruct((rows, n), F32),
        compiler_params=_cparams(("arbitrary",)),
        name="adaln",
    )(cpad, w, b.reshape(1, n))


def _rms_mod(x, nw, shift, scale):
    ms = jnp.mean(x * x, axis=-1, keepdims=True)
    y = x * lax.rsqrt(ms + EPS) * nw
    return y * (1.0 + scale) + shift


def _split3(x):
    hi = x.astype(BF16)
    r = x - hi.astype(F32)
    mid = r.astype(BF16)
    lo = (r - mid.astype(F32)).astype(BF16)
    return hi, mid, lo


def _inproj_kernel(has_pos, *refs):
    if has_pos:
        x_ref, pos_ref = refs[0], refs[1]
        refs = refs[2:]
    else:
        x_ref, pos_ref = refs[0], None
        refs = refs[1:]
    (mod_ref, nw_ref, wqkv_ref, wo_ref, wug_ref, wab_ref, bg_ref,
     q_ref, k_ref, v_ref, o_ref, gc_ref, gr_ref, gtot_ref, u_ref, sa_ref, sb_ref) = refs
    x = x_ref[0]
    if has_pos:
        x = x + pos_ref[...]
    h = _rms_mod(x, nw_ref[...], mod_ref[0, 0:1, :], mod_ref[0, 1:2, :])
    hb = h.astype(BF16)
    w = MLSTM_HEADS * MLSTM_HEAD_DIM
    qkv = jnp.dot(hb, wqkv_ref[...], preferred_element_type=F32)
    q_ref[0] = (qkv[:, 0:w] * (MLSTM_HEAD_DIM ** -0.5)).astype(BF16)
    k_ref[0] = qkv[:, w:2 * w].astype(BF16)
    vt = qkv[:, 2 * w:3 * w].T.astype(BF16)
    for hh in range(MLSTM_HEADS):
        v_ref[0, hh, 0] = vt[hh * MLSTM_HEAD_DIM:(hh + 1) * MLSTM_HEAD_DIM, :]
    ot = jnp.dot(hb, wo_ref[...], preferred_element_type=F32).T
    for hh in range(MLSTM_HEADS):
        o_ref[0, hh, 0] = ot[hh * MLSTM_HEAD_DIM:(hh + 1) * MLSTM_HEAD_DIM, :]
    ug = jnp.dot(hb, wug_ref[...], preferred_element_type=F32)
    su = u_ref.shape[-1]
    u_ref[0] = ug[:, 0:su]
    g = ug[:, su:su + LANES] + bg_ref[...]
    col = lax.broadcasted_iota(jnp.int32, g.shape, 1)
    g = jnp.where((col % 2) == 1, _log_sigmoid(g), g)
    tm = g.shape[0]
    tril = (lax.broadcasted_iota(jnp.int32, (tm, tm), 1)
            <= lax.broadcasted_iota(jnp.int32, (tm, tm), 0)).astype(BF16)
    cs = sum(jnp.dot(tril, part, preferred_element_type=F32) for part in _split3(g))
    tot = cs[tm - 1:tm, :]
    cum = jnp.where((col % 4) < 2, cs, tot - cs + g)
    cum_l = pltpu.roll(cum, LANES - 1, axis=1)
    tot_l = pltpu.roll(jnp.broadcast_to(tot, g.shape), LANES - 1, axis=1)
    r = g - cum_l
    cumt = cum.T
    ggt = (tot_l - cum_l + g).T
    for hh in range(MLSTM_HEADS):
        gc_ref[0, hh] = r[:, 4 * hh:4 * hh + 4]
        gr_ref[0, hh, 0, 0:4, :] = cumt[4 * hh:4 * hh + 4, :]
        gr_ref[0, hh, 0, 4:8, :] = ggt[4 * hh:4 * hh + 4, :]
        gtot_ref[0, hh, 0] = tot[:, 4 * hh:4 * hh + 4]
    ab = jnp.dot(hb, wab_ref[...], preferred_element_type=F32)
    d = sa_ref.shape[-1]
    sa_ref[0] = _sigmoid(ab[:, 0:d])
    sb_ref[0] = _sigmoid(ab[:, d:2 * d])


def _inproj(x, pos, mod, mod_row_fn, nw, wts, tm, L):
    B, T, D = x.shape
    has_pos = pos is not None
    wqkv, wo, wug, wab, bg = wts
    su = wug.shape[1] - LANES
    assert tm == L, (tm, L)
    const = lambda shape: pl.BlockSpec(shape, lambda b, i: (0,) * len(shape),
                                       pipeline_mode=pl.Buffered(1))
    in_specs = [pl.BlockSpec((1, tm, D), lambda b, i: (b, i, 0))]
    args = [x]
    if has_pos:
        in_specs.append(pl.BlockSpec((tm, D), lambda b, i: (i, 0)))
        args.append(pos)
    in_specs += [pl.BlockSpec((1, N_MOD, D), lambda b, i: (mod_row_fn(b), 0, 0)),
                 const((1, D)), const(wqkv.shape), const(wo.shape), const(wug.shape),
                 const(wab.shape), const((1, LANES))]
    args += [mod, nw, wqkv, wo, wug, wab, bg]
    W = MLSTM_HEADS * MLSTM_HEAD_DIM
    tok = lambda width, dt: (jax.ShapeDtypeStruct((B, T, width), dt),
                             pl.BlockSpec((1, tm, width), lambda b, i: (b, i, 0)))
    outs = [tok(W, BF16), tok(W, BF16),
            (jax.ShapeDtypeStruct((B, MLSTM_HEADS, T // L, MLSTM_HEAD_DIM, L), BF16),
             pl.BlockSpec((1, MLSTM_HEADS, 1, MLSTM_HEAD_DIM, L), lambda b, i: (b, 0, i, 0, 0))),
            (jax.ShapeDtypeStruct((B, MLSTM_HEADS, T // L, MLSTM_HEAD_DIM, L), F32),
             pl.BlockSpec((1, MLSTM_HEADS, 1, MLSTM_HEAD_DIM, L), lambda b, i: (b, 0, i, 0, 0))),
            (jax.ShapeDtypeStruct((B, MLSTM_HEADS, T, 4), F32),
             pl.BlockSpec((1, MLSTM_HEADS, tm, 4), lambda b, i: (b, 0, i, 0))),
            (jax.ShapeDtypeStruct((B, MLSTM_HEADS, T // L, 8, L), F32),
             pl.BlockSpec((1, MLSTM_HEADS, 1, 8, L), lambda b, i: (b, 0, i, 0, 0))),
            (jax.ShapeDtypeStruct((B, MLSTM_HEADS, T // L, 1, 4), F32),
             pl.BlockSpec((1, MLSTM_HEADS, 1, 1, 4), lambda b, i: (b, 0, i, 0, 0))),
            tok(su, F32), tok(D, F32), tok(D, F32)]
    return pl.pallas_call(
        functools.partial(_inproj_kernel, has_pos),
        grid=(B, T // tm),
        in_specs=in_specs,
        out_specs=[o[1] for o in outs],
        out_shape=[o[0] for o in outs],
        compiler_params=_cparams(("arbitrary", "arbitrary")),
        name="inproj",
    )(*args)


MLSTM_AUG_ROWS = 16


def _mlstm_chunk(q, k, vt_aug, r_col, b_row, gg_row, total, C_ref, m_ref, d, valid_t):
    dh = q.shape[1]
    Ca = C_ref[d]
    m = m_ref[d]
    nt = (((1,), (1,)), ((), ()))
    rmask = jnp.where(valid_t, r_col, -jnp.inf)
    mx = jnp.maximum(m, jnp.max(rmask, axis=0, keepdims=True))
    p = jnp.exp(rmask - mx)
    w_inter = jnp.exp(m - mx)
    st = lax.dot_general(k, q, nt, preferred_element_type=F32) * p
    na = w_inter * lax.dot_general(Ca.astype(BF16), q, nt, preferred_element_type=F32) \
        + jnp.dot(vt_aug, st.astype(BF16), preferred_element_type=F32)
    rden = 1.0 / jnp.maximum(jnp.abs(na[dh:dh + 1, :]), jnp.exp(-(b_row + mx)))
    ht = na[0:dh, :] * rden
    m_new = jnp.maximum(total + m, jnp.max(gg_row, axis=1, keepdims=True))
    decay = jnp.exp(total + m - m_new)
    vw = (vt_aug.astype(F32) * jnp.exp(gg_row - m_new)).astype(BF16)
    C_ref[d] = decay * Ca + jnp.dot(vw, k, preferred_element_type=F32)
    m_ref[d] = m_new
    return ht


def _mlstm_kernel(zero_init, L, *refs):
    T, dh = refs[0].shape[1], refs[0].shape[2]
    if zero_init:
        (q_ref, k_ref, v_ref, o_ref, gc_ref, gr_ref, gtot_ref, hn_ref,
         out_ref, Co_ref, no_ref, mo_ref, C_s, m_s, hf_s, hb_s) = refs
        C_s[...] = jnp.zeros_like(C_s)
        m_s[...] = jnp.zeros_like(m_s)
    else:
        (q_ref, k_ref, v_ref, o_ref, gc_ref, gr_ref, gtot_ref, hn_ref, C0_ref, n0_ref, m0_ref,
         out_ref, C_s, m_s, hf_s, hb_s) = refs
        for d in range(2):
            C_s[d, 0:dh, :] = C0_ref[0, d, 0].T
            C_s[d, dh:dh + MLSTM_AUG_ROWS, :] = jnp.broadcast_to(n0_ref[0, d, 0], (MLSTM_AUG_ROWS, dh))
        m_s[...] = m0_ref[0, :, 0]
    nc = T // L
    row = lax.broadcasted_iota(jnp.int32, (L, L), 0)
    col = lax.broadcasted_iota(jnp.int32, (L, L), 1)
    masks_t = (row <= col, row >= col)
    ones = jnp.ones((MLSTM_AUG_ROWS, L), BF16)

    def body(c, carry):
        for d in range(2):
            cc = c if d == 0 else nc - 1 - c
            r0 = pl.multiple_of(cc * L, L)
            q = q_ref[0, pl.ds(r0, L), :]
            k = k_ref[0, pl.ds(r0, L), :]
            vt_aug = jnp.concatenate([v_ref[0, 0, cc], ones], axis=0)
            gc = gc_ref[0, 0, pl.ds(r0, L), :]
            gr = gr_ref[0, 0, cc]
            tot = gtot_ref[0, 0, cc]
            ht = _mlstm_chunk(q, k, vt_aug, gc[:, 2 * d:2 * d + 1], gr[2 * d + 1:2 * d + 2, :],
                              gr[4 + 2 * d:5 + 2 * d, :], tot[:, 2 * d + 1:2 * d + 2],
                              C_s, m_s, d, masks_t[d])
            (hf_s if d == 0 else hb_s)[cc] = ht
        return carry

    lax.fori_loop(0, nc, body, 0, unroll=4 if nc % 4 == 0 else 1)

    hn = jnp.concatenate([hn_ref[0]] * (L // LANES), axis=1)

    def fin(c, carry):
        hm = (hf_s[c] + hb_s[c]) * _sigmoid(o_ref[0, 0, c])
        hm = hm * lax.rsqrt(jnp.mean(hm * hm, axis=0, keepdims=True) + EPS) * hn
        out_ref[0, 0, c] = hm.astype(out_ref.dtype)
        return carry

    lax.fori_loop(0, nc, fin, 0)
    if zero_init:
        for d in range(2):
            Co_ref[0, d, 0] = C_s[d, 0:dh, :].T
        no_ref[0, :, 0] = C_s[:, dh:dh + 1, :]
        mo_ref[0, :, 0] = m_s[...]


def _mlstm(q, k, v, o, gc, gr, gtot, head_norm, state, L):
    B, T, W = q.shape
    H, dh = MLSTM_HEADS, MLSTM_HEAD_DIM
    nc = T // L
    zero_init = state is None
    bh = lambda width: pl.BlockSpec((1, T, width), lambda b, h: (b, 0, h))
    per_head_t = pl.BlockSpec((1, 1, nc, dh, L), lambda b, h: (b, h, 0, 0, 0))
    in_specs = [bh(dh), bh(dh), per_head_t, per_head_t,
                pl.BlockSpec((1, 1, T, 4), lambda b, h: (b, h, 0, 0)),
                pl.BlockSpec((1, 1, nc, 8, L), lambda b, h: (b, h, 0, 0, 0)),
                pl.BlockSpec((1, 1, nc, 1, 4), lambda b, h: (b, h, 0, 0, 0)),
                pl.BlockSpec((1, dh, LANES), lambda b, h: (h, 0, 0))]
    hn_cols = jnp.broadcast_to(head_norm.reshape(H, dh, 1), (H, dh, LANES))
    args = [q, k, v, o, gc, gr, gtot, hn_cols]
    st_specs = [pl.BlockSpec((1, 2, 1, dh, dh), lambda b, h: (b, 0, h, 0, 0)),
                pl.BlockSpec((1, 2, 1, 1, dh), lambda b, h: (b, 0, h, 0, 0)),
                pl.BlockSpec((1, 2, 1, 1, 1), lambda b, h: (b, 0, h, 0, 0))]
    st_shapes = [jax.ShapeDtypeStruct((B, 2, H, dh, dh), F32),
                 jax.ShapeDtypeStruct((B, 2, H, 1, dh), F32),
                 jax.ShapeDtypeStruct((B, 2, H, 1, 1), F32)]
    out_specs = [per_head_t]
    out_shape = [jax.ShapeDtypeStruct((B, H, nc, dh, L), BF16)]
    if zero_init:
        out_specs += st_specs
        out_shape += st_shapes
    else:
        C0, n0, m0 = state
        in_specs += st_specs
        args += [C0, n0.reshape(B, 2, H, 1, dh), m0.reshape(B, 2, H, 1, 1)]
    return pl.pallas_call(
        functools.partial(_mlstm_kernel, zero_init, L),
        grid=(B, H),
        in_specs=in_specs,
        out_specs=out_specs,
        out_shape=out_shape,
        scratch_shapes=[pltpu.VMEM((2, dh + MLSTM_AUG_ROWS, dh), F32), pltpu.VMEM((2, 1, 1), F32),
                        pltpu.VMEM((nc, dh, L), F32), pltpu.VMEM((nc, dh, L), F32)],
        compiler_params=_cparams(("arbitrary", "arbitrary")),
        name="mlstm",
    )(*args)


def _s5_kernel(backward, Lc, *refs):
    if backward:
        (u_ref, yf_ref, x0_ref, ar_ref, as_ref, bw_ref, cw_ref, d_ref, wg_ref, bgl_ref,
         out_ref, xl_ref, X_s, st_s, U_s, H_s) = refs
    else:
        (u_ref, x0_ref, ar_ref, as_ref, bw_ref, cw_ref, out_ref, xl_ref, X_s, st_s, U_s) = refs
    ci = pl.program_id(1)
    nj = S5_LANE_GROUPS
    half = X_s.shape[1] // (2 * nj)
    nreq = u_ref.shape[0]

    @pl.when(ci == 0)
    def _():
        st_s[...] = x0_ref[0]

    for r in range(nreq):
        for j in range(nj):
            U_s[j, pl.ds(r, Lc, stride=nreq), :] = u_ref[r, :, j * LANES:(j + 1) * LANES]
    nseg = S5_SEGMENTS
    seg_t = Lc // nseg
    seg_order = range(nseg - 1, -1, -1) if backward else range(nseg)

    def rows_of(sg):
        return slice(sg * seg_t * SUBLANES, (sg + 1) * seg_t * SUBLANES)

    for sg in seg_order:
        for j in range(nj):
            X_s[rows_of(sg), 2 * half * j:2 * half * (j + 1)] = jnp.dot(
                U_s[j, rows_of(sg), :].astype(BF16), bw_ref[j], preferred_element_type=F32)

    def swap(x):
        parts = []
        for j in range(nj):
            parts.append(x[:, 2 * half * j + half:2 * half * (j + 1)])
            parts.append(x[:, 2 * half * j:2 * half * j + half])
        return jnp.concatenate(parts, axis=1)

    x = st_s[...]
    for sg in seg_order:
        ts = range(sg * seg_t, (sg + 1) * seg_t)
        for t in (reversed(ts) if backward else ts):
            rs = slice(t * SUBLANES, (t + 1) * SUBLANES)
            x = ar_ref[...] * x + as_ref[...] * swap(x) + X_s[rs, :]
            X_s[rs, :] = x
        y = jnp.concatenate(
            [jnp.dot(X_s[rows_of(sg), 2 * half * j:2 * half * (j + 1)].astype(BF16), cw_ref[j],
                     preferred_element_type=F32) for j in range(nj)], axis=1)
        if backward:
            u = jnp.concatenate([U_s[j, rows_of(sg), :] for j in range(nj)], axis=1)
            z = yf_ref[0, rows_of(sg), :] + y + d_ref[...] * u
            gate = _sigmoid(jnp.dot(z.astype(BF16), wg_ref[...], preferred_element_type=F32) + bgl_ref[...])
            hb = jax.nn.gelu(z) * gate
            for j in range(nj):
                H_s[j, rows_of(sg), :] = hb[:, j * LANES:(j + 1) * LANES]
        else:
            out_ref[0, rows_of(sg), :] = y
    st_s[...] = x
    xl_ref[0] = x
    if backward:
        for r in range(nreq):
            for j in range(nj):
                out_ref[r, :, j * LANES:(j + 1) * LANES] = \
                    H_s[j, pl.ds(r, Lc, stride=nreq), :].astype(out_ref.dtype)


def _s5_pass(backward, u, yf, x0, ar, asg, bw, cw, glu, Lc):
    B, T, su = u.shape
    G8 = B // SUBLANES
    rows = Lc * SUBLANES
    nchunk = T // Lc
    SL = x0.shape[-1]
    nj = S5_LANE_GROUPS
    assert su == nj * LANES, su
    cidx = (lambda c: nchunk - 1 - c) if backward else (lambda c: c)
    reqspec = pl.BlockSpec((SUBLANES, Lc, su), lambda g, c: (g, cidx(c), 0))
    rowspec = pl.BlockSpec((1, rows, su), lambda g, c: (g, cidx(c), 0))
    const = lambda shape: pl.BlockSpec(shape, lambda g, c: (0,) * len(shape))
    in_specs = [reqspec]
    args = [u]
    if backward:
        in_specs.append(rowspec)
        args.append(yf)
    in_specs += [pl.BlockSpec((1, SUBLANES, SL), lambda g, c: (g, 0, 0)),
                 const((SUBLANES, SL)), const((SUBLANES, SL)), const(bw.shape), const(cw.shape)]
    args += [x0, ar, asg, bw, cw]
    scratch = [pltpu.VMEM((rows, SL), F32), pltpu.VMEM((SUBLANES, SL), F32),
               pltpu.VMEM((nj, rows, LANES), F32)]
    if backward:
        d, wg, bgl = glu
        in_specs += [const((1, su)), const(wg.shape), const((1, su))]
        args += [d, wg, bgl]
        scratch.append(pltpu.VMEM((nj, rows, LANES), F32))
        out0 = (jax.ShapeDtypeStruct((B, T, su), BF16), reqspec)
    else:
        out0 = (jax.ShapeDtypeStruct((G8, T * SUBLANES, su), F32), rowspec)
    return pl.pallas_call(
        functools.partial(_s5_kernel, backward, Lc),
        grid=(G8, nchunk),
        in_specs=in_specs,
        out_specs=[out0[1], pl.BlockSpec((1, SUBLANES, SL), lambda g, c: (g, 0, 0))],
        out_shape=[out0[0], jax.ShapeDtypeStruct((G8, SUBLANES, SL), F32)],
        scratch_shapes=scratch,
        compiler_params=_cparams(("arbitrary", "arbitrary")),
        name="s5_bwd" if backward else "s5_fwd",
    )(*args)


def _s5_discretise(a_re, a_im, log_dt, b_re, b_im, c_re, c_im):
    G, P = a_re.shape
    nj = S5_LANE_GROUPS
    gl = G // nj
    dt = jnp.exp(log_dt)[:, None]
    e = jnp.exp(a_re * dt)
    ar = e * jnp.cos(a_im * dt)
    ai = e * jnp.sin(a_im * dt)
    den = a_re * a_re + a_im * a_im
    fr = ((ar - 1.0) * a_re + ai * a_im) / den
    fi = (ai * a_re - (ar - 1.0) * a_im) / den
    bbr = fr[..., None] * b_re - fi[..., None] * b_im
    bbi = fr[..., None] * b_im + fi[..., None] * b_re

    def lanes(re, im):
        return jnp.stack([re.reshape(nj, gl * P), im.reshape(nj, gl * P)], axis=1).reshape(1, -1)

    a_row = lanes(ar, ar)
    as_row = lanes(-ai, ai)
    eye = jnp.eye(gl, dtype=F32)
    C = b_re.shape[-1]

    def bmat(bb):
        x = bb.reshape(nj, gl, P, C)
        return jnp.einsum('jgpc,gh->jgchp', x, eye).reshape(nj, gl * C, gl * P)

    bw = jnp.concatenate([bmat(bbr), bmat(bbi)], axis=2).astype(BF16)

    def cmat(cc):
        x = cc.reshape(nj, gl, C, P)
        return jnp.einsum('jgcp,gh->jgphc', x, eye).reshape(nj, gl * P, gl * C)

    cw = jnp.concatenate([cmat(c_re), cmat(-c_im)], axis=1).astype(BF16)
    return a_row, as_row, bw, cw


def _s5_state_to_lanes(s_re, s_im):
    B, G, P = s_re.shape
    nj = S5_LANE_GROUPS
    x = jnp.stack([s_re.reshape(B, nj, (G // nj) * P), s_im.reshape(B, nj, (G // nj) * P)], axis=2)
    return x.reshape(B // SUBLANES, SUBLANES, 2 * G * P)


def _s5_lanes_to_state(x, G, P):
    G8 = x.shape[0]
    nj = S5_LANE_GROUPS
    x = x.reshape(G8 * SUBLANES, nj, 2, G // nj, P)
    return x[:, :, 0].reshape(-1, G, P), x[:, :, 1].reshape(-1, G, P)


def _merge_kernel(has_pos, *refs):
    if has_pos:
        x_ref, pos_ref = refs[0], refs[1]
        refs = refs[2:]
    else:
        x_ref, pos_ref = refs[0], None
        refs = refs[1:]
    (ha_ref, hb_ref, sa_ref, sb_ref, mod_ref, nw_ref, wpa_ref, wpb_ref, wout_ref, wrh_ref, wrm_ref,
     br_ref, x1_ref, hp_ref, aff_ref) = refs
    x = x_ref[0]
    if has_pos:
        x = x + pos_ref[...]
    tn = (((0,), (0,)), ((), ()))
    dh = ha_ref.shape[3]
    pa = sum(lax.dot_general(ha_ref[0, hh, 0], wpa_ref[hh * dh:(hh + 1) * dh, :], tn,
                             preferred_element_type=F32) for hh in range(ha_ref.shape[1]))
    merged = sa_ref[0] * pa \
        + sb_ref[0] * jnp.dot(hb_ref[0], wpb_ref[...], preferred_element_type=F32)
    out = jnp.dot(merged.astype(BF16), wout_ref[...], preferred_element_type=F32)
    x1 = x + mod_ref[0, 2:3, :] * out
    h2 = _rms_mod(x1, nw_ref[...], mod_ref[0, 3:4, :], mod_ref[0, 4:5, :])
    tm = h2.shape[0]
    for sl in range(h2.shape[1] // LANES):
        hp_ref[0, pl.ds(sl, tm, stride=SUBLANES), :] = h2[:, sl * LANES:(sl + 1) * LANES]
        x1_ref[0, pl.ds(sl, tm, stride=SUBLANES), :] = x1[:, sl * LANES:(sl + 1) * LANES]
    h2h, h2m, _ = _split3(h2)
    logits = jnp.dot(h2h, wrh_ref[...], preferred_element_type=F32) \
        + jnp.dot(h2m, wrh_ref[...], preferred_element_type=F32) \
        + jnp.dot(h2h, wrm_ref[...], preferred_element_type=F32)
    lt = logits.T[0:N_EXPERTS, :] + br_ref[...]
    mx = jnp.max(lt, axis=0, keepdims=True)
    ex = jnp.exp(lt - mx)
    aff_ref[0] = ex / jnp.sum(ex, axis=0, keepdims=True)


def _merge(x, pos, ha, hb, sa, sb, mod, mod_row_fn, nw, wts, tm):
    B, T, D = x.shape
    has_pos = pos is not None
    wpa, wpb, wout, wrh, wrm, br = wts
    assert D == SUBLANES * LANES, D
    const = lambda shape: pl.BlockSpec(shape, lambda b, i: (0,) * len(shape),
                                       pipeline_mode=pl.Buffered(1))
    tok = lambda width: pl.BlockSpec((1, tm, width), lambda b, i: (b, i, 0))
    in_specs = [tok(D)]
    args = [x]
    if has_pos:
        in_specs.append(pl.BlockSpec((tm, D), lambda b, i: (i, 0)))
        args.append(pos)
    assert ha.shape[-1] == tm, (ha.shape, tm)
    in_specs += [pl.BlockSpec((1,) + ha.shape[1:2] + (1,) + ha.shape[3:], lambda b, i: (b, 0, i, 0, 0)),
                 tok(hb.shape[-1]), tok(D), tok(D),
                 pl.BlockSpec((1, N_MOD, D), lambda b, i: (mod_row_fn(b), 0, 0)),
                 const((1, D)), const(wpa.shape), const(wpb.shape), const(wout.shape),
                 const(wrh.shape), const(wrm.shape), const(br.shape)]
    args += [ha, hb, sa, sb, mod, nw, wpa, wpb, wout, wrh, wrm, br]
    return pl.pallas_call(
        functools.partial(_merge_kernel, has_pos),
        grid=(B, T // tm),
        in_specs=in_specs,
        out_specs=[pl.BlockSpec((1, tm * SUBLANES, LANES), lambda b, i: (b, i, 0)),
                   pl.BlockSpec((1, tm * SUBLANES, LANES), lambda b, i: (b, i, 0)),
                   pl.BlockSpec((1, N_EXPERTS, tm), lambda b, i: (b, 0, i))],
        out_shape=[jax.ShapeDtypeStruct((B, T * SUBLANES, LANES), F32),
                   jax.ShapeDtypeStruct((B, T * SUBLANES, LANES), F32),
                   jax.ShapeDtypeStruct((B, N_EXPERTS, T), F32)],
        compiler_params=_cparams(("arbitrary", "arbitrary")),
        name="merge",
    )(*args)


def _route_kernel(cap, aff_ref, idx_ref, val_ref, jb_s, jl_s):
    aff = aff_ref[0]
    E, T = aff.shape
    thr_bits = jnp.zeros((E, 1), jnp.int32)
    for bit in range(30, -1, -1):
        cand = thr_bits | (1 << bit)
        cnt = jnp.sum((aff >= lax.bitcast_convert_type(cand, F32)).astype(F32), axis=1, keepdims=True)
        thr_bits = jnp.where(cnt >= cap, cand, thr_bits)
    thr = lax.bitcast_convert_type(thr_bits, F32)
    nxt = lax.bitcast_convert_type(thr_bits + 1, F32)
    gt = aff >= nxt
    eq = (aff >= thr) & jnp.logical_not(gt)
    need = cap - jnp.sum(gt.astype(F32), axis=1, keepdims=True)

    r = lax.broadcasted_iota(jnp.int32, (LANES, LANES), 0)
    c = lax.broadcasted_iota(jnp.int32, (LANES, LANES), 1)
    tri = (r < c).astype(BF16)

    def excl_cumsum(mk):
        outs = []
        carry = jnp.zeros((E, 1), F32)
        for kb in range(T // LANES):
            blk = mk[:, kb * LANES:(kb + 1) * LANES]
            outs.append(jnp.dot(blk.astype(BF16), tri, preferred_element_type=F32) + carry)
            carry = carry + jnp.sum(blk.astype(F32), axis=1, keepdims=True)
        return jnp.concatenate(outs, axis=1)

    tie_rank = excl_cumsum(eq)
    mask = gt | (eq & (tie_rank < need))
    slot = jnp.where(mask, excl_cumsum(mask), -1.0).astype(jnp.int32)
    JL = min(cap, LANES)
    njb = cap // JL
    NC = 5
    WR = -(-(njb * NC) // SUBLANES) * SUBLANES
    jb_s[...] = slot >> (JL.bit_length() - 1)
    jl_s[...] = slot & (JL - 1)
    tio = lax.broadcasted_iota(jnp.int32, (1, T), 1)
    t_hi = (tio >> 6).astype(F32)
    t_lo = (tio & 63).astype(F32)
    jl_col = lax.broadcasted_iota(jnp.int32, (JL, 1), 0)
    wrow = lax.broadcasted_iota(jnp.int32, (WR, 1), 0)
    wjb = jnp.full((WR, 1), -2, jnp.int32)
    for jb in range(njb):
        wjb = jnp.where((wrow >= NC * jb) & (wrow < NC * (jb + 1)), jb, wjb)
    wcomp = wrow - NC * wjb
    lane = lax.broadcasted_iota(jnp.int32, (JL, E), 1)
    nt = (((1,), (1,)), ((), ()))

    def per_expert(e, carry):
        a1, a2, a3 = _split3(aff_ref[0, pl.ds(e, 1), :])
        comp = jnp.where(wcomp == 0, t_hi,
                         jnp.where(wcomp == 1, t_lo,
                                   jnp.where(wcomp == 2, a1.astype(F32),
                                             jnp.where(wcomp == 3, a2.astype(F32), a3.astype(F32)))))
        wt = jnp.where(jb_s[pl.ds(e, 1), :] == wjb, comp, 0.0).astype(BF16)
        onehot = jnp.where(jl_s[pl.ds(e, 1), :] == jl_col, 1.0, 0.0).astype(BF16)
        r = lax.dot_general(onehot, wt, nt, preferred_element_type=F32)
        out = []
        for jb in range(njb):
            c0 = jb * NC
            ie = r[:, c0:c0 + 1] * 64.0 + r[:, c0 + 1:c0 + 2]
            ve = r[:, c0 + 2:c0 + 3] + r[:, c0 + 3:c0 + 4] + r[:, c0 + 4:c0 + 5]
            out.append(jnp.where(lane == e, ie, carry[2 * jb]))
            out.append(jnp.where(lane == e, ve, carry[2 * jb + 1]))
        return tuple(out)

    res = lax.fori_loop(0, E, per_expert, tuple(jnp.zeros((JL, E), F32) for _ in range(2 * njb)),
                        unroll=2)
    for jb in range(njb):
        idx_ref[0, jb * JL:(jb + 1) * JL, :] = res[2 * jb].astype(jnp.int32)
        val_ref[0, jb * JL:(jb + 1) * JL, :] = res[2 * jb + 1]


def _route(aff, cap):
    B, E0, T = aff.shape
    R = max(1, min(B, ROUTE_ROW_ELEMS // (E0 * T)))
    E = R * E0
    idx, val = pl.pallas_call(
        functools.partial(_route_kernel, cap),
        grid=(B // R,),
        in_specs=[pl.BlockSpec((1, E, T), lambda b: (b, 0, 0))],
        out_specs=[pl.BlockSpec((1, cap, E), lambda b: (b, 0, 0)),
                   pl.BlockSpec((1, cap, E), lambda b: (b, 0, 0))],
        out_shape=[jax.ShapeDtypeStruct((B // R, cap, E), jnp.int32),
                   jax.ShapeDtypeStruct((B // R, cap, E), F32)],
        scratch_shapes=[pltpu.VMEM((E, T), jnp.int32), pltpu.VMEM((E, T), jnp.int32)],
        compiler_params=_cparams(("arbitrary",)),
        name="route",
    )(aff.reshape(B // R, E, T))
    unbatch = lambda a: a.reshape(B // R, cap, R, E0).transpose(0, 2, 1, 3).reshape(B, cap, E0)
    return unbatch(idx), unbatch(val)


def _moe_kernel(idx_ref, idxn_ref, val_ref, valn_ref, tok_ref, g2_ref, nw_ref, w1_ref, w3_ref, w2_ref, x1_ref, out_ref,
                tok_s, xe_s, xb_s, ye_s, yt_s, acc_s, yb_s, sem_tok, sem_in, sem_out):
    b, e, f = pl.program_id(0), pl.program_id(1), pl.program_id(2)
    nb, ne = pl.num_programs(0), pl.num_programs(1)
    cap = ye_s.shape[0]
    S = SUBLANES
    GR = MOE_ROW_GROUP
    nr = tok_s.shape[0]
    nt = nr // S

    def tok_copy(t):
        return pltpu.make_async_copy(tok_ref.at[t], tok_s, sem_tok)

    def x1_copy():
        return pltpu.make_async_copy(x1_ref.at[b], acc_s.at[pl.ds(0, nr), :], sem_in)

    def tile(ref, r):
        return ref.at[pl.ds(pl.multiple_of(r * S, S), S), :]

    def gather_group(iref, g):
        rows = [iref[0, 0, g * GR + i] for i in range(GR)]
        vals = [tile(tok_s, rows[i])[...] for i in range(GR)]
        for i in range(GR):
            tile(xe_s, g * GR + i)[...] = vals[i]

    def scatter_group(iref, vref, g, park=None):
        rows = [iref[0, 0, g * GR + i] for i in range(GR)]
        if park is not None:
            rows = [jnp.where(park, nt + i, rows[i]) for i in range(GR)]
        vals = [tile(acc_s, rows[i])[...] + tile(yt_s, g * GR + i)[...] * vref[0, 0, g * GR + i]
                for i in range(GR)]
        for i in range(GR):
            tile(acc_s, rows[i])[...] = vals[i]

    def ffn_half():
        xe = xb_s[...]
        h1 = jnp.dot(xe, w1_ref[0], preferred_element_type=F32)
        h3 = jnp.dot(xe, w3_ref[0], preferred_element_type=F32)
        hid = (h1 * _sigmoid(h1) * h3).astype(BF16)
        return jnp.dot(hid, w2_ref[0], preferred_element_type=F32)

    @pl.when((e == 0) & (f == 0))
    def _():
        @pl.when(b == 0)
        def _():
            tok_copy(b).start()
        x1_copy().start()
        yt_s[...] = jnp.zeros_like(yt_s)
        acc_s[pl.ds(nr, GR * S), :] = jnp.zeros((GR * S, LANES), F32)
        tok_copy(b).wait()
        lax.fori_loop(0, cap // GR, lambda g, c: (gather_group(idx_ref, g), c)[1], 0)

    @pl.when(f == 0)
    def _():
        xb_s[...] = jnp.concatenate([xe_s[pl.ds(sl, cap, stride=S), :] for sl in range(S)],
                                    axis=1).astype(BF16)
        ye_s[...] = ffn_half()
        for g in range(cap // GR):
            scatter_group(idxn_ref, valn_ref, g, park=(e == 0))

    @pl.when((e == 0) & (f == 1))
    def _():
        x1_copy().wait()

    @pl.when(f == 1)
    def _():
        yw = (ye_s[...] + ffn_half()) * g2_ref[0]
        for sl in range(S):
            yt_s[pl.ds(sl, cap, stride=S), :] = yw[:, sl * LANES:(sl + 1) * LANES]
        for g in range(cap // GR):
            gather_group(idxn_ref, g)

    @pl.when((e == ne - 1) & (f == 1))
    def _():
        lax.fori_loop(0, cap // GR, lambda g, c: (scatter_group(idx_ref, val_ref, g), c)[1], 0)

        @pl.when(b + 1 < nb)
        def _():
            tok_copy(b + 1).start()
        ch = yb_s.shape[1]
        nch = nr // (ch * S)
        copies = []
        for c in range(nch):
            slot = c % 2
            if c >= 2:
                copies[c - 2].wait()
            x = jnp.concatenate([acc_s[pl.ds(c * ch * S + sl, ch, stride=S), :] for sl in range(S)], axis=1)
            ms = jnp.mean(x * x, axis=-1, keepdims=True)
            yb_s[slot] = x * lax.rsqrt(ms + EPS) * nw_ref[...]
            cp = pltpu.make_async_copy(yb_s.at[slot], out_ref.at[b, pl.ds(c * ch, ch), :], sem_out.at[slot])
            cp.start()
            copies.append(cp)
        for c in range(max(nch - 2, 0), nch):
            copies[c].wait()


def _moe(idx, tok, x1, val, gate2, nw, w1, w3, w2):
    B, NR, _ = tok.shape
    E, D, FF = w1.shape
    cap = idx.shape[-1]
    nf = EXPERT_FF_SPLIT
    assert nf == 2, nf
    fh = FF // nf
    NT = NR // SUBLANES
    ch = min(MOE_OUT_CHUNK, NT)
    own = pl.BlockSpec((1, 1, cap), lambda b, e, f: (b * E + e, 0, 0), memory_space=pltpu.SMEM)
    nbr = pl.BlockSpec((1, 1, cap), lambda b, e, f: (b * E + jnp.clip(e - 1 + 2 * f, 0, E - 1), 0, 0),
                       memory_space=pltpu.SMEM)
    return pl.pallas_call(
        _moe_kernel,
        grid=(B, E, nf),
        in_specs=[own, nbr, own, nbr,
                  pl.BlockSpec(memory_space=pl.ANY),
                  pl.BlockSpec((1, 1, D), lambda b, e, f: (b, 0, 0)),
                  pl.BlockSpec((1, D), lambda b, e, f: (0, 0)),
                  pl.BlockSpec((1, D, fh), lambda b, e, f: (e, 0, f)),
                  pl.BlockSpec((1, D, fh), lambda b, e, f: (e, 0, f)),
                  pl.BlockSpec((1, fh, D), lambda b, e, f: (e, f, 0)),
                  pl.BlockSpec(memory_space=pl.ANY)],
        out_specs=pl.BlockSpec(memory_space=pl.ANY),
        out_shape=jax.ShapeDtypeStruct((B, NT, D), F32),
        scratch_shapes=[pltpu.VMEM((NR, LANES), F32),
                        pltpu.VMEM((cap * SUBLANES, LANES), F32), pltpu.VMEM((cap, D), BF16),
                        pltpu.VMEM((cap, D), F32), pltpu.VMEM((cap * SUBLANES, LANES), F32),
                        pltpu.VMEM((NR + MOE_ROW_GROUP * SUBLANES, LANES), F32), pltpu.VMEM((2, ch, D), F32),
                        pltpu.SemaphoreType.DMA(()), pltpu.SemaphoreType.DMA(()),
                        pltpu.SemaphoreType.DMA((2,))],
        compiler_params=_cparams(("arbitrary", "arbitrary", "arbitrary")),
        name="moe",
    )(idx, idx, val, val, tok, gate2, nw, w1, w3, w2, x1)


def _pos2d(T, D):
    rows = T // GRID_W
    quarter = D // 4
    freqs = 1.0 / (10000.0 ** (jnp.arange(quarter, dtype=F32) / quarter))
    er = jnp.arange(rows, dtype=F32)[:, None] * freqs
    ec = jnp.arange(GRID_W, dtype=F32)[:, None] * freqs
    rep = lambda a: jnp.repeat(a, GRID_W, axis=0)
    til = lambda a: jnp.tile(a, (rows, 1))
    return jnp.concatenate([rep(jnp.sin(er)), rep(jnp.cos(er)), til(jnp.sin(ec)), til(jnp.cos(ec))], axis=-1)


def _stream(x, pos, mod, mod_row_fn, P, mstate, s5_x0, tm):
    B, T, D = x.shape
    L = min(MLSTM_CHUNK, T)
    q, k, v, o, gc, gr, gtot, u, sa, sb = _inproj(x, pos, mod, mod_row_fn, P['norm_mix'],
                                                   P['w_in_parts'], tm, L)
    res = _mlstm(q, k, v, o, gc, gr, gtot, P['head_norm'], mstate, L)
    ha, mout = res[0], res[1:]
    Lc = min(S5_CHUNK, T)
    yf, xf = _s5_pass(False, u, None, s5_x0[0], *P['s5_dir'][0], None, Lc)
    hb, xb = _s5_pass(True, u, yf, s5_x0[1], *P['s5_dir'][1], P['glu'], Lc)
    x1, hp, aff = _merge(x, pos, ha, hb, sa, sb, mod, mod_row_fn, P['norm_ffn'], P['merge_w'], tm)
    cap = CAPACITY_FACTOR * T // N_EXPERTS
    idx, val = _route(aff, cap)
    E = N_EXPERTS
    gb = max(1, min(B, MOE_TABLE_ROWS // T))
    if any(mod_row_fn(r) != mod_row_fn(r - r % gb) for r in range(B)):
        gb = 1
    nb = B // gb
    offs = (jnp.arange(B, dtype=jnp.int32) % gb) * T
    idx_f = (idx + offs[:, None, None]).reshape(nb, gb, cap, E).transpose(0, 3, 1, 2).reshape(nb * E, 1, gb * cap)
    val_f = val.reshape(nb, gb, cap, E).transpose(0, 3, 1, 2).reshape(nb * E, 1, gb * cap)
    gate2 = jnp.stack([mod[mod_row_fn(t * gb), N_MOD - 1] for t in range(nb)])[:, None, :]
    tiles = lambda a: a.reshape(nb, gb * T * SUBLANES, LANES)
    y = _moe(idx_f, tiles(hp), tiles(x1), val_f, gate2, P['norm_final'], *P['experts']).reshape(B, T, D)
    return y, mout, (xf, xb)


def kernel(x_prompt, x_sample, state_mlstm_C, state_mlstm_n, state_mlstm_m, state_s5_re, state_s5_im,
           c, c_ctx, w_ada, b_ada, norm_mix, norm_ffn, w_in, b_gate, head_norm,
           s5_a_re, s5_a_im, s5_log_dt, s5_b_re, s5_b_im, s5_c_re, s5_c_im, s5_d, w_glu, b_glu,
           w_pA, w_pB, w_out, w_router, b_router, w_e1, w_e3, w_e2, norm_final):
    Bp, Tp, D = x_prompt.shape
    Bs, Ts, _ = x_sample.shape
    depth = w_ada.shape[0]
    H, dh = MLSTM_HEADS, MLSTM_HEAD_DIM
    W = H * dh
    G, Pn = s5_a_re.shape[2], s5_a_re.shape[3]
    SW = G * S5_GROUP

    assert depth == 1, depth
    l = 0
    pos = _pos2d(Ts, D)
    nrow = Bs + 1
    rows = -(-nrow // SUBLANES) * SUBLANES
    cpad = jnp.zeros((rows, D), F32).at[0:Bs].set(c).at[Bs].set(c_ctx)
    mod = _adaln(cpad, w_ada[l], b_ada[l]).reshape(rows, N_MOD, D)

    wi = w_in[l]
    gcols = wi[:, 4 * W:4 * W + 4 * H].reshape(D, 4, H).transpose(0, 2, 1).reshape(D, 4 * H)
    wug = jnp.concatenate([wi[:, 4 * W + 4 * H:4 * W + 4 * H + SW], gcols,
                           jnp.zeros((D, LANES - 4 * H), F32)], axis=1).astype(BF16)
    bg = jnp.concatenate([b_gate[l].T.reshape(1, 4 * H), jnp.zeros((1, LANES - 4 * H), F32)], axis=1)
    w_in_parts = (wi[:, 0:3 * W].astype(BF16), wi[:, 3 * W:4 * W].astype(BF16), wug,
                  wi[:, 4 * W + 4 * H + SW:].astype(BF16), bg)
    s5_dir = [_s5_discretise(s5_a_re[l, d], s5_a_im[l, d], s5_log_dt[l, d], s5_b_re[l], s5_b_im[l],
                             s5_c_re[l, d], s5_c_im[l, d]) for d in range(2)]
    s5_dir = [(jnp.broadcast_to(a, (SUBLANES, a.shape[1])), jnp.broadcast_to(s, (SUBLANES, s.shape[1])), bw, cw)
              for (a, s, bw, cw) in s5_dir]
    wr = jnp.concatenate([w_router[l], jnp.zeros((D, LANES - N_EXPERTS), F32)], axis=1)
    wr_hi = wr.astype(BF16)
    wr_mid = (wr - wr_hi.astype(F32)).astype(BF16)
    P = {
        'norm_mix': norm_mix[l].reshape(1, D), 'norm_ffn': norm_ffn[l].reshape(1, D),
        'norm_final': norm_final.reshape(1, D),
        'w_in_parts': w_in_parts, 'head_norm': head_norm[l], 's5_dir': s5_dir,
        'glu': (s5_d[l].reshape(1, SW), w_glu[l].astype(BF16), b_glu[l].reshape(1, SW)),
        'merge_w': (w_pA[l].astype(BF16), w_pB[l].astype(BF16), w_out[l].astype(BF16), wr_hi, wr_mid,
                    b_router[l].reshape(N_EXPERTS, 1)),
        'experts': (w_e1[l].astype(BF16), w_e3[l].astype(BF16), w_e2[l].astype(BF16)),
    }
    zero_x0 = jnp.zeros((Bp // SUBLANES, SUBLANES, 2 * G * Pn), F32)
    yp, mout, (xf, xb) = _stream(x_prompt, None, mod, lambda b: Bs, P, None, (zero_x0, zero_x0),
                                 min(MLSTM_CHUNK, Tp))
    lat_x0 = [_s5_state_to_lanes(state_s5_re[:, l, d], state_s5_im[:, l, d]) for d in range(2)]
    ys, _, _ = _stream(x_sample, pos, mod, lambda b: b, P,
                       (state_mlstm_C[:, l], state_mlstm_n[:, l], state_mlstm_m[:, l]), lat_x0,
                       min(MLSTM_CHUNK, Ts))
    Cn, nn, mn = mout
    fre, fim = _s5_lanes_to_state(xf, G, Pn)
    bre, bim = _s5_lanes_to_state(xb, G, Pn)
    outs = (Cn, nn.reshape(Bp, 2, H, dh), mn.reshape(Bp, 2, H),
            jnp.stack([fre, bre], axis=1), jnp.stack([fim, bim], axis=1))
    out_C, out_n, out_m, out_re, out_im = (o[:, None] for o in outs)
    return (yp, ys, out_C, out_n, out_m, out_re, out_im)
```

```python
import functools

import jax
import jax.numpy as jnp
from jax import lax
from jax.experimental import pallas as pl
from jax.experimental.pallas import tpu as pltpu

F32 = jnp.float32
BF16 = jnp.bfloat16
EPS = 1e-6

MLSTM_HEADS = 4
MLSTM_HEAD_DIM = 256
N_EXPERTS = 16
CAPACITY_FACTOR = 2
S5_GROUP = 16
S5_STATE = 64
GRID_W = 64
N_MOD = 6

LANES = 128
SUBLANES = 8
VMEM_LIMIT_BYTES = 60000 * 1024

MLSTM_CHUNK = 256
MLSTM_UNROLL = 8
S5_CHUNK = 128
S5_LANE_GROUPS = 4
S5_SEGMENTS = 4
EXPERT_FF_SPLIT = 2
ROUTE_ROW_ELEMS = 16 * 4096
MOE_ROW_GROUP = 8
MOE_OUT_CHUNK = 256
MOE_TABLE_ROWS = 4096


def _cparams(sem):
    return pltpu.CompilerParams(dimension_semantics=sem, vmem_limit_bytes=VMEM_LIMIT_BYTES)


def _sigmoid(x):
    return jax.nn.sigmoid(x)


def _log_sigmoid(x):
    return jnp.minimum(x, 0.0) - jnp.log1p(jnp.exp(-jnp.abs(x)))


def _adaln_kernel(c_ref, w_ref, b_ref, o_ref):
    c = c_ref[...]
    s = c * _sigmoid(c)
    o_ref[...] = jnp.dot(s, w_ref[...], precision=lax.Precision.HIGHEST,
                         preferred_element_type=F32) + b_ref[...]


def _adaln(cpad, w, b):
    rows, d = cpad.shape
    n = w.shape[1]
    tn = 1536
    return pl.pallas_call(
        _adaln_kernel,
        grid=(n // tn,),
        in_specs=[pl.BlockSpec((rows, d), lambda j: (0, 0)),
                  pl.BlockSpec((d, tn), lambda j: (0, j)),
                  pl.BlockSpec((1, tn), lambda j: (0, j))],
        out_specs=pl.BlockSpec((rows, tn), lambda j: (0, j)),
        out_shape=jax.ShapeDtypeStruct((rows, n), F32),
        compiler_params=_cparams(("arbitrary",)),
        name="adaln",
    )(cpad, w, b.reshape(1, n))


def _rms_mod(x, nw, shift, scale):
    ms = jnp.mean(x * x, axis=-1, keepdims=True)
    y = x * lax.rsqrt(ms + EPS) * nw
    return y * (1.0 + scale) + shift


def _split3(x):
    hi = x.astype(BF16)
    r = x - hi.astype(F32)
    mid = r.astype(BF16)
    lo = (r - mid.astype(F32)).astype(BF16)
    return hi, mid, lo


def _inproj_kernel(has_pos, *refs):
    if has_pos:
        x_ref, pos_ref = refs[0], refs[1]
        refs = refs[2:]
    else:
        x_ref, pos_ref = refs[0], None
        refs = refs[1:]
    (mod_ref, nw_ref, wqkv_ref, wo_ref, wug_ref, wab_ref, bg_ref,
     q_ref, k_ref, v_ref, o_ref, gc_ref, gr_ref, gtot_ref, u_ref, sa_ref, sb_ref) = refs
    x = x_ref[0]
    if has_pos:
        x = x + pos_ref[...]
    h = _rms_mod(x, nw_ref[...], mod_ref[0, 0:1, :], mod_ref[0, 1:2, :])
    hb = h.astype(BF16)
    w = MLSTM_HEADS * MLSTM_HEAD_DIM
    qkv = jnp.dot(hb, wqkv_ref[...], preferred_element_type=F32)
    q_ref[0] = (qkv[:, 0:w] * (MLSTM_HEAD_DIM ** -0.5)).astype(BF16)
    k_ref[0] = qkv[:, w:2 * w].astype(BF16)
    vt = qkv[:, 2 * w:3 * w].T.astype(BF16)
    for hh in range(MLSTM_HEADS):
        v_ref[0, hh, 0] = vt[hh * MLSTM_HEAD_DIM:(hh + 1) * MLSTM_HEAD_DIM, :]
    ot = jnp.dot(hb, wo_ref[...], preferred_element_type=F32).T
    for hh in range(MLSTM_HEADS):
        o_ref[0, hh, 0] = ot[hh * MLSTM_HEAD_DIM:(hh + 1) * MLSTM_HEAD_DIM, :]
    ug = jnp.dot(hb, wug_ref[...], preferred_element_type=F32)
    su = u_ref.shape[-1]
    u_ref[0] = ug[:, 0:su]
    g = ug[:, su:su + LANES] + bg_ref[...]
    col = lax.broadcasted_iota(jnp.int32, g.shape, 1)
    g = jnp.where((col % 2) == 1, _log_sigmoid(g), g)
    tm = g.shape[0]
    tril = (lax.broadcasted_iota(jnp.int32, (tm, tm), 1)
            <= lax.broadcasted_iota(jnp.int32, (tm, tm), 0)).astype(BF16)
    cs = sum(jnp.dot(tril, part, preferred_element_type=F32) for part in _split3(g))
    tot = cs[tm - 1:tm, :]
    cum = jnp.where((col % 4) < 2, cs, tot - cs + g)
    cum_l = pltpu.roll(cum, LANES - 1, axis=1)
    tot_l = pltpu.roll(jnp.broadcast_to(tot, g.shape), LANES - 1, axis=1)
    r = g - cum_l
    cumt = cum.T
    ggt = (tot_l - cum_l + g).T
    for hh in range(MLSTM_HEADS):
        gc_ref[0, hh] = r[:, 4 * hh:4 * hh + 4]
        gr_ref[0, hh, 0, 0:4, :] = cumt[4 * hh:4 * hh + 4, :]
        gr_ref[0, hh, 0, 4:8, :] = ggt[4 * hh:4 * hh + 4, :]
        gtot_ref[0, hh, 0] = tot[:, 4 * hh:4 * hh + 4]
    ab = jnp.dot(hb, wab_ref[...], preferred_element_type=F32)
    d = sa_ref.shape[-1]
    sa_ref[0] = _sigmoid(ab[:, 0:d])
    sb_ref[0] = _sigmoid(ab[:, d:2 * d])


def _inproj(x, pos, mod, mod_row_fn, nw, wts, tm, L):
    B, T, D = x.shape
    has_pos = pos is not None
    wqkv, wo, wug, wab, bg = wts
    su = wug.shape[1] - LANES
    assert tm == L, (tm, L)
    const = lambda shape: pl.BlockSpec(shape, lambda b, i: (0,) * len(shape),
                                       pipeline_mode=pl.Buffered(1))
    in_specs = [pl.BlockSpec((1, tm, D), lambda b, i: (b, i, 0))]
    args = [x]
    if has_pos:
        in_specs.append(pl.BlockSpec((tm, D), lambda b, i: (i, 0)))
        args.append(pos)
    in_specs += [pl.BlockSpec((1, N_MOD, D), lambda b, i: (mod_row_fn(b), 0, 0)),
                 const((1, D)), const(wqkv.shape), const(wo.shape), const(wug.shape),
                 const(wab.shape), const((1, LANES))]
    args += [mod, nw, wqkv, wo, wug, wab, bg]
    W = MLSTM_HEADS * MLSTM_HEAD_DIM
    tok = lambda width, dt: (jax.ShapeDtypeStruct((B, T, width), dt),
                             pl.BlockSpec((1, tm, width), lambda b, i: (b, i, 0)))
    outs = [tok(W, BF16), tok(W, BF16),
            (jax.ShapeDtypeStruct((B, MLSTM_HEADS, T // L, MLSTM_HEAD_DIM, L), BF16),
             pl.BlockSpec((1, MLSTM_HEADS, 1, MLSTM_HEAD_DIM, L), lambda b, i: (b, 0, i, 0, 0))),
            (jax.ShapeDtypeStruct((B, MLSTM_HEADS, T // L, MLSTM_HEAD_DIM, L), F32),
             pl.BlockSpec((1, MLSTM_HEADS, 1, MLSTM_HEAD_DIM, L), lambda b, i: (b, 0, i, 0, 0))),
            (jax.ShapeDtypeStruct((B, MLSTM_HEADS, T, 4), F32),
             pl.BlockSpec((1, MLSTM_HEADS, tm, 4), lambda b, i: (b, 0, i, 0))),
            (jax.ShapeDtypeStruct((B, MLSTM_HEADS, T // L, 8, L), F32),
             pl.BlockSpec((1, MLSTM_HEADS, 1, 8, L), lambda b, i: (b, 0, i, 0, 0))),
            (jax.ShapeDtypeStruct((B, MLSTM_HEADS, T // L, 1, 4), F32),
             pl.BlockSpec((1, MLSTM_HEADS, 1, 1, 4), lambda b, i: (b, 0, i, 0, 0))),
            tok(su, F32), tok(D, F32), tok(D, F32)]
    return pl.pallas_call(
        functools.partial(_inproj_kernel, has_pos),
        grid=(B, T // tm),
        in_specs=in_specs,
        out_specs=[o[1] for o in outs],
        out_shape=[o[0] for o in outs],
        compiler_params=_cparams(("arbitrary", "arbitrary")),
        name="inproj",
    )(*args)


MLSTM_AUG_ROWS = 16


def _mlstm_chunk(q, k, vt_aug, r_col, b_row, gg_row, total, C_ref, m_ref, d, valid_t):
    dh = q.shape[1]
    Ca = C_ref[d]
    m = m_ref[d]
    nt = (((1,), (1,)), ((), ()))
    rmask = jnp.where(valid_t, r_col, -jnp.inf)
    mx = jnp.maximum(m, jnp.max(rmask, axis=0, keepdims=True))
    p = jnp.exp(rmask - mx)
    w_inter = jnp.exp(m - mx)
    st = lax.dot_general(k, q, nt, preferred_element_type=F32) * p
    na = w_inter * lax.dot_general(Ca.astype(BF16), q, nt, preferred_element_type=F32) \
        + jnp.dot(vt_aug, st.astype(BF16), preferred_element_type=F32)
    rden = 1.0 / jnp.maximum(jnp.abs(na[dh:dh + 1, :]), jnp.exp(-(b_row + mx)))
    ht = na[0:dh, :] * rden
    m_new = jnp.maximum(total + m, jnp.max(gg_row, axis=1, keepdims=True))
    decay = jnp.exp(total + m - m_new)
    vw = (vt_aug.astype(F32) * jnp.exp(gg_row - m_new)).astype(BF16)
    C_ref[d] = decay * Ca + jnp.dot(vw, k, preferred_element_type=F32)
    m_ref[d] = m_new
    return ht


def _mlstm_kernel(zero_init, L, *refs):
    T, dh = refs[0].shape[1], refs[0].shape[2]
    if zero_init:
        (q_ref, k_ref, v_ref, o_ref, gc_ref, gr_ref, gtot_ref, hn_ref,
         out_ref, Co_ref, no_ref, mo_ref, C_s, m_s, hf_s, hb_s) = refs
        C_s[...] = jnp.zeros_like(C_s)
        m_s[...] = jnp.zeros_like(m_s)
    else:
        (q_ref, k_ref, v_ref, o_ref, gc_ref, gr_ref, gtot_ref, hn_ref, C0_ref, n0_ref, m0_ref,
         out_ref, C_s, m_s, hf_s, hb_s) = refs
        for d in range(2):
            C_s[d, 0:dh, :] = C0_ref[0, d, 0].T
            C_s[d, dh:dh + MLSTM_AUG_ROWS, :] = jnp.broadcast_to(n0_ref[0, d, 0], (MLSTM_AUG_ROWS, dh))
        m_s[...] = m0_ref[0, :, 0]
    nc = T // L
    row = lax.broadcasted_iota(jnp.int32, (L, L), 0)
    col = lax.broadcasted_iota(jnp.int32, (L, L), 1)
    masks_t = (row <= col, row >= col)
    ones = jnp.ones((MLSTM_AUG_ROWS, L), BF16)

    def body(c, carry):
        for d in range(2):
            cc = c if d == 0 else nc - 1 - c
            r0 = pl.multiple_of(cc * L, L)
            q = q_ref[0, pl.ds(r0, L), :]
            k = k_ref[0, pl.ds(r0, L), :]
            vt_aug = jnp.concatenate([v_ref[0, 0, cc], ones], axis=0)
            gc = gc_ref[0, 0, pl.ds(r0, L), :]
            gr = gr_ref[0, 0, cc]
            tot = gtot_ref[0, 0, cc]
            ht = _mlstm_chunk(q, k, vt_aug, gc[:, 2 * d:2 * d + 1], gr[2 * d + 1:2 * d + 2, :],
                              gr[4 + 2 * d:5 + 2 * d, :], tot[:, 2 * d + 1:2 * d + 2],
                              C_s, m_s, d, masks_t[d])
            (hf_s if d == 0 else hb_s)[cc] = ht
        return carry

    lax.fori_loop(0, nc, body, 0, unroll=MLSTM_UNROLL if nc % MLSTM_UNROLL == 0 else 1)

    hn = jnp.concatenate([hn_ref[0]] * (L // LANES), axis=1)

    def fin(c, carry):
        hm = (hf_s[c] + hb_s[c]) * _sigmoid(o_ref[0, 0, c])
        hm = hm * lax.rsqrt(jnp.mean(hm * hm, axis=0, keepdims=True) + EPS) * hn
        out_ref[0, 0, c] = hm.astype(out_ref.dtype)
        return carry

    lax.fori_loop(0, nc, fin, 0)
    if zero_init:
        for d in range(2):
            Co_ref[0, d, 0] = C_s[d, 0:dh, :].T
        no_ref[0, :, 0] = C_s[:, dh:dh + 1, :]
        mo_ref[0, :, 0] = m_s[...]


def _mlstm(q, k, v, o, gc, gr, gtot, head_norm, state, L):
    B, T, W = q.shape
    H, dh = MLSTM_HEADS, MLSTM_HEAD_DIM
    nc = T // L
    zero_init = state is None
    bh = lambda width: pl.BlockSpec((1, T, width), lambda b, h: (b, 0, h))
    per_head_t = pl.BlockSpec((1, 1, nc, dh, L), lambda b, h: (b, h, 0, 0, 0))
    in_specs = [bh(dh), bh(dh), per_head_t, per_head_t,
                pl.BlockSpec((1, 1, T, 4), lambda b, h: (b, h, 0, 0)),
                pl.BlockSpec((1, 1, nc, 8, L), lambda b, h: (b, h, 0, 0, 0)),
                pl.BlockSpec((1, 1, nc, 1, 4), lambda b, h: (b, h, 0, 0, 0)),
                pl.BlockSpec((1, dh, LANES), lambda b, h: (h, 0, 0))]
    hn_cols = jnp.broadcast_to(head_norm.reshape(H, dh, 1), (H, dh, LANES))
    args = [q, k, v, o, gc, gr, gtot, hn_cols]
    st_specs = [pl.BlockSpec((1, 2, 1, dh, dh), lambda b, h: (b, 0, h, 0, 0)),
                pl.BlockSpec((1, 2, 1, 1, dh), lambda b, h: (b, 0, h, 0, 0)),
                pl.BlockSpec((1, 2, 1, 1, 1), lambda b, h: (b, 0, h, 0, 0))]
    st_shapes = [jax.ShapeDtypeStruct((B, 2, H, dh, dh), F32),
                 jax.ShapeDtypeStruct((B, 2, H, 1, dh), F32),
                 jax.ShapeDtypeStruct((B, 2, H, 1, 1), F32)]
    out_specs = [per_head_t]
    out_shape = [jax.ShapeDtypeStruct((B, H, nc, dh, L), BF16)]
    if zero_init:
        out_specs += st_specs
        out_shape += st_shapes
    else:
        C0, n0, m0 = state
        in_specs += st_specs
        args += [C0, n0.reshape(B, 2, H, 1, dh), m0.reshape(B, 2, H, 1, 1)]
    return pl.pallas_call(
        functools.partial(_mlstm_kernel, zero_init, L),
        grid=(B, H),
        in_specs=in_specs,
        out_specs=out_specs,
        out_shape=out_shape,
        scratch_shapes=[pltpu.VMEM((2, dh + MLSTM_AUG_ROWS, dh), F32), pltpu.VMEM((2, 1, 1), F32),
                        pltpu.VMEM((nc, dh, L), F32), pltpu.VMEM((nc, dh, L), F32)],
        compiler_params=_cparams(("arbitrary", "arbitrary")),
        name="mlstm",
    )(*args)


def _s5_kernel(backward, Lc, *refs):
    if backward:
        (u_ref, yf_ref, x0_ref, ar_ref, as_ref, bw_ref, cw_ref, d_ref, wg_ref, bgl_ref,
         out_ref, xl_ref, X_s, st_s, U_s, H_s) = refs
    else:
        (u_ref, x0_ref, ar_ref, as_ref, bw_ref, cw_ref, out_ref, xl_ref, X_s, st_s, U_s) = refs
    ci = pl.program_id(1)
    nj = S5_LANE_GROUPS
    half = X_s.shape[1] // (2 * nj)
    nreq = u_ref.shape[0]

    @pl.when(ci == 0)
    def _():
        st_s[...] = x0_ref[0]

    for r in range(nreq):
        for j in range(nj):
            U_s[j, pl.ds(r, Lc, stride=nreq), :] = u_ref[r, :, j * LANES:(j + 1) * LANES]
    nseg = S5_SEGMENTS
    seg_t = Lc // nseg
    seg_order = range(nseg - 1, -1, -1) if backward else range(nseg)

    def rows_of(sg):
        return slice(sg * seg_t * SUBLANES, (sg + 1) * seg_t * SUBLANES)

    for sg in seg_order:
        for j in range(nj):
            X_s[rows_of(sg), 2 * half * j:2 * half * (j + 1)] = jnp.dot(
                U_s[j, rows_of(sg), :].astype(BF16), bw_ref[j], preferred_element_type=F32)

    def swap(x):
        parts = []
        for j in range(nj):
            parts.append(x[:, 2 * half * j + half:2 * half * (j + 1)])
            parts.append(x[:, 2 * half * j:2 * half * j + half])
        return jnp.concatenate(parts, axis=1)

    x = st_s[...]
    for sg in seg_order:
        ts = range(sg * seg_t, (sg + 1) * seg_t)
        for t in (reversed(ts) if backward else ts):
            rs = slice(t * SUBLANES, (t + 1) * SUBLANES)
            x = ar_ref[...] * x + as_ref[...] * swap(x) + X_s[rs, :]
            X_s[rs, :] = x
        y = jnp.concatenate(
            [jnp.dot(X_s[rows_of(sg), 2 * half * j:2 * half * (j + 1)].astype(BF16), cw_ref[j],
                     preferred_element_type=F32) for j in range(nj)], axis=1)
        if backward:
            u = jnp.concatenate([U_s[j, rows_of(sg), :] for j in range(nj)], axis=1)
            z = yf_ref[0, rows_of(sg), :] + y + d_ref[...] * u
            gate = _sigmoid(jnp.dot(z.astype(BF16), wg_ref[...], preferred_element_type=F32) + bgl_ref[...])
            hb = jax.nn.gelu(z) * gate
            for j in range(nj):
                H_s[j, rows_of(sg), :] = hb[:, j * LANES:(j + 1) * LANES]
        else:
            out_ref[0, rows_of(sg), :] = y
    st_s[...] = x
    xl_ref[0] = x
    if backward:
        for r in range(nreq):
            for j in range(nj):
                out_ref[r, :, j * LANES:(j + 1) * LANES] = \
                    H_s[j, pl.ds(r, Lc, stride=nreq), :].astype(out_ref.dtype)


def _s5_pass(backward, u, yf, x0, ar, asg, bw, cw, glu, Lc):
    B, T, su = u.shape
    G8 = B // SUBLANES
    rows = Lc * SUBLANES
    nchunk = T // Lc
    SL = x0.shape[-1]
    nj = S5_LANE_GROUPS
    assert su == nj * LANES, su
    cidx = (lambda c: nchunk - 1 - c) if backward else (lambda c: c)
    reqspec = pl.BlockSpec((SUBLANES, Lc, su), lambda g, c: (g, cidx(c), 0))
    rowspec = pl.BlockSpec((1, rows, su), lambda g, c: (g, cidx(c), 0))
    const = lambda shape: pl.BlockSpec(shape, lambda g, c: (0,) * len(shape))
    in_specs = [reqspec]
    args = [u]
    if backward:
        in_specs.append(rowspec)
        args.append(yf)
    in_specs += [pl.BlockSpec((1, SUBLANES, SL), lambda g, c: (g, 0, 0)),
                 const((SUBLANES, SL)), const((SUBLANES, SL)), const(bw.shape), const(cw.shape)]
    args += [x0, ar, asg, bw, cw]
    scratch = [pltpu.VMEM((rows, SL), F32), pltpu.VMEM((SUBLANES, SL), F32),
               pltpu.VMEM((nj, rows, LANES), F32)]
    if backward:
        d, wg, bgl = glu
        in_specs += [const((1, su)), const(wg.shape), const((1, su))]
        args += [d, wg, bgl]
        scratch.append(pltpu.VMEM((nj, rows, LANES), F32))
        out0 = (jax.ShapeDtypeStruct((B, T, su), BF16), reqspec)
    else:
        out0 = (jax.ShapeDtypeStruct((G8, T * SUBLANES, su), F32), rowspec)
    return pl.pallas_call(
        functools.partial(_s5_kernel, backward, Lc),
        grid=(G8, nchunk),
        in_specs=in_specs,
        out_specs=[out0[1], pl.BlockSpec((1, SUBLANES, SL), lambda g, c: (g, 0, 0))],
        out_shape=[out0[0], jax.ShapeDtypeStruct((G8, SUBLANES, SL), F32)],
        scratch_shapes=scratch,
        compiler_params=_cparams(("arbitrary", "arbitrary")),
        name="s5_bwd" if backward else "s5_fwd",
    )(*args)


def _s5_discretise(a_re, a_im, log_dt, b_re, b_im, c_re, c_im):
    G, P = a_re.shape
    nj = S5_LANE_GROUPS
    gl = G // nj
    dt = jnp.exp(log_dt)[:, None]
    e = jnp.exp(a_re * dt)
    ar = e * jnp.cos(a_im * dt)
    ai = e * jnp.sin(a_im * dt)
    den = a_re * a_re + a_im * a_im
    fr = ((ar - 1.0) * a_re + ai * a_im) / den
    fi = (ai * a_re - (ar - 1.0) * a_im) / den
    bbr = fr[..., None] * b_re - fi[..., None] * b_im
    bbi = fr[..., None] * b_im + fi[..., None] * b_re

    def lanes(re, im):
        return jnp.stack([re.reshape(nj, gl * P), im.reshape(nj, gl * P)], axis=1).reshape(1, -1)

    a_row = lanes(ar, ar)
    as_row = lanes(-ai, ai)
    eye = jnp.eye(gl, dtype=F32)
    C = b_re.shape[-1]

    def bmat(bb):
        x = bb.reshape(nj, gl, P, C)
        return jnp.einsum('jgpc,gh->jgchp', x, eye).reshape(nj, gl * C, gl * P)

    bw = jnp.concatenate([bmat(bbr), bmat(bbi)], axis=2).astype(BF16)

    def cmat(cc):
        x = cc.reshape(nj, gl, C, P)
        return jnp.einsum('jgcp,gh->jgphc', x, eye).reshape(nj, gl * P, gl * C)

    cw = jnp.concatenate([cmat(c_re), cmat(-c_im)], axis=1).astype(BF16)
    return a_row, as_row, bw, cw


def _s5_state_to_lanes(s_re, s_im):
    B, G, P = s_re.shape
    nj = S5_LANE_GROUPS
    x = jnp.stack([s_re.reshape(B, nj, (G // nj) * P), s_im.reshape(B, nj, (G // nj) * P)], axis=2)
    return x.reshape(B // SUBLANES, SUBLANES, 2 * G * P)


def _s5_lanes_to_state(x, G, P):
    G8 = x.shape[0]
    nj = S5_LANE_GROUPS
    x = x.reshape(G8 * SUBLANES, nj, 2, G // nj, P)
    return x[:, :, 0].reshape(-1, G, P), x[:, :, 1].reshape(-1, G, P)


def _merge_kernel(has_pos, *refs):
    if has_pos:
        x_ref, pos_ref = refs[0], refs[1]
        refs = refs[2:]
    else:
        x_ref, pos_ref = refs[0], None
        refs = refs[1:]
    (ha_ref, hb_ref, sa_ref, sb_ref, mod_ref, nw_ref, wpa_ref, wpb_ref, wout_ref, wrh_ref, wrm_ref,
     br_ref, x1_ref, hp_ref, aff_ref) = refs
    x = x_ref[0]
    if has_pos:
        x = x + pos_ref[...]
    tn = (((0,), (0,)), ((), ()))
    dh = ha_ref.shape[3]
    pa = sum(lax.dot_general(ha_ref[0, hh, 0], wpa_ref[hh * dh:(hh + 1) * dh, :], tn,
                             preferred_element_type=F32) for hh in range(ha_ref.shape[1]))
    merged = sa_ref[0] * pa \
        + sb_ref[0] * jnp.dot(hb_ref[0], wpb_ref[...], preferred_element_type=F32)
    out = jnp.dot(merged.astype(BF16), wout_ref[...], preferred_element_type=F32)
    x1 = x + mod_ref[0, 2:3, :] * out
    h2 = _rms_mod(x1, nw_ref[...], mod_ref[0, 3:4, :], mod_ref[0, 4:5, :])
    tm = h2.shape[0]
    for sl in range(h2.shape[1] // LANES):
        hp_ref[0, pl.ds(sl, tm, stride=SUBLANES), :] = h2[:, sl * LANES:(sl + 1) * LANES]
        x1_ref[0, pl.ds(sl, tm, stride=SUBLANES), :] = x1[:, sl * LANES:(sl + 1) * LANES]
    h2h, h2m, _ = _split3(h2)
    logits = jnp.dot(h2h, wrh_ref[...], preferred_element_type=F32) \
        + jnp.dot(h2m, wrh_ref[...], preferred_element_type=F32) \
        + jnp.dot(h2h, wrm_ref[...], preferred_element_type=F32)
    lt = logits.T[0:N_EXPERTS, :] + br_ref[...]
    mx = jnp.max(lt, axis=0, keepdims=True)
    ex = jnp.exp(lt - mx)
    aff_ref[0] = ex / jnp.sum(ex, axis=0, keepdims=True)


def _merge(x, pos, ha, hb, sa, sb, mod, mod_row_fn, nw, wts, tm):
    B, T, D = x.shape
    has_pos = pos is not None
    wpa, wpb, wout, wrh, wrm, br = wts
    assert D == SUBLANES * LANES, D
    const = lambda shape: pl.BlockSpec(shape, lambda b, i: (0,) * len(shape),
                                       pipeline_mode=pl.Buffered(1))
    tok = lambda width: pl.BlockSpec((1, tm, width), lambda b, i: (b, i, 0))
    in_specs = [tok(D)]
    args = [x]
    if has_pos:
        in_specs.append(pl.BlockSpec((tm, D), lambda b, i: (i, 0)))
        args.append(pos)
    assert ha.shape[-1] == tm, (ha.shape, tm)
    in_specs += [pl.BlockSpec((1,) + ha.shape[1:2] + (1,) + ha.shape[3:], lambda b, i: (b, 0, i, 0, 0)),
                 tok(hb.shape[-1]), tok(D), tok(D),
                 pl.BlockSpec((1, N_MOD, D), lambda b, i: (mod_row_fn(b), 0, 0)),
                 const((1, D)), const(wpa.shape), const(wpb.shape), const(wout.shape),
                 const(wrh.shape), const(wrm.shape), const(br.shape)]
    args += [ha, hb, sa, sb, mod, nw, wpa, wpb, wout, wrh, wrm, br]
    return pl.pallas_call(
        functools.partial(_merge_kernel, has_pos),
        grid=(B, T // tm),
        in_specs=in_specs,
        out_specs=[pl.BlockSpec((1, tm * SUBLANES, LANES), lambda b, i: (b, i, 0)),
                   pl.BlockSpec((1, tm * SUBLANES, LANES), lambda b, i: (b, i, 0)),
                   pl.BlockSpec((1, N_EXPERTS, tm), lambda b, i: (b, 0, i))],
        out_shape=[jax.ShapeDtypeStruct((B, T * SUBLANES, LANES), F32),
                   jax.ShapeDtypeStruct((B, T * SUBLANES, LANES), F32),
                   jax.ShapeDtypeStruct((B, N_EXPERTS, T), F32)],
        compiler_params=_cparams(("arbitrary", "arbitrary")),
        name="merge",
    )(*args)


def _route_kernel(cap, aff_ref, idx_ref, val_ref, jb_s, jl_s):
    aff = aff_ref[0]
    E, T = aff.shape
    thr_bits = jnp.zeros((E, 1), jnp.int32)
    for bit in range(30, -1, -1):
        cand = thr_bits | (1 << bit)
        cnt = jnp.sum((aff >= lax.bitcast_convert_type(cand, F32)).astype(F32), axis=1, keepdims=True)
        thr_bits = jnp.where(cnt >= cap, cand, thr_bits)
    thr = lax.bitcast_convert_type(thr_bits, F32)
    nxt = lax.bitcast_convert_type(thr_bits + 1, F32)
    gt = aff >= nxt
    eq = (aff >= thr) & jnp.logical_not(gt)
    need = cap - jnp.sum(gt.astype(F32), axis=1, keepdims=True)

    r = lax.broadcasted_iota(jnp.int32, (LANES, LANES), 0)
    c = lax.broadcasted_iota(jnp.int32, (LANES, LANES), 1)
    tri = (r < c).astype(BF16)

    def excl_cumsum(mk):
        outs = []
        carry = jnp.zeros((E, 1), F32)
        for kb in range(T // LANES):
            blk = mk[:, kb * LANES:(kb + 1) * LANES]
            outs.append(jnp.dot(blk.astype(BF16), tri, preferred_element_type=F32) + carry)
            carry = carry + jnp.sum(blk.astype(F32), axis=1, keepdims=True)
        return jnp.concatenate(outs, axis=1)

    tie_rank = excl_cumsum(eq)
    mask = gt | (eq & (tie_rank < need))
    slot = jnp.where(mask, excl_cumsum(mask), -1.0).astype(jnp.int32)
    JL = min(cap, LANES)
    njb = cap // JL
    NC = 5
    WR = -(-(njb * NC) // SUBLANES) * SUBLANES
    jb_s[...] = slot >> (JL.bit_length() - 1)
    jl_s[...] = slot & (JL - 1)
    tio = lax.broadcasted_iota(jnp.int32, (1, T), 1)
    t_hi = (tio >> 6).astype(F32)
    t_lo = (tio & 63).astype(F32)
    jl_col = lax.broadcasted_iota(jnp.int32, (JL, 1), 0)
    wrow = lax.broadcasted_iota(jnp.int32, (WR, 1), 0)
    wjb = jnp.full((WR, 1), -2, jnp.int32)
    for jb in range(njb):
        wjb = jnp.where((wrow >= NC * jb) & (wrow < NC * (jb + 1)), jb, wjb)
    wcomp = wrow - NC * wjb
    lane = lax.broadcasted_iota(jnp.int32, (JL, E), 1)
    nt = (((1,), (1,)), ((), ()))

    def per_expert(e, carry):
        a1, a2, a3 = _split3(aff_ref[0, pl.ds(e, 1), :])
        comp = jnp.where(wcomp == 0, t_hi,
                         jnp.where(wcomp == 1, t_lo,
                                   jnp.where(wcomp == 2, a1.astype(F32),
                                             jnp.where(wcomp == 3, a2.astype(F32), a3.astype(F32)))))
        wt = jnp.where(jb_s[pl.ds(e, 1), :] == wjb, comp, 0.0).astype(BF16)
        onehot = jnp.where(jl_s[pl.ds(e, 1), :] == jl_col, 1.0, 0.0).astype(BF16)
        r = lax.dot_general(onehot, wt, nt, preferred_element_type=F32)
        out = []
        for jb in range(njb):
            c0 = jb * NC
            ie = r[:, c0:c0 + 1] * 64.0 + r[:, c0 + 1:c0 + 2]
            ve = r[:, c0 + 2:c0 + 3] + r[:, c0 + 3:c0 + 4] + r[:, c0 + 4:c0 + 5]
            out.append(jnp.where(lane == e, ie, carry[2 * jb]))
            out.append(jnp.where(lane == e, ve, carry[2 * jb + 1]))
        return tuple(out)

    res = lax.fori_loop(0, E, per_expert, tuple(jnp.zeros((JL, E), F32) for _ in range(2 * njb)),
                        unroll=2)
    for jb in range(njb):
        idx_ref[0, jb * JL:(jb + 1) * JL, :] = res[2 * jb].astype(jnp.int32)
        val_ref[0, jb * JL:(jb + 1) * JL, :] = res[2 * jb + 1]


def _route(aff, cap):
    B, E0, T = aff.shape
    R = max(1, min(B, ROUTE_ROW_ELEMS // (E0 * T)))
    E = R * E0
    idx, val = pl.pallas_call(
        functools.partial(_route_kernel, cap),
        grid=(B // R,),
        in_specs=[pl.BlockSpec((1, E, T), lambda b: (b, 0, 0))],
        out_specs=[pl.BlockSpec((1, cap, E), lambda b: (b, 0, 0)),
                   pl.BlockSpec((1, cap, E), lambda b: (b, 0, 0))],
        out_shape=[jax.ShapeDtypeStruct((B // R, cap, E), jnp.int32),
                   jax.ShapeDtypeStruct((B // R, cap, E), F32)],
        scratch_shapes=[pltpu.VMEM((E, T), jnp.int32), pltpu.VMEM((E, T), jnp.int32)],
        compiler_params=_cparams(("arbitrary",)),
        name="route",
    )(aff.reshape(B // R, E, T))
    unbatch = lambda a: a.reshape(B // R, cap, R, E0).transpose(0, 2, 1, 3).reshape(B, cap, E0)
    return unbatch(idx), unbatch(val)


def _moe_kernel(idx_ref, idxn_ref, val_ref, valn_ref, tok_ref, g2_ref, nw_ref, w1_ref, w3_ref, w2_ref, x1_ref, out_ref,
                tok_s, xe_s, xb_s, ye_s, yt_s, acc_s, yb_s, sem_tok, sem_in, sem_out):
    b, e, f = pl.program_id(0), pl.program_id(1), pl.program_id(2)
    nb, ne = pl.num_programs(0), pl.num_programs(1)
    cap = ye_s.shape[0]
    S = SUBLANES
    GR = MOE_ROW_GROUP
    nr = tok_s.shape[0]
    nt = nr // S

    def tok_copy(t):
        return pltpu.make_async_copy(tok_ref.at[t], tok_s, sem_tok)

    def x1_copy():
        return pltpu.make_async_copy(x1_ref.at[b], acc_s.at[pl.ds(0, nr), :], sem_in)

    def tile(ref, r):
        return ref.at[pl.ds(pl.multiple_of(r * S, S), S), :]

    def gather_group(iref, g):
        rows = [iref[0, 0, g * GR + i] for i in range(GR)]
        vals = [tile(tok_s, rows[i])[...] for i in range(GR)]
        for i in range(GR):
            tile(xe_s, g * GR + i)[...] = vals[i]

    def scatter_group(iref, vref, g, park=None):
        rows = [iref[0, 0, g * GR + i] for i in range(GR)]
        if park is not None:
            rows = [jnp.where(park, nt + i, rows[i]) for i in range(GR)]
        vals = [tile(acc_s, rows[i])[...] + tile(yt_s, g * GR + i)[...] * vref[0, 0, g * GR + i]
                for i in range(GR)]
        for i in range(GR):
            tile(acc_s, rows[i])[...] = vals[i]

    def ffn_half():
        xe = xb_s[...]
        h1 = jnp.dot(xe, w1_ref[0], preferred_element_type=F32)
        h3 = jnp.dot(xe, w3_ref[0], preferred_element_type=F32)
        hid = (h1 * _sigmoid(h1) * h3).astype(BF16)
        return jnp.dot(hid, w2_ref[0], preferred_element_type=F32)

    @pl.when((e == 0) & (f == 0))
    def _():
        @pl.when(b == 0)
        def _():
            tok_copy(b).start()
        x1_copy().start()
        yt_s[...] = jnp.zeros_like(yt_s)
        acc_s[pl.ds(nr, GR * S), :] = jnp.zeros((GR * S, LANES), F32)
        tok_copy(b).wait()
        lax.fori_loop(0, cap // GR, lambda g, c: (gather_group(idx_ref, g), c)[1], 0)

    @pl.when(f == 0)
    def _():
        xb_s[...] = jnp.concatenate([xe_s[pl.ds(sl, cap, stride=S), :] for sl in range(S)],
                                    axis=1).astype(BF16)
        ye_s[...] = ffn_half()
        for g in range(cap // GR):
            scatter_group(idxn_ref, valn_ref, g, park=(e == 0))

    @pl.when((e == 0) & (f == 1))
    def _():
        x1_copy().wait()

    @pl.when(f == 1)
    def _():
        yw = (ye_s[...] + ffn_half()) * g2_ref[0]
        for sl in range(S):
            yt_s[pl.ds(sl, cap, stride=S), :] = yw[:, sl * LANES:(sl + 1) * LANES]
        for g in range(cap // GR):
            gather_group(idxn_ref, g)

    @pl.when((e == ne - 1) & (f == 1))
    def _():
        lax.fori_loop(0, cap // GR, lambda g, c: (scatter_group(idx_ref, val_ref, g), c)[1], 0)

        @pl.when(b + 1 < nb)
        def _():
            tok_copy(b + 1).start()
        ch = yb_s.shape[1]
        nch = nr // (ch * S)
        copies = []
        for c in range(nch):
            slot = c % 2
            if c >= 2:
                copies[c - 2].wait()
            x = jnp.concatenate([acc_s[pl.ds(c * ch * S + sl, ch, stride=S), :] for sl in range(S)], axis=1)
            ms = jnp.mean(x * x, axis=-1, keepdims=True)
            yb_s[slot] = x * lax.rsqrt(ms + EPS) * nw_ref[...]
            cp = pltpu.make_async_copy(yb_s.at[slot], out_ref.at[b, pl.ds(c * ch, ch), :], sem_out.at[slot])
            cp.start()
            copies.append(cp)
        for c in range(max(nch - 2, 0), nch):
            copies[c].wait()


def _moe(idx, tok, x1, val, gate2, nw, w1, w3, w2):
    B, NR, _ = tok.shape
    E, D, FF = w1.shape
    cap = idx.shape[-1]
    nf = EXPERT_FF_SPLIT
    assert nf == 2, nf
    fh = FF // nf
    NT = NR // SUBLANES
    ch = min(MOE_OUT_CHUNK, NT)
    own = pl.BlockSpec((1, 1, cap), lambda b, e, f: (b * E + e, 0, 0), memory_space=pltpu.SMEM)
    nbr = pl.BlockSpec((1, 1, cap), lambda b, e, f: (b * E + jnp.clip(e - 1 + 2 * f, 0, E - 1), 0, 0),
                       memory_space=pltpu.SMEM)
    return pl.pallas_call(
        _moe_kernel,
        grid=(B, E, nf),
        in_specs=[own, nbr, own, nbr,
                  pl.BlockSpec(memory_space=pl.ANY),
                  pl.BlockSpec((1, 1, D), lambda b, e, f: (b, 0, 0)),
                  pl.BlockSpec((1, D), lambda b, e, f: (0, 0)),
                  pl.BlockSpec((1, D, fh), lambda b, e, f: (e, 0, f)),
                  pl.BlockSpec((1, D, fh), lambda b, e, f: (e, 0, f)),
                  pl.BlockSpec((1, fh, D), lambda b, e, f: (e, f, 0)),
                  pl.BlockSpec(memory_space=pl.ANY)],
        out_specs=pl.BlockSpec(memory_space=pl.ANY),
        out_shape=jax.ShapeDtypeStruct((B, NT, D), F32),
        scratch_shapes=[pltpu.VMEM((NR, LANES), F32),
                        pltpu.VMEM((cap * SUBLANES, LANES), F32), pltpu.VMEM((cap, D), BF16),
                        pltpu.VMEM((cap, D), F32), pltpu.VMEM((cap * SUBLANES, LANES), F32),
                        pltpu.VMEM((NR + MOE_ROW_GROUP * SUBLANES, LANES), F32), pltpu.VMEM((2, ch, D), F32),
                        pltpu.SemaphoreType.DMA(()), pltpu.SemaphoreType.DMA(()),
                        pltpu.SemaphoreType.DMA((2,))],
        compiler_params=_cparams(("arbitrary", "arbitrary", "arbitrary")),
        name="moe",
    )(idx, idx, val, val, tok, gate2, nw, w1, w3, w2, x1)


def _pos2d(T, D):
    rows = T // GRID_W
    quarter = D // 4
    freqs = 1.0 / (10000.0 ** (jnp.arange(quarter, dtype=F32) / quarter))
    er = jnp.arange(rows, dtype=F32)[:, None] * freqs
    ec = jnp.arange(GRID_W, dtype=F32)[:, None] * freqs
    rep = lambda a: jnp.repeat(a, GRID_W, axis=0)
    til = lambda a: jnp.tile(a, (rows, 1))
    return jnp.concatenate([rep(jnp.sin(er)), rep(jnp.cos(er)), til(jnp.sin(ec)), til(jnp.cos(ec))], axis=-1)


def _stream(x, pos, mod, mod_row_fn, P, mstate, s5_x0, tm):
    B, T, D = x.shape
    L = min(MLSTM_CHUNK, T)
    q, k, v, o, gc, gr, gtot, u, sa, sb = _inproj(x, pos, mod, mod_row_fn, P['norm_mix'],
                                                   P['w_in_parts'], tm, L)
    res = _mlstm(q, k, v, o, gc, gr, gtot, P['head_norm'], mstate, L)
    ha, mout = res[0], res[1:]
    Lc = min(S5_CHUNK, T)
    yf, xf = _s5_pass(False, u, None, s5_x0[0], *P['s5_dir'][0], None, Lc)
    hb, xb = _s5_pass(True, u, yf, s5_x0[1], *P['s5_dir'][1], P['glu'], Lc)
    x1, hp, aff = _merge(x, pos, ha, hb, sa, sb, mod, mod_row_fn, P['norm_ffn'], P['merge_w'], tm)
    cap = CAPACITY_FACTOR * T // N_EXPERTS
    idx, val = _route(aff, cap)
    E = N_EXPERTS
    gb = max(1, min(B, MOE_TABLE_ROWS // T))
    if any(mod_row_fn(r) != mod_row_fn(r - r % gb) for r in range(B)):
        gb = 1
    nb = B // gb
    offs = (jnp.arange(B, dtype=jnp.int32) % gb) * T
    idx_f = (idx + offs[:, None, None]).reshape(nb, gb, cap, E).transpose(0, 3, 1, 2).reshape(nb * E, 1, gb * cap)
    val_f = val.reshape(nb, gb, cap, E).transpose(0, 3, 1, 2).reshape(nb * E, 1, gb * cap)
    gate2 = jnp.stack([mod[mod_row_fn(t * gb), N_MOD - 1] for t in range(nb)])[:, None, :]
    tiles = lambda a: a.reshape(nb, gb * T * SUBLANES, LANES)
    y = _moe(idx_f, tiles(hp), tiles(x1), val_f, gate2, P['norm_final'], *P['experts']).reshape(B, T, D)
    return y, mout, (xf, xb)


def kernel(x_prompt, x_sample, state_mlstm_C, state_mlstm_n, state_mlstm_m, state_s5_re, state_s5_im,
           c, c_ctx, w_ada, b_ada, norm_mix, norm_ffn, w_in, b_gate, head_norm,
           s5_a_re, s5_a_im, s5_log_dt, s5_b_re, s5_b_im, s5_c_re, s5_c_im, s5_d, w_glu, b_glu,
           w_pA, w_pB, w_out, w_router, b_router, w_e1, w_e3, w_e2, norm_final):
    Bp, Tp, D = x_prompt.shape
    Bs, Ts, _ = x_sample.shape
    depth = w_ada.shape[0]
    H, dh = MLSTM_HEADS, MLSTM_HEAD_DIM
    W = H * dh
    G, Pn = s5_a_re.shape[2], s5_a_re.shape[3]
    SW = G * S5_GROUP

    assert depth == 1, depth
    l = 0
    pos = _pos2d(Ts, D)
    nrow = Bs + 1
    rows = -(-nrow // SUBLANES) * SUBLANES
    cpad = jnp.zeros((rows, D), F32).at[0:Bs].set(c).at[Bs].set(c_ctx)
    mod = _adaln(cpad, w_ada[l], b_ada[l]).reshape(rows, N_MOD, D)

    wi = w_in[l]
    gcols = wi[:, 4 * W:4 * W + 4 * H].reshape(D, 4, H).transpose(0, 2, 1).reshape(D, 4 * H)
    wug = jnp.concatenate([wi[:, 4 * W + 4 * H:4 * W + 4 * H + SW], gcols,
                           jnp.zeros((D, LANES - 4 * H), F32)], axis=1).astype(BF16)
    bg = jnp.concatenate([b_gate[l].T.reshape(1, 4 * H), jnp.zeros((1, LANES - 4 * H), F32)], axis=1)
    w_in_parts = (wi[:, 0:3 * W].astype(BF16), wi[:, 3 * W:4 * W].astype(BF16), wug,
                  wi[:, 4 * W + 4 * H + SW:].astype(BF16), bg)
    s5_dir = [_s5_discretise(s5_a_re[l, d], s5_a_im[l, d], s5_log_dt[l, d], s5_b_re[l], s5_b_im[l],
                             s5_c_re[l, d], s5_c_im[l, d]) for d in range(2)]
    s5_dir = [(jnp.broadcast_to(a, (SUBLANES, a.shape[1])), jnp.broadcast_to(s, (SUBLANES, s.shape[1])), bw, cw)
              for (a, s, bw, cw) in s5_dir]
    wr = jnp.concatenate([w_router[l], jnp.zeros((D, LANES - N_EXPERTS), F32)], axis=1)
    wr_hi = wr.astype(BF16)
    wr_mid = (wr - wr_hi.astype(F32)).astype(BF16)
    P = {
        'norm_mix': norm_mix[l].reshape(1, D), 'norm_ffn': norm_ffn[l].reshape(1, D),
        'norm_final': norm_final.reshape(1, D),
        'w_in_parts': w_in_parts, 'head_norm': head_norm[l], 's5_dir': s5_dir,
        'glu': (s5_d[l].reshape(1, SW), w_glu[l].astype(BF16), b_glu[l].reshape(1, SW)),
        'merge_w': (w_pA[l].astype(BF16), w_pB[l].astype(BF16), w_out[l].astype(BF16), wr_hi, wr_mid,
                    b_router[l].reshape(N_EXPERTS, 1)),
        'experts': (w_e1[l].astype(BF16), w_e3[l].astype(BF16), w_e2[l].astype(BF16)),
    }
    zero_x0 = jnp.zeros((Bp // SUBLANES, SUBLANES, 2 * G * Pn), F32)
    yp, mout, (xf, xb) = _stream(x_prompt, None, mod, lambda b: Bs, P, None, (zero_x0, zero_x0),
                                 min(MLSTM_CHUNK, Tp))
    lat_x0 = [_s5_state_to_lanes(state_s5_re[:, l, d], state_s5_im[:, l, d]) for d in range(2)]
    ys, _, _ = _stream(x_sample, pos, mod, lambda b: b, P,
                       (state_mlstm_C[:, l], state_mlstm_n[:, l], state_mlstm_m[:, l]), lat_x0,
                       min(MLSTM_CHUNK, Ts))
    Cn, nn, mn = mout
    fre, fim = _s5_lanes_to_state(xf, G, Pn)
    bre, bim = _s5_lanes_to_state(xb, G, Pn)
    outs = (Cn, nn.reshape(Bp, 2, H, dh), mn.reshape(Bp, 2, H),
            jnp.stack([fre, bre], axis=1), jnp.stack([fim, bim], axis=1))
    out_C, out_n, out_m, out_re, out_im = (o[:, None] for o in outs)
    return (yp, ys, out_C, out_n, out_m, out_re, out_im)
```
